```python
import math
import jax, jax.numpy as jnp
from jax import lax
import numpy as np

D_MODEL = 1024
BATCH = 16
SEQ = 4096
DEPTH = 1
DEC_BATCH = 8
DEC_SEQ = 8192
PAST_LEN = 128

GRID_W = 64
D_ATTN = D_MODEL // 2
D_HYENA = D_MODEL - D_ATTN
D_IN = 3 * D_ATTN + 3 * D_HYENA
NA_HEADS = 8
NA_HEAD_DIM = D_ATTN // NA_HEADS
WIN_H_MAX = 8
WIN_W = 16
Q_COL_BLOCK = 16
K_COL_BLOCK = Q_COL_BLOCK + WIN_W
N_COL_BLOCKS = GRID_W // Q_COL_BLOCK
HY_ORDER = 2
HY_SHORT = 3
HY_BANDS = 8
HY_EMB = 1 + 2 * HY_BANDS
HY_FILTER_FFN = 64
HY_FAST_DECAY = 0.3
HY_SLOW_DECAY = 1.5
HY_TARGET = 1e-2
D_FF = ((8 * D_MODEL // 3 + 127) // 128) * 128
N_MOD = 9
EPS = 1e-6
NEG_INF = -1e30

kernel_name = "hybrid_natten_hyena_macaron_encoder"


def rmsnorm(x, g):
    xf = x.astype(jnp.float32)
    xf = xf * lax.rsqrt(jnp.mean(xf * xf, axis=-1, keepdims=True) + EPS)
    return xf.astype(x.dtype) * g


def swiglu(x, w_gate, w_up, w_down):
    return (jax.nn.silu(x @ w_gate) * (x @ w_up)) @ w_down


def neighbourhood_attention(q, k, v, rpb):
    b, seq_len, n_heads, head_dim = q.shape
    rows = seq_len // GRID_W
    win_h = min(WIN_H_MAX, rows)
    r = np.arange(rows)
    row_idx = np.clip(r - win_h // 2, 0, rows - win_h)[:, None] + np.arange(win_h)[None, :]
    qcol = np.arange(GRID_W).reshape(N_COL_BLOCKS, Q_COL_BLOCK)
    kcol_start = np.clip(qcol[:, 0] - WIN_W // 2, 0, GRID_W - K_COL_BLOCK)
    col_idx = kcol_start[:, None] + np.arange(K_COL_BLOCK)[None, :]
    qwin_start = np.clip(qcol - WIN_W // 2, 0, GRID_W - WIN_W)
    kc = col_idx[:, None, :]
    col_ok = (kc >= qwin_start[..., None]) & (kc < qwin_start[..., None] + WIN_W)
    d_row = row_idx - r[:, None] + WIN_H_MAX - 1
    d_col = np.clip(kc - qcol[..., None] + WIN_W - 1, 0, 2 * WIN_W - 2)
    bias = rpb[:, d_row[:, None, None, :, None], d_col[None, :, :, None, :]].astype(jnp.float32)
    bias = jnp.where(col_ok[None, None, :, :, None, :], bias, NEG_INF)
    scale = head_dim ** -0.5
    qg = q.reshape(b, rows, N_COL_BLOCKS, Q_COL_BLOCK, n_heads, head_dim)
    kg = k.reshape(b, rows, GRID_W, n_heads, head_dim)
    vg = v.reshape(b, rows, GRID_W, n_heads, head_dim)
    ri = row_idx[:, None, :, None]
    ci = col_idx[None, :, None, :]

    def one_sequence(args):
        qs, ks, vs = args
        kw = ks[ri, ci]
        vw = vs[ri, ci]
        s = jnp.einsum("rjqhd,rjakhd->hrjqak", qs, kw, preferred_element_type=jnp.float32) * scale + bias
        p = jax.nn.softmax(s.reshape(s.shape[:4] + (-1,)), axis=-1).reshape(s.shape)
        return jnp.einsum("hrjqak,rjakhd->rjqhd", p.astype(vs.dtype), vw)

    out = lax.map(one_sequence, (qg, kg, vg))
    return out.reshape(b, seq_len, n_heads * head_dim)


def short_conv(x, w, bias):
    seq_len = x.shape[1]
    pad = HY_SHORT // 2
    xp = jnp.pad(x, ((0, 0), (pad, pad), (0, 0)))
    out = bias
    for i in range(HY_SHORT):
        out = out + xp[:, i:i + seq_len] * w[i]
    return out


def hyena_filters(seq_len, w1, b1, w2, b2, w3, b3, wo, freq):
    t = jnp.linspace(0.0, 1.0, seq_len, dtype=jnp.float32)[:, None]
    w = 2.0 * math.pi * jnp.arange(seq_len, dtype=jnp.float32)[:, None] / seq_len
    f = jnp.linspace(1e-4, HY_BANDS - 1, HY_BANDS, dtype=jnp.float32)[None, :]
    z = jnp.concatenate([t, jnp.cos(f * w), -jnp.sin(f * w)], axis=-1)
    h = jnp.sin(freq * (z @ w1 + b1))
    h = jnp.sin(freq * (h @ w2 + b2))
    h = jnp.sin(freq * (h @ w3 + b3))
    h = (h @ wo).astype(jnp.float32).reshape(seq_len, 2, HY_ORDER, D_HYENA)
    deltas = jnp.abs(jnp.linspace(math.log(HY_TARGET) / HY_FAST_DECAY,
                                  math.log(HY_TARGET) / HY_SLOW_DECAY, D_HYENA, dtype=jnp.float32))
    h = h * jnp.exp(-t * deltas)[:, None, None, :]
    h_fwd = h[:, 0]
    h_bwd = h[:-1, 1]
    l1 = jnp.sum(jnp.abs(h_fwd), axis=0) + jnp.sum(jnp.abs(h_bwd), axis=0)
    return h_fwd / l1, h_bwd / l1


def two_sided_fftconv(u, h_fwd, h_bwd):
    seq_len, ch = h_fwd.shape
    k = jnp.concatenate([h_fwd, jnp.zeros((1, ch), jnp.float32), h_bwd[::-1]], axis=0)
    k_f = jnp.fft.rfft(k, axis=0)
    u_f = jnp.fft.rfft(u.astype(jnp.float32), n=2 * seq_len, axis=1)
    y = jnp.fft.irfft(u_f * k_f[None], n=2 * seq_len, axis=1)[:, :seq_len]
    return y.astype(u.dtype)


def hyena_mixer(u, conv_w, conv_b, w1, b1, w2, b2, w3, b3, wo, freq, skip):
    seq_len = u.shape[1]
    u = short_conv(u, conv_w, conv_b)
    parts = jnp.split(u, HY_ORDER + 1, axis=-1)
    h_fwd, h_bwd = hyena_filters(seq_len, w1, b1, w2, b2, w3, b3, wo, freq)
    z = parts[0]
    for n in range(HY_ORDER):
        z = parts[n + 1] * (two_sided_fftconv(z, h_fwd[:, n], h_bwd[:, n]) + skip[n] * z)
    return z


def encoder_layer(x, c, p):
    b, seq_len, _ = x.shape
    mod = (jax.nn.silu(c) @ p["w_ada"] + p["b_ada"]).reshape(b, N_MOD, 1, D_MODEL)
    sh1, sc1, g1, sh2, sc2, g2, sh3, sc3, g3 = (mod[:, i] for i in range(N_MOD))
    hn = rmsnorm(x, p["ffn1_norm"]) * (1.0 + sc1) + sh1
    x = x + 0.5 * g1 * swiglu(hn, p["ffn1_w_gate"], p["ffn1_w_up"], p["ffn1_w_down"])
    hn = rmsnorm(x, p["mix_norm"]) * (1.0 + sc2) + sh2
    proj = hn @ p["w_in"]
    q, k, v, hy_in = jnp.split(proj, [D_ATTN, 2 * D_ATTN, 3 * D_ATTN], axis=-1)
    shp = (b, seq_len, NA_HEADS, NA_HEAD_DIM)
    attn = neighbourhood_attention(q.reshape(shp), k.reshape(shp), v.reshape(shp), p["na_rpb"])
    hy = hyena_mixer(hy_in, p["hy_conv_w"], p["hy_conv_b"], p["hy_w1"], p["hy_b1"], p["hy_w2"],
                     p["hy_b2"], p["hy_w3"], p["hy_b3"], p["hy_wo"], p["hy_sin_freq"], p["hy_skip"])
    mixed = jnp.concatenate([rmsnorm(attn, p["attn_out_norm"]), rmsnorm(hy, p["hy_out_norm"])], axis=-1)
    x = x + g2 * (mixed @ p["w_out"])
    hn = rmsnorm(x, p["ffn2_norm"]) * (1.0 + sc3) + sh3
    x = x + 0.5 * g3 * swiglu(hn, p["ffn2_w_gate"], p["ffn2_w_up"], p["ffn2_w_down"])
    return x


def setup_inputs(seed: int = 0) -> dict:
    key = jax.random.key(seed)
    ks = iter(jax.random.split(key, 48))

    def normal(shape, scale):
        return jax.random.normal(next(ks), shape, jnp.float32) * scale

    def gain(shape):
        return 1.0 + 0.05 * jax.random.normal(next(ks), shape, jnp.float32)

    L = DEPTH
    return {
        "x_prompt": normal((BATCH, SEQ, D_MODEL), 1.0),
        "x_sample": normal((DEC_BATCH, DEC_SEQ, D_MODEL), 1.0),
        "c_prompt": normal((BATCH, D_MODEL), 1.0),
        "c_sample": normal((DEC_BATCH, D_MODEL), 1.0),
        "w_ada": normal((L, D_MODEL, N_MOD * D_MODEL), 0.5 * D_MODEL ** -0.5),
        "b_ada": normal((L, N_MOD * D_MODEL), 0.02),
        "ffn1_norm": gain((L, D_MODEL)),
        "ffn1_w_gate": normal((L, D_MODEL, D_FF), D_MODEL ** -0.5),
        "ffn1_w_up": normal((L, D_MODEL, D_FF), D_MODEL ** -0.5),
        "ffn1_w_down": normal((L, D_FF, D_MODEL), D_FF ** -0.5),
        "mix_norm": gain((L, D_MODEL)),
        "w_in": normal((L, D_MODEL, D_IN), D_MODEL ** -0.5),
        "na_rpb": normal((L, NA_HEADS, 2 * WIN_H_MAX - 1, 2 * WIN_W - 1), 0.1),
        "hy_conv_w": normal((L, HY_SHORT, 3 * D_HYENA), HY_SHORT ** -0.5),
        "hy_conv_b": normal((L, 3 * D_HYENA), 0.02),
        "hy_w1": normal((L, HY_EMB, HY_FILTER_FFN), HY_EMB ** -0.5),
        "hy_b1": normal((L, HY_FILTER_FFN), 0.1),
        "hy_w2": normal((L, HY_FILTER_FFN, HY_FILTER_FFN), HY_FILTER_FFN ** -0.5),
        "hy_b2": normal((L, HY_FILTER_FFN), 0.1),
        "hy_w3": normal((L, HY_FILTER_FFN, HY_FILTER_FFN), HY_FILTER_FFN ** -0.5),
        "hy_b3": normal((L, HY_FILTER_FFN), 0.1),
        "hy_wo": normal((L, HY_FILTER_FFN, 2 * HY_ORDER * D_HYENA), HY_FILTER_FFN ** -0.5),
        "hy_sin_freq": gain((L, HY_FILTER_FFN)),
        "hy_skip": normal((L, HY_ORDER, D_HYENA), 0.5),
        "attn_out_norm": gain((L, D_ATTN)),
        "hy_out_norm": gain((L, D_HYENA)),
        "w_out": normal((L, D_MODEL, D_MODEL), D_MODEL ** -0.5),
        "ffn2_norm": gain((L, D_MODEL)),
        "ffn2_w_gate": normal((L, D_MODEL, D_FF), D_MODEL ** -0.5),
        "ffn2_w_up": normal((L, D_MODEL, D_FF), D_MODEL ** -0.5),
        "ffn2_w_down": normal((L, D_FF, D_MODEL), D_FF ** -0.5),
        "final_norm": gain((D_MODEL,)),
    }


def reference(x_prompt, x_sample, c_prompt, c_sample, w_ada, b_ada, ffn1_norm, ffn1_w_gate,
              ffn1_w_up, ffn1_w_down, mix_norm, w_in, na_rpb, hy_conv_w, hy_conv_b, hy_w1, hy_b1,
              hy_w2, hy_b2, hy_w3, hy_b3, hy_wo, hy_sin_freq, hy_skip, attn_out_norm, hy_out_norm,
              w_out, ffn2_norm, ffn2_w_gate, ffn2_w_up, ffn2_w_down, final_norm):
    def trunk(x, c):
        for i in range(DEPTH):
            p = {
                "w_ada": w_ada[i], "b_ada": b_ada[i],
                "ffn1_norm": ffn1_norm[i], "ffn1_w_gate": ffn1_w_gate[i],
                "ffn1_w_up": ffn1_w_up[i], "ffn1_w_down": ffn1_w_down[i],
                "mix_norm": mix_norm[i], "w_in": w_in[i], "na_rpb": na_rpb[i],
                "hy_conv_w": hy_conv_w[i], "hy_conv_b": hy_conv_b[i],
                "hy_w1": hy_w1[i], "hy_b1": hy_b1[i], "hy_w2": hy_w2[i], "hy_b2": hy_b2[i],
                "hy_w3": hy_w3[i], "hy_b3": hy_b3[i], "hy_wo": hy_wo[i],
                "hy_sin_freq": hy_sin_freq[i], "hy_skip": hy_skip[i],
                "attn_out_norm": attn_out_norm[i], "hy_out_norm": hy_out_norm[i], "w_out": w_out[i],
                "ffn2_norm": ffn2_norm[i], "ffn2_w_gate": ffn2_w_gate[i],
                "ffn2_w_up": ffn2_w_up[i], "ffn2_w_down": ffn2_w_down[i],
            }
            x = encoder_layer(x, c, p)
        return rmsnorm(x, final_norm)

    y_prompt = trunk(x_prompt, c_prompt)
    y_sample = trunk(x_sample, c_sample)
    return (y_prompt, y_sample)
```

```python
import functools
import math

import ml_dtypes
import numpy as np
import jax
import jax.numpy as jnp
from jax import lax
from jax.experimental import pallas as pl
from jax.experimental.pallas import tpu as pltpu

F32 = jnp.float32
BF16 = jnp.bfloat16
HIGHEST = lax.Precision.HIGHEST

D_MODEL = 1024
GRID_W = 64
D_MIX = 512
NA_HEADS = 8
NA_HEAD_DIM = D_MIX // NA_HEADS
WIN_H = 8
WIN_W = 16
HY_ORDER = 2
HY_BANDS = 8
HY_EMB = 1 + 2 * HY_BANDS
HY_FAST_DECAY = 0.3
HY_SLOW_DECAY = 1.5
HY_TARGET = 1e-2
D_FF = ((8 * D_MODEL // 3 + 127) // 128) * 128
N_MOD = 9
EPS = 1e-6
NEG_INF = -1e30

V7X_VMEM_LIMIT_BYTES = 56 * 1024 * 1024
TOKEN_TILE = 512
FF_CHUNK = D_FF // 2
HALO = 16
LANE_BLOCK = 128
FFT_CT = 256
FFT_NB = 8
FFT_KB = 8
FILT_TILE = 512
FILT_PAD = 128


def _cparams(n_axes):
    return pltpu.CompilerParams(
        dimension_semantics=("arbitrary",) * n_axes,
        vmem_limit_bytes=V7X_VMEM_LIMIT_BYTES,
    )


def _rms(x, gain):
    ms = jnp.mean(x * x, axis=-1, keepdims=True)
    return x * lax.rsqrt(ms + EPS) * gain


def _silu(x):
    return x / (1.0 + jnp.exp(-x))


SLABS = LANE_BLOCK // FFT_NB
LANES = 128


def _store_slabs(ref, tile):
    for i1 in range(tile.shape[0] // LANE_BLOCK):
        for t in range(tile.shape[1] // LANES):
            rows = tile[i1 * LANE_BLOCK:(i1 + 1) * LANE_BLOCK, t * LANES:(t + 1) * LANES]
            ref[:, t, i1 * FFT_NB:(i1 + 1) * FFT_NB, :] = rows.reshape(SLABS, FFT_NB, LANES)


def _load_slabs(ref):
    _, tiles, rows, _ = ref.shape
    return jnp.concatenate(
        [jnp.concatenate([ref[:, t, i1 * FFT_NB:(i1 + 1) * FFT_NB, :].reshape(LANE_BLOCK, LANES)
                          for t in range(tiles)], axis=1)
         for i1 in range(rows // FFT_NB)], axis=0)


def _load_strided(ref, lead, start, size):
    tiles = ref.shape[len(lead)]
    return jnp.concatenate([ref[lead + (t, pl.ds(start, size, stride=FFT_NB), slice(None))]
                            for t in range(tiles)], axis=1)


def _store_strided(ref, start, val):
    for t in range(ref.shape[0]):
        ref[t, pl.ds(start, val.shape[0], stride=FFT_NB), :] = val[:, t * LANES:(t + 1) * LANES]


def _ada_kernel(c_ref, w_ref, b_ref, o_ref):
    s = _silu(c_ref[...])
    o_ref[...] = jnp.dot(s, w_ref[...], precision=HIGHEST, preferred_element_type=F32) + b_ref[...]


def _ada(c_all, w_ada, b_ada):
    rows = c_all.shape[0]
    n_out = w_ada.shape[1]
    tn = D_MODEL
    return pl.pallas_call(
        _ada_kernel,
        out_shape=jax.ShapeDtypeStruct((rows, n_out), F32),
        grid=(n_out // tn,),
        in_specs=[
            pl.BlockSpec((rows, D_MODEL), lambda j: (0, 0)),
            pl.BlockSpec((D_MODEL, tn), lambda j: (0, j)),
            pl.BlockSpec((1, tn), lambda j: (0, j)),
        ],
        out_specs=pl.BlockSpec((rows, tn), lambda j: (0, j)),
        compiler_params=_cparams(1),
        name="ada_mod",
    )(c_all, w_ada, b_ada.reshape(1, n_out))


def _ffn_kernel(x_ref, mod_ref, gain_ref, wg_ref, wu_ref, wd_ref, *rest, mod_base, final):
    if final:
        fn_ref, o_ref = rest
    else:
        (o_ref,) = rest
    x = x_ref[...]
    shift = mod_ref[mod_base:mod_base + 1, :]
    scale = mod_ref[mod_base + 1:mod_base + 2, :]
    gate = mod_ref[mod_base + 2:mod_base + 3, :]
    hb = (_rms(x, gain_ref[...]) * (1.0 + scale) + shift).astype(BF16)
    acc = None
    for c0 in range(0, D_FF, FF_CHUNK):
        g = jnp.dot(hb, wg_ref[:, c0:c0 + FF_CHUNK], preferred_element_type=F32)
        u = jnp.dot(hb, wu_ref[:, c0:c0 + FF_CHUNK], preferred_element_type=F32)
        a = (_silu(g) * u).astype(BF16)
        d = jnp.dot(a, wd_ref[c0:c0 + FF_CHUNK, :], preferred_element_type=F32)
        acc = d if acc is None else acc + d
    y = x + 0.5 * gate * acc
    if final:
        y = _rms(y, fn_ref[...])
    o_ref[...] = y


def _ffn(x2d, mod, gain, wg, wu, wd, seq_len, mod_base, final_gain=None):
    t = x2d.shape[0]
    tm = TOKEN_TILE
    final = final_gain is not None
    resident = lambda shape: pl.BlockSpec(shape, lambda i: (0, 0), pipeline_mode=pl.Buffered(1))
    in_specs = [
        pl.BlockSpec((tm, D_MODEL), lambda i: (i, 0)),
        pl.BlockSpec((None, N_MOD, D_MODEL), lambda i: (i * tm // seq_len, 0, 0)),
        resident((1, D_MODEL)),
        resident((D_MODEL, D_FF)),
        resident((D_MODEL, D_FF)),
        resident((D_FF, D_MODEL)),
    ]
    args = [x2d, mod, gain.reshape(1, D_MODEL), wg, wu, wd]
    if final:
        in_specs.append(resident((1, D_MODEL)))
        args.append(final_gain.reshape(1, D_MODEL))
    return pl.pallas_call(
        functools.partial(_ffn_kernel, mod_base=mod_base, final=final),
        out_shape=jax.ShapeDtypeStruct((t, D_MODEL), F32),
        grid=(t // tm,),
        in_specs=in_specs,
        out_specs=pl.BlockSpec((tm, D_MODEL), lambda i: (i, 0)),
        compiler_params=_cparams(1),
        name="ffn_final" if final else "ffn",
    )(*args)


def _inproj_kernel(x_ref, xp_ref, xn_ref, mod_ref, gain_ref, wqkv_ref, why_ref, cw_ref, cb_ref,
                   q_ref, k_ref, v_ref, hv_ref, hx1_ref, hx2_ref, ext_ref, u_ref,
                   *, tiles_per_seq, tm):
    pos = pl.program_id(0) % tiles_per_seq
    gain = gain_ref[...]
    shift = mod_ref[3:4, :]
    scale = 1.0 + mod_ref[4:5, :]

    def normed(x):
        return _rms(x, gain) * scale + shift

    hb = normed(x_ref[...]).astype(BF16)
    qkv = jnp.dot(hb, wqkv_ref[...], preferred_element_type=F32)
    q_ref[...] = (qkv[:, :D_MIX] * (NA_HEAD_DIM ** -0.5)).astype(BF16)
    k_ref[...] = qkv[:, D_MIX:2 * D_MIX].astype(BF16)
    v_ref[...] = qkv[:, 2 * D_MIX:].astype(BF16)

    has_prev = jnp.where(pos != 0, 1.0, 0.0)
    has_next = jnp.where(pos != tiles_per_seq - 1, 1.0, 0.0)
    ext_ref[0:HALO, :] = (normed(xp_ref[...]) * has_prev).astype(BF16)
    ext_ref[HALO:HALO + tm, :] = hb
    ext_ref[HALO + tm:, :] = (normed(xn_ref[...]) * has_next).astype(BF16)
    u_ref[...] = jnp.dot(ext_ref[...], why_ref[...], preferred_element_type=F32)
    conv = (cb_ref[...]
            + u_ref[pl.ds(HALO - 1, tm), :] * cw_ref[0:1, :]
            + u_ref[pl.ds(HALO, tm), :] * cw_ref[1:2, :]
            + u_ref[pl.ds(HALO + 1, tm), :] * cw_ref[2:3, :])
    for part, ref in enumerate((hv_ref, hx1_ref, hx2_ref)):
        _store_slabs(ref, conv[:, part * D_MIX:(part + 1) * D_MIX])


def _inproj(x2d, mod, gain, w_qkv, w_hy, conv_w, conv_b, seq_len):
    t = x2d.shape[0]
    tm = TOKEN_TILE
    n_halo_blocks = t // HALO
    per_tile = tm // HALO
    const = lambda shape: pl.BlockSpec(shape, lambda i: (0, 0))
    tiles_per_seq = seq_len // tm
    tile_out = jax.ShapeDtypeStruct((t, D_MIX), BF16)
    tile_spec = pl.BlockSpec((tm, D_MIX), lambda i: (i, 0))
    slab_rows = tm // LANE_BLOCK * FFT_NB
    slab_out = jax.ShapeDtypeStruct((t // seq_len, SLABS, D_MIX // LANES, seq_len // SLABS, LANES), F32)
    slab_spec = pl.BlockSpec((None, SLABS, D_MIX // LANES, slab_rows, LANES),
                             lambda i: (i // tiles_per_seq, 0, 0, i % tiles_per_seq, 0))
    return pl.pallas_call(
        functools.partial(_inproj_kernel, tiles_per_seq=tiles_per_seq, tm=tm),
        out_shape=[tile_out] * 3 + [slab_out] * 3,
        grid=(t // tm,),
        in_specs=[
            pl.BlockSpec((tm, D_MODEL), lambda i: (i, 0)),
            pl.BlockSpec((HALO, D_MODEL), lambda i: (jnp.maximum(i * per_tile - 1, 0), 0)),
            pl.BlockSpec((HALO, D_MODEL),
                         lambda i: (jnp.minimum((i + 1) * per_tile, n_halo_blocks - 1), 0)),
            pl.BlockSpec((None, N_MOD, D_MODEL), lambda i: (i * tm // seq_len, 0, 0)),
            const((1, D_MODEL)),
            const((D_MODEL, 3 * D_MIX)),
            const((D_MODEL, 3 * D_MIX)),
            const((3, 3 * D_MIX)),
            const((1, 3 * D_MIX)),
        ],
        out_specs=[tile_spec] * 3 + [slab_spec] * 3,
        scratch_shapes=[
            pltpu.VMEM((tm + 2 * HALO, D_MODEL), BF16),
            pltpu.VMEM((tm + 2 * HALO, 3 * D_MIX), F32),
        ],
        compiler_params=_cparams(1),
        name="inproj",
    )(x2d, x2d, x2d, mod, gain.reshape(1, D_MODEL), w_qkv, w_hy, conv_w,
      conv_b.reshape(1, 3 * D_MIX))


ROW_TOKENS = GRID_W
GROUP_ROWS = WIN_H
GROUP_TOKENS = GROUP_ROWS * ROW_TOKENS
WIN_TOKENS = WIN_H * ROW_TOKENS


def _attn_bias_table(rpb):
    o = np.arange(WIN_H)[:, None]
    a = np.arange(WIN_H)[None, :]
    d_row = o + a
    qc = np.arange(GRID_W)[:, None]
    kc = np.arange(GRID_W)[None, :]
    win_start = np.clip(qc - WIN_W // 2, 0, GRID_W - WIN_W)
    col_ok = (kc >= win_start) & (kc < win_start + WIN_W)
    d_col = np.clip(kc - qc + WIN_W - 1, 0, 2 * WIN_W - 2)
    bias = rpb[:, d_row[:, :, None, None], d_col[None, None, :, :]].astype(F32)
    bias = jnp.where(col_ok[None, None, None], bias, NEG_INF)
    bias = jnp.transpose(bias, (1, 0, 3, 2, 4))
    return bias.reshape(WIN_H, NA_HEADS, GRID_W, WIN_TOKENS)


def _attn_kernel(q_ref, kp_ref, kc_ref, kn_ref, vp_ref, vc_ref, vn_ref, bias_ref, gain_ref,
                 o_ref, kcat, vcat, acc, *, rows):
    g = pl.program_id(1)
    kcat[0:GROUP_TOKENS, :] = kp_ref[...]
    kcat[GROUP_TOKENS:2 * GROUP_TOKENS, :] = kc_ref[...]
    kcat[2 * GROUP_TOKENS:, :] = kn_ref[...]
    vcat[0:GROUP_TOKENS, :] = vp_ref[...]
    vcat[GROUP_TOKENS:2 * GROUP_TOKENS, :] = vc_ref[...]
    vcat[2 * GROUP_TOKENS:, :] = vn_ref[...]
    first_head = lax.broadcasted_iota(jnp.int32, (ROW_TOKENS, 2 * NA_HEAD_DIM), 1) < NA_HEAD_DIM

    def row_body(rr, carry):
        r = g * GROUP_ROWS + rr
        start = jnp.clip(r - WIN_H // 2, 0, rows - WIN_H)
        koff = pl.multiple_of((start - g * GROUP_ROWS + GROUP_ROWS) * ROW_TOKENS, ROW_TOKENS)
        row_class = start - r + WIN_H - 1
        qoff = pl.multiple_of(rr * ROW_TOKENS, ROW_TOKENS)
        for hp in range(NA_HEADS // 2):
            lanes = slice(hp * 2 * NA_HEAD_DIM, (hp + 1) * 2 * NA_HEAD_DIM)
            q2 = q_ref[pl.ds(qoff, ROW_TOKENS), lanes]
            kw = kcat[pl.ds(koff, WIN_TOKENS), lanes]
            vw = vcat[pl.ds(koff, WIN_TOKENS), lanes]
            outs = []
            for hh in range(2):
                keep = first_head if hh == 0 else jnp.logical_not(first_head)
                qm = jnp.where(keep, q2, jnp.zeros_like(q2))
                s = lax.dot_general(qm, kw, (((1,), (1,)), ((), ())), preferred_element_type=F32)
                s = s + bias_ref[row_class, 2 * hp + hh]
                m = jnp.max(s, axis=-1, keepdims=True)
                p = jnp.exp(s - m)
                denom = jnp.sum(p, axis=-1, keepdims=True)
                pv = jnp.dot(p.astype(BF16), vw, preferred_element_type=F32)
                outs.append(pv / denom)
            acc[pl.ds(qoff, ROW_TOKENS), lanes] = jnp.where(first_head, outs[0], outs[1])
        return carry

    lax.fori_loop(0, GROUP_ROWS, row_body, 0)
    o_ref[...] = _rms(acc[...], gain_ref[...]).astype(BF16)


def _attention(q, k, v, bias, gain, batch, seq_len):
    rows = seq_len // GRID_W
    groups = rows // GROUP_ROWS
    q3 = q.reshape(batch, seq_len, D_MIX)
    k3 = k.reshape(batch, seq_len, D_MIX)
    v3 = v.reshape(batch, seq_len, D_MIX)
    blk = (None, GROUP_TOKENS, D_MIX)
    cur = pl.BlockSpec(blk, lambda b, g: (b, g, 0))
    prev = pl.BlockSpec(blk, lambda b, g: (b, jnp.maximum(g - 1, 0), 0))
    nxt = pl.BlockSpec(blk, lambda b, g: (b, jnp.minimum(g + 1, groups - 1), 0))
    out = pl.pallas_call(
        functools.partial(_attn_kernel, rows=rows),
        out_shape=jax.ShapeDtypeStruct((batch, seq_len, D_MIX), BF16),
        grid=(batch, groups),
        in_specs=[
            cur, prev, cur, nxt, prev, cur, nxt,
            pl.BlockSpec((WIN_H, NA_HEADS, GRID_W, WIN_TOKENS), lambda b, g: (0, 0, 0, 0),
                         pipeline_mode=pl.Buffered(1)),
            pl.BlockSpec((1, D_MIX), lambda b, g: (0, 0)),
        ],
        out_specs=cur,
        scratch_shapes=[
            pltpu.VMEM((3 * GROUP_TOKENS, D_MIX), BF16),
            pltpu.VMEM((3 * GROUP_TOKENS, D_MIX), BF16),
            pltpu.VMEM((GROUP_TOKENS, D_MIX), F32),
        ],
        compiler_params=_cparams(2),
        name="natten",
    )(q3, k3, k3, k3, v3, v3, v3, bias, gain.reshape(1, D_MIX))
    return out.reshape(batch * seq_len, D_MIX)


def _split_hi_lo(x):
    hi = x.astype(ml_dtypes.bfloat16)
    lo = (x - hi.astype(np.float64)).astype(ml_dtypes.bfloat16)
    return hi, lo


def _stack_hi_lo(m):
    hi, lo = _split_hi_lo(m)
    return np.concatenate([hi, lo], axis=-2)


def _embed(re, im):
    return np.concatenate([np.concatenate([re, -im], axis=-1),
                           np.concatenate([im, re], axis=-1)], axis=-2)


@functools.lru_cache(maxsize=None)
def _fft_tables(seq_len):
    n = 2 * seq_len
    n2 = LANE_BLOCK
    n1 = n // n2
    i2 = np.arange(n2)[:, None, None]
    k1 = np.arange(n1)[None, :, None]
    i1 = np.arange(n1 // 2)[None, None, :]
    ang = -2.0 * np.pi * ((k1 * (n2 * i1 + i2)) % n) / n
    gr, gi = np.cos(ang), np.sin(ang)
    g_fwd = _stack_hi_lo(_embed(gr, gi))
    g_inv = _stack_hi_lo(_embed(np.swapaxes(gr, 1, 2) / n, -np.swapaxes(gi, 1, 2) / n))
    jk = np.outer(np.arange(n2), np.arange(n2))
    ang2 = -2.0 * np.pi * (jk % n2) / n2
    fr, fi = np.cos(ang2), np.sin(ang2)
    f_fwd = _stack_hi_lo(_embed(fr, fi))
    f_inv = _stack_hi_lo(_embed(fr, -fi))
    return n1, n2, g_fwd, g_inv, f_fwd, f_inv


def _dft3(m_hl, x, m):
    x_hi = x.astype(BF16)
    x_lo = (x - x_hi.astype(F32)).astype(BF16)
    t = jnp.dot(m_hl, x_hi, preferred_element_type=F32)
    return t[:m] + t[m:] + jnp.dot(m_hl[:m], x_lo, preferred_element_type=F32)


def _stage_a_forward(x_ref, g_ref, a_ref, *, n1):
    for i in range(FFT_NB):
        x = jnp.concatenate([_load_strided(x_ref, (0,), i, n1 // 2),
                             _load_strided(x_ref, (1,), i, n1 // 2)], axis=0)
        _store_strided(a_ref, i, _dft3(g_ref[i], x, 2 * n1))


def _stage_a_inverse(d_ref, gi_ref, y_ref, *, n1):
    for i in range(FFT_NB):
        _store_strided(y_ref, i, _dft3(gi_ref[i], _load_strided(d_ref, (), i, 2 * n1), n1))


def _k1_kernel(x_ref, g_ref, a_ref, *, n1):
    _stage_a_forward(x_ref, g_ref, a_ref, n1=n1)


FFT_TILES = FFT_CT // LANES


def _seq_spec(n1):
    return pl.BlockSpec((None, 2, None, FFT_TILES, n1 // 2 * FFT_NB, LANES),
                        lambda c, j, p: (p, 0, j, c, 0, 0))


def _spec_spec(n1):
    return pl.BlockSpec((None, None, FFT_TILES, 2 * n1 * FFT_NB, LANES), lambda c, j, p: (p, j, c, 0, 0))


def _fft_stage_a(x6, g_fwd, n1):
    pairs, _, slabs, tiles, _, _ = x6.shape
    return pl.pallas_call(
        functools.partial(_k1_kernel, n1=n1),
        out_shape=jax.ShapeDtypeStruct((pairs, slabs, tiles, 2 * n1 * FFT_NB, LANES), F32),
        grid=(tiles // FFT_TILES, slabs, pairs),
        in_specs=[
            _seq_spec(n1),
            pl.BlockSpec((FFT_NB, 4 * n1, n1), lambda c, j, p: (j, 0, 0)),
        ],
        out_specs=_spec_spec(n1),
        compiler_params=_cparams(3),
        name="hy_stage_a",
    )(x6, g_fwd)


def _load_low_index(ref, part, kk):
    return jnp.concatenate([ref[:, t, part, kk].reshape(LANE_BLOCK, LANES) for t in range(ref.shape[1])],
                           axis=1)


def _store_low_index(ref, part, kk, val):
    for t in range(ref.shape[1]):
        ref[:, t, part, kk] = val[:, t * LANES:(t + 1) * LANES].reshape(SLABS, FFT_NB, LANES)


def _k2_kernel(a_ref, kf_ref, f_ref, fi_ref, d_ref, *, kb):
    n2 = LANE_BLOCK
    f_hl = f_ref[...]
    fi_hl = fi_ref[...]

    def body(kk, carry):
        x = jnp.concatenate([_load_low_index(a_ref, 0, kk), _load_low_index(a_ref, 1, kk)], axis=0)
        c = _dft3(f_hl, x, 2 * n2)
        cr, ci = c[:n2], c[n2:]
        kr, ki = kf_ref[0, kk], kf_ref[1, kk]
        y = jnp.concatenate([cr * kr - ci * ki, cr * ki + ci * kr], axis=0)
        d = _dft3(fi_hl, y, 2 * n2)
        _store_low_index(d_ref, 0, kk, d[:n2])
        _store_low_index(d_ref, 1, kk, d[n2:])
        return carry

    lax.fori_loop(0, kb, body, 0)


def _fft_stage_c(a4, kf, f_fwd, f_inv, order, n1):
    pairs, slabs, tiles, _, _ = a4.shape
    n2 = LANE_BLOCK
    kb = FFT_KB
    ch_blocks = tiles // FFT_TILES
    a7 = a4.reshape(pairs, slabs, tiles, 2, n1, FFT_NB, LANES)
    spec = pl.BlockSpec((None, slabs, FFT_TILES, 2, kb, FFT_NB, LANES),
                        lambda c, k, p: (p, 0, c, 0, k, 0, 0))
    d7 = pl.pallas_call(
        functools.partial(_k2_kernel, kb=kb),
        out_shape=jax.ShapeDtypeStruct(a7.shape, F32),
        grid=(ch_blocks, n1 // kb, pairs),
        in_specs=[
            spec,
            pl.BlockSpec((2, kb, n2, FFT_CT), lambda c, k, p: (0, k, 0, order * ch_blocks + c)),
            pl.BlockSpec((4 * n2, 2 * n2), lambda c, k, p: (0, 0)),
            pl.BlockSpec((4 * n2, 2 * n2), lambda c, k, p: (0, 0)),
        ],
        out_specs=spec,
        compiler_params=_cparams(3),
        name="hy_stage_c",
    )(a7, kf, f_fwd, f_inv)
    return d7.reshape(a4.shape)


def _k3_kernel(d_ref, gi_ref, z_ref, x_ref, skip_ref, *rest, n1, forward):
    if forward:
        g_ref, o_ref, a_ref, y_ref = rest
    else:
        o_ref, y_ref = rest
    _stage_a_inverse(d_ref, gi_ref, y_ref, n1=n1)
    rows = n1 // 2 * FFT_NB
    for part in range(2):
        for t in range(FFT_TILES):
            conv = y_ref[t, part * rows:(part + 1) * rows, :]
            o_ref[part, t] = x_ref[part, t] * (conv + skip_ref[t] * z_ref[part, t])
    if forward:
        _stage_a_forward(o_ref, g_ref, a_ref, n1=n1)


def _fft_stage_a_inverse(d4, g_inv, z6, x6, skip_row, n1, g_fwd=None):
    pairs, slabs, tiles, _, _ = d4.shape
    forward = g_fwd is not None
    in_specs = [
        _spec_spec(n1),
        pl.BlockSpec((FFT_NB, 2 * n1, 2 * n1), lambda c, j, p: (j, 0, 0)),
        _seq_spec(n1),
        _seq_spec(n1),
        pl.BlockSpec((FFT_TILES, 1, LANES), lambda c, j, p: (c, 0, 0)),
    ]
    args = [d4, g_inv, z6, x6, skip_row]
    out_shape = [jax.ShapeDtypeStruct(z6.shape, F32)]
    out_specs = [_seq_spec(n1)]
    if forward:
        in_specs.append(pl.BlockSpec((FFT_NB, 4 * n1, n1), lambda c, j, p: (j, 0, 0)))
        args.append(g_fwd)
        out_shape.append(jax.ShapeDtypeStruct(d4.shape, F32))
        out_specs.append(_spec_spec(n1))
    return pl.pallas_call(
        functools.partial(_k3_kernel, n1=n1, forward=forward),
        out_shape=out_shape,
        grid=(tiles // FFT_TILES, slabs, pairs),
        in_specs=in_specs,
        out_specs=out_specs,
        scratch_shapes=[pltpu.VMEM((FFT_TILES, n1 * FFT_NB, LANES), F32)],
        compiler_params=_cparams(3),
        name="hy_stage_a_inv_fwd" if forward else "hy_stage_a_inv",
    )(*args)


def _filt_kernel(zf_ref, zb_ref, w1_ref, b1_ref, w2_ref, b2_ref, w3_ref, b3_ref, wo_ref, freq_ref,
                 delta_ref, h_ref, l1_ref, *, tl):
    i = pl.program_id(0)
    freq = freq_ref[...]
    half = wo_ref.shape[1] // 2

    def dot(a, b):
        return jnp.dot(a, b, precision=HIGHEST, preferred_element_type=F32)

    def taps(z, wo):
        h = jnp.sin(freq * (dot(z, w1_ref[...]) + b1_ref[...]))
        h = jnp.sin(freq * (dot(h, w2_ref[...]) + b2_ref[...]))
        h = jnp.sin(freq * (dot(h, w3_ref[...]) + b3_ref[...]))
        return dot(h, wo) * jnp.exp(-z[:, 0:1] * delta_ref[...])

    hf = taps(zf_ref[...], wo_ref[:, :half])
    hb = taps(zb_ref[...], wo_ref[:, half:])
    row = i * tl + lax.broadcasted_iota(jnp.int32, (tl, 1), 0)
    hb = jnp.where(row == 0, 0.0, hb)
    _store_slabs(h_ref.at[0, 0], hf)
    _store_slabs(h_ref.at[1, 0], hb)
    h_ref[0, 1] = jnp.zeros(h_ref.shape[2:], F32)
    h_ref[1, 1] = jnp.zeros(h_ref.shape[2:], F32)

    @pl.when(i == 0)
    def _():
        l1_ref[...] = jnp.zeros_like(l1_ref)

    l1_ref[...] += (jnp.sum(jnp.abs(hf), axis=0, keepdims=True)
                    + jnp.sum(jnp.abs(hb), axis=0, keepdims=True))


def _pad_to(x, shape):
    return jnp.pad(x, [(0, s - d) for d, s in zip(x.shape, shape)])


def _filter_taps(seq_len, w1, b1, w2, b2, w3, b3, wo, freq):
    t = jnp.linspace(0.0, 1.0, seq_len, dtype=F32)[:, None]
    w = 2.0 * math.pi * jnp.arange(seq_len, dtype=F32)[:, None] / seq_len
    f = jnp.linspace(1e-4, HY_BANDS - 1, HY_BANDS, dtype=F32)[None, :]
    z = _pad_to(jnp.concatenate([t, jnp.cos(f * w), -jnp.sin(f * w)], axis=-1), (seq_len, FILT_PAD))
    z_prev = jnp.roll(z, 1, axis=0)
    deltas = jnp.abs(jnp.linspace(math.log(HY_TARGET) / HY_FAST_DECAY,
                                  math.log(HY_TARGET) / HY_SLOW_DECAY, D_MIX, dtype=F32))
    n_cols = HY_ORDER * D_MIX
    pad2 = (FILT_PAD, FILT_PAD)
    row = lambda v: _pad_to(v.reshape(1, -1), (1, FILT_PAD))
    tl = FILT_TILE
    const = lambda shape: pl.BlockSpec(shape, lambda i: (0,) * len(shape))
    return pl.pallas_call(
        functools.partial(_filt_kernel, tl=tl),
        out_shape=[jax.ShapeDtypeStruct((2, 2, SLABS, n_cols // LANES, seq_len // SLABS, LANES), F32),
                   jax.ShapeDtypeStruct((1, n_cols), F32)],
        grid=(seq_len // tl,),
        in_specs=[
            pl.BlockSpec((tl, FILT_PAD), lambda i: (i, 0)),
            pl.BlockSpec((tl, FILT_PAD), lambda i: (i, 0)),
            const(pad2), const((1, FILT_PAD)), const(pad2), const((1, FILT_PAD)),
            const(pad2), const((1, FILT_PAD)), const((FILT_PAD, 2 * n_cols)), const((1, FILT_PAD)),
            const((1, n_cols)),
        ],
        out_specs=[pl.BlockSpec((2, 2, SLABS, n_cols // LANES, tl // LANE_BLOCK * FFT_NB, LANES),
                                lambda i: (0, 0, 0, 0, i, 0)),
                   const((1, n_cols))],
        compiler_params=_cparams(1),
        name="hy_filter_taps",
    )(z, z_prev, _pad_to(w1, pad2), row(b1), _pad_to(w2, pad2), row(b2), _pad_to(w3, pad2), row(b3),
      _pad_to(wo, (FILT_PAD, 2 * n_cols)), row(freq), jnp.tile(deltas, HY_ORDER).reshape(1, n_cols))


def _k2f_kernel(af_ref, ab_ref, l1_ref, f_ref, kf_ref, *, kb):
    n2 = LANE_BLOCK
    f_hl = f_ref[...]
    inv_l1 = 1.0 / l1_ref[...]

    def spectrum(ref, kk):
        x = jnp.concatenate([_load_low_index(ref, 0, kk), _load_low_index(ref, 1, kk)], axis=0)
        return _dft3(f_hl, x, 2 * n2)

    def body(kk, carry):
        cf = spectrum(af_ref, kk)
        cb = spectrum(ab_ref, kk)
        kf_ref[0, kk] = (cf[:n2] + cb[:n2]) * inv_l1
        kf_ref[1, kk] = (cf[n2:] - cb[n2:]) * inv_l1
        return carry

    lax.fori_loop(0, kb, body, 0)


def _filter_spectrum(a4, l1, f_fwd, n1):
    _, slabs, tiles, _, _ = a4.shape
    cols = tiles * LANES
    n2 = LANE_BLOCK
    kb, ct = FFT_KB, FFT_CT
    a6 = a4.reshape(2, slabs, tiles, 2, n1, FFT_NB, LANES)
    spec = lambda d: pl.BlockSpec((None, slabs, FFT_TILES, 2, kb, FFT_NB, LANES),
                                  lambda c, k: (d, 0, c, 0, k, 0, 0))
    return pl.pallas_call(
        functools.partial(_k2f_kernel, kb=kb),
        out_shape=jax.ShapeDtypeStruct((2, n1, n2, cols), F32),
        grid=(cols // ct, n1 // kb),
        in_specs=[
            spec(0),
            spec(1),
            pl.BlockSpec((1, ct), lambda c, k: (0, c)),
            pl.BlockSpec((4 * n2, 2 * n2), lambda c, k: (0, 0)),
        ],
        out_specs=pl.BlockSpec((2, kb, n2, ct), lambda c, k: (0, k, 0, c)),
        compiler_params=_cparams(2),
        name="hy_filter_spectrum",
    )(a6, a6, l1, f_fwd)


def _hyena(hv, hx1, hx2, skip, filt_params, seq_len):
    n1, _, g_fwd, g_inv, f_fwd, f_inv = _fft_tables(seq_len)
    g_fwd, g_inv, f_fwd, f_inv = (jnp.asarray(m) for m in (g_fwd, g_inv, f_fwd, f_inv))
    taps, l1 = _filter_taps(seq_len, *filt_params)
    kf = _filter_spectrum(_fft_stage_a(taps, g_fwd, n1), l1, f_fwd, n1)
    as_pairs = lambda a: a.reshape((a.shape[0] // 2, 2) + a.shape[1:])
    z0, x1, x2 = as_pairs(hv), as_pairs(hx1), as_pairs(hx2)
    d = _fft_stage_c(_fft_stage_a(z0, g_fwd, n1), kf, f_fwd, f_inv, 0, n1)
    skip_rows = skip.reshape(HY_ORDER, D_MIX // LANES, 1, LANES)
    z1, a = _fft_stage_a_inverse(d, g_inv, z0, x1, skip_rows[0], n1, g_fwd=g_fwd)
    d = _fft_stage_c(a, kf, f_fwd, f_inv, 1, n1)
    (z2,) = _fft_stage_a_inverse(d, g_inv, z1, x2, skip_rows[1], n1)
    return z2.reshape(hv.shape)


def _outproj_kernel(x_ref, an_ref, hz_ref, mod_ref, gain_ref, wa_ref, wh_ref, o_ref):
    hn = _rms(_load_slabs(hz_ref), gain_ref[...]).astype(BF16)
    mixed = (jnp.dot(an_ref[...], wa_ref[...], preferred_element_type=F32)
             + jnp.dot(hn, wh_ref[...], preferred_element_type=F32))
    o_ref[...] = x_ref[...] + mod_ref[5:6, :] * mixed


def _outproj(x2d, attn_n, hz, mod, hy_gain, w_attn, w_hy, seq_len):
    t = x2d.shape[0]
    tm = TOKEN_TILE
    tiles_per_seq = seq_len // tm
    const = lambda shape: pl.BlockSpec(shape, lambda i: (0, 0))
    return pl.pallas_call(
        _outproj_kernel,
        out_shape=jax.ShapeDtypeStruct((t, D_MODEL), F32),
        grid=(t // tm,),
        in_specs=[
            pl.BlockSpec((tm, D_MODEL), lambda i: (i, 0)),
            pl.BlockSpec((tm, D_MIX), lambda i: (i, 0)),
            pl.BlockSpec((None, SLABS, D_MIX // LANES, tm // LANE_BLOCK * FFT_NB, LANES),
                         lambda i: (i // tiles_per_seq, 0, 0, i % tiles_per_seq, 0)),
            pl.BlockSpec((None, N_MOD, D_MODEL), lambda i: (i * tm // seq_len, 0, 0)),
            const((1, D_MIX)),
            const((D_MIX, D_MODEL)),
            const((D_MIX, D_MODEL)),
        ],
        out_specs=pl.BlockSpec((tm, D_MODEL), lambda i: (i, 0)),
        compiler_params=_cparams(1),
        name="outproj",
    )(x2d, attn_n, hz, mod, hy_gain.reshape(1, D_MIX), w_attn, w_hy)


def _trunk(x, mod, p, final_norm):
    batch, seq_len, _ = x.shape
    x2d = x.reshape(batch * seq_len, D_MODEL)
    x2d = _ffn(x2d, mod, p["ffn1_norm"], p["ffn1_w_gate"], p["ffn1_w_up"], p["ffn1_w_down"],
               seq_len, mod_base=0)
    q, k, v, hv, hx1, hx2 = _inproj(x2d, mod, p["mix_norm"], p["w_qkv"], p["w_hy"],
                                    p["hy_conv_w"], p["hy_conv_b"], seq_len)
    attn_n = _attention(q, k, v, p["attn_bias"], p["attn_out_norm"], batch, seq_len)
    hz = _hyena(hv, hx1, hx2, p["hy_skip"], p["hy_filter"], seq_len)
    x2d = _outproj(x2d, attn_n, hz, mod, p["hy_out_norm"], p["w_out_attn"], p["w_out_hy"], seq_len)
    y = _ffn(x2d, mod, p["ffn2_norm"], p["ffn2_w_gate"], p["ffn2_w_up"], p["ffn2_w_down"],
             seq_len, mod_base=6, final_gain=final_norm)
    return y.reshape(batch, seq_len, D_MODEL)


def kernel(x_prompt, x_sample, c_prompt, c_sample, w_ada, b_ada, ffn1_norm, ffn1_w_gate, ffn1_w_up,
           ffn1_w_down, mix_norm, w_in, na_rpb, hy_conv_w, hy_conv_b, hy_w1, hy_b1, hy_w2, hy_b2,
           hy_w3, hy_b3, hy_wo, hy_sin_freq, hy_skip, attn_out_norm, hy_out_norm, w_out, ffn2_norm,
           ffn2_w_gate, ffn2_w_up, ffn2_w_down, final_norm):
    assert w_ada.shape[0] == 1, "single-layer encoder"
    n_prompt = c_prompt.shape[0]
    mod_all = _ada(jnp.concatenate([c_prompt, c_sample], axis=0), w_ada[0], b_ada[0])
    mod_all = mod_all.reshape(-1, N_MOD, D_MODEL)
    bf = lambda w: w[0].astype(BF16)
    p = {
        "ffn1_norm": ffn1_norm[0], "ffn1_w_gate": bf(ffn1_w_gate), "ffn1_w_up": bf(ffn1_w_up),
        "ffn1_w_down": bf(ffn1_w_down),
        "mix_norm": mix_norm[0],
        "w_qkv": w_in[0, :, :3 * D_MIX].astype(BF16), "w_hy": w_in[0, :, 3 * D_MIX:].astype(BF16),
        "attn_bias": _attn_bias_table(na_rpb[0]),
        "hy_conv_w": hy_conv_w[0], "hy_conv_b": hy_conv_b[0],
        "hy_filter": (hy_w1[0], hy_b1[0], hy_w2[0], hy_b2[0], hy_w3[0], hy_b3[0], hy_wo[0],
                      hy_sin_freq[0]),
        "hy_skip": hy_skip[0],
        "attn_out_norm": attn_out_norm[0], "hy_out_norm": hy_out_norm[0],
        "w_out_attn": w_out[0, :D_MIX].astype(BF16), "w_out_hy": w_out[0, D_MIX:].astype(BF16),
        "ffn2_norm": ffn2_norm[0], "ffn2_w_gate": bf(ffn2_w_gate), "ffn2_w_up": bf(ffn2_w_up),
        "ffn2_w_down": bf(ffn2_w_down),
    }
    y_prompt = _trunk(x_prompt, mod_all[:n_prompt], p, final_norm)
    y_sample = _trunk(x_sample, mod_all[n_prompt:], p, final_norm)
    return (y_prompt, y_sample)
```

```python
import functools
import math

import ml_dtypes
import numpy as np
import jax
import jax.numpy as jnp
from jax import lax
from jax.experimental import pallas as pl
from jax.experimental.pallas import tpu as pltpu

F32 = jnp.float32
BF16 = jnp.bfloat16
HIGHEST = lax.Precision.HIGHEST

D_MODEL = 1024
GRID_W = 64
D_MIX = 512
NA_HEADS = 8
NA_HEAD_DIM = D_MIX // NA_HEADS
WIN_H = 8
WIN_W = 16
HY_ORDER = 2
HY_BANDS = 8
HY_EMB = 1 + 2 * HY_BANDS
HY_FAST_DECAY = 0.3
HY_SLOW_DECAY = 1.5
HY_TARGET = 1e-2
D_FF = ((8 * D_MODEL // 3 + 127) // 128) * 128
N_MOD = 9
EPS = 1e-6
NEG_INF = -1e30

V7X_VMEM_LIMIT_BYTES = 56 * 1024 * 1024
TOKEN_TILE = 512
FF_CHUNK = D_FF // 2
HALO = 16
LANE_BLOCK = 128
FFT_CT = 256
FFT_NB = 8
FFT_KB = 8
FFT_K_GROUP = 4
FILT_TILE = 512
FILT_PAD = 128


def _cparams(n_axes):
    return pltpu.CompilerParams(
        dimension_semantics=("arbitrary",) * n_axes,
        vmem_limit_bytes=V7X_VMEM_LIMIT_BYTES,
    )


def _rms(x, gain):
    ms = jnp.mean(x * x, axis=-1, keepdims=True)
    return x * lax.rsqrt(ms + EPS) * gain


def _silu(x):
    return x / (1.0 + jnp.exp(-x))


SLABS = LANE_BLOCK // FFT_NB
LANES = 128


def _store_slabs(ref, tile):
    for i1 in range(tile.shape[0] // LANE_BLOCK):
        for t in range(tile.shape[1] // LANES):
            rows = tile[i1 * LANE_BLOCK:(i1 + 1) * LANE_BLOCK, t * LANES:(t + 1) * LANES]
            ref[:, t, i1 * FFT_NB:(i1 + 1) * FFT_NB, :] = rows.reshape(SLABS, FFT_NB, LANES)


def _load_slabs(ref):
    _, tiles, rows, _ = ref.shape
    return jnp.concatenate(
        [jnp.concatenate([ref[:, t, i1 * FFT_NB:(i1 + 1) * FFT_NB, :].reshape(LANE_BLOCK, LANES)
                          for t in range(tiles)], axis=1)
         for i1 in range(rows // FFT_NB)], axis=0)


def _load_strided(ref, lead, start, size):
    tiles = ref.shape[len(lead)]
    return jnp.concatenate([ref[lead + (t, pl.ds(start, size, stride=FFT_NB), slice(None))]
                            for t in range(tiles)], axis=1)


def _store_strided(ref, start, val):
    for t in range(ref.shape[0]):
        ref[t, pl.ds(start, val.shape[0], stride=FFT_NB), :] = val[:, t * LANES:(t + 1) * LANES]


def _ada_kernel(c_ref, w_ref, b_ref, o_ref):
    s = _silu(c_ref[...])
    o_ref[...] = jnp.dot(s, w_ref[...], precision=HIGHEST, preferred_element_type=F32) + b_ref[...]


def _ada(c_all, w_ada, b_ada):
    rows = c_all.shape[0]
    n_out = w_ada.shape[1]
    tn = D_MODEL
    return pl.pallas_call(
        _ada_kernel,
        out_shape=jax.ShapeDtypeStruct((rows, n_out), F32),
        grid=(n_out // tn,),
        in_specs=[
            pl.BlockSpec((rows, D_MODEL), lambda j: (0, 0)),
            pl.BlockSpec((D_MODEL, tn), lambda j: (0, j)),
            pl.BlockSpec((1, tn), lambda j: (0, j)),
        ],
        out_specs=pl.BlockSpec((rows, tn), lambda j: (0, j)),
        compiler_params=_cparams(1),
        name="ada_mod",
    )(c_all, w_ada, b_ada.reshape(1, n_out))


def _ffn_kernel(x_ref, mod_ref, gain_ref, wg_ref, wu_ref, wd_ref, *rest, mod_base, final):
    if final:
        fn_ref, o_ref = rest
    else:
        (o_ref,) = rest
    x = x_ref[...]
    shift = mod_ref[mod_base:mod_base + 1, :]
    scale = mod_ref[mod_base + 1:mod_base + 2, :]
    gate = mod_ref[mod_base + 2:mod_base + 3, :]
    hb = (_rms(x, gain_ref[...]) * (1.0 + scale) + shift).astype(BF16)
    acc = None
    for c0 in range(0, D_FF, FF_CHUNK):
        g = jnp.dot(hb, wg_ref[:, c0:c0 + FF_CHUNK], preferred_element_type=F32)
        u = jnp.dot(hb, wu_ref[:, c0:c0 + FF_CHUNK], preferred_element_type=F32)
        a = (_silu(g) * u).astype(BF16)
        d = jnp.dot(a, wd_ref[c0:c0 + FF_CHUNK, :], preferred_element_type=F32)
        acc = d if acc is None else acc + d
    y = x + 0.5 * gate * acc
    if final:
        y = _rms(y, fn_ref[...])
    o_ref[...] = y


def _ffn(x2d, mod, gain, wg, wu, wd, seq_len, mod_base, final_gain=None):
    t = x2d.shape[0]
    tm = TOKEN_TILE
    final = final_gain is not None
    resident = lambda shape: pl.BlockSpec(shape, lambda i: (0, 0), pipeline_mode=pl.Buffered(1))
    in_specs = [
        pl.BlockSpec((tm, D_MODEL), lambda i: (i, 0)),
        pl.BlockSpec((None, N_MOD, D_MODEL), lambda i: (i * tm // seq_len, 0, 0)),
        resident((1, D_MODEL)),
        resident((D_MODEL, D_FF)),
        resident((D_MODEL, D_FF)),
        resident((D_FF, D_MODEL)),
    ]
    args = [x2d, mod, gain.reshape(1, D_MODEL), wg, wu, wd]
    if final:
        in_specs.append(resident((1, D_MODEL)))
        args.append(final_gain.reshape(1, D_MODEL))
    return pl.pallas_call(
        functools.partial(_ffn_kernel, mod_base=mod_base, final=final),
        out_shape=jax.ShapeDtypeStruct((t, D_MODEL), F32),
        grid=(t // tm,),
        in_specs=in_specs,
        out_specs=pl.BlockSpec((tm, D_MODEL), lambda i: (i, 0)),
        compiler_params=_cparams(1),
        name="ffn_final" if final else "ffn",
    )(*args)


def _inproj_kernel(x_ref, xp_ref, xn_ref, mod_ref, gain_ref, wqkv_ref, why_ref, cw_ref, cb_ref,
                   q_ref, k_ref, v_ref, hv_ref, hx1_ref, hx2_ref, ext_ref, u_ref,
                   *, tiles_per_seq, tm):
    pos = pl.program_id(0) % tiles_per_seq
    gain = gain_ref[...]
    shift = mod_ref[3:4, :]
    scale = 1.0 + mod_ref[4:5, :]

    def normed(x):
        return _rms(x, gain) * scale + shift

    hb = normed(x_ref[...]).astype(BF16)
    qkv = jnp.dot(hb, wqkv_ref[...], preferred_element_type=F32)
    q_ref[...] = (qkv[:, :D_MIX] * (NA_HEAD_DIM ** -0.5)).astype(BF16)
    k_ref[...] = qkv[:, D_MIX:2 * D_MIX].astype(BF16)
    v_ref[...] = qkv[:, 2 * D_MIX:].astype(BF16)

    has_prev = jnp.where(pos != 0, 1.0, 0.0)
    has_next = jnp.where(pos != tiles_per_seq - 1, 1.0, 0.0)
    ext_ref[0:HALO, :] = (normed(xp_ref[...]) * has_prev).astype(BF16)
    ext_ref[HALO:HALO + tm, :] = hb
    ext_ref[HALO + tm:, :] = (normed(xn_ref[...]) * has_next).astype(BF16)
    u_ref[...] = jnp.dot(ext_ref[...], why_ref[...], preferred_element_type=F32)
    conv = (cb_ref[...]
            + u_ref[pl.ds(HALO - 1, tm), :] * cw_ref[0:1, :]
            + u_ref[pl.ds(HALO, tm), :] * cw_ref[1:2, :]
            + u_ref[pl.ds(HALO + 1, tm), :] * cw_ref[2:3, :])
    for part, ref in enumerate((hv_ref, hx1_ref, hx2_ref)):
        _store_slabs(ref, conv[:, part * D_MIX:(part + 1) * D_MIX])


def _inproj(x2d, mod, gain, w_qkv, w_hy, conv_w, conv_b, seq_len):
    t = x2d.shape[0]
    tm = TOKEN_TILE
    n_halo_blocks = t // HALO
    per_tile = tm // HALO
    const = lambda shape: pl.BlockSpec(shape, lambda i: (0, 0))
    tiles_per_seq = seq_len // tm
    tile_out = jax.ShapeDtypeStruct((t, D_MIX), BF16)
    tile_spec = pl.BlockSpec((tm, D_MIX), lambda i: (i, 0))
    slab_rows = tm // LANE_BLOCK * FFT_NB
    slab_out = jax.ShapeDtypeStruct((t // seq_len, SLABS, D_MIX // LANES, seq_len // SLABS, LANES), F32)
    slab_spec = pl.BlockSpec((None, SLABS, D_MIX // LANES, slab_rows, LANES),
                             lambda i: (i // tiles_per_seq, 0, 0, i % tiles_per_seq, 0))
    return pl.pallas_call(
        functools.partial(_inproj_kernel, tiles_per_seq=tiles_per_seq, tm=tm),
        out_shape=[tile_out] * 3 + [slab_out] * 3,
        grid=(t // tm,),
        in_specs=[
            pl.BlockSpec((tm, D_MODEL), lambda i: (i, 0)),
            pl.BlockSpec((HALO, D_MODEL), lambda i: (jnp.maximum(i * per_tile - 1, 0), 0)),
            pl.BlockSpec((HALO, D_MODEL),
                         lambda i: (jnp.minimum((i + 1) * per_tile, n_halo_blocks - 1), 0)),
            pl.BlockSpec((None, N_MOD, D_MODEL), lambda i: (i * tm // seq_len, 0, 0)),
            const((1, D_MODEL)),
            const((D_MODEL, 3 * D_MIX)),
            const((D_MODEL, 3 * D_MIX)),
            const((3, 3 * D_MIX)),
            const((1, 3 * D_MIX)),
        ],
        out_specs=[tile_spec] * 3 + [slab_spec] * 3,
        scratch_shapes=[
            pltpu.VMEM((tm + 2 * HALO, D_MODEL), BF16),
            pltpu.VMEM((tm + 2 * HALO, 3 * D_MIX), F32),
        ],
        compiler_params=_cparams(1),
        name="inproj",
    )(x2d, x2d, x2d, mod, gain.reshape(1, D_MODEL), w_qkv, w_hy, conv_w,
      conv_b.reshape(1, 3 * D_MIX))


ROW_TOKENS = GRID_W
GROUP_ROWS = WIN_H
GROUP_TOKENS = GROUP_ROWS * ROW_TOKENS
WIN_TOKENS = WIN_H * ROW_TOKENS


def _attn_bias_table(rpb):
    n_heads, n_drow, n_dcol = rpb.shape
    period = 2 * GRID_W - 1
    wrapped = jnp.concatenate([rpb[..., WIN_W - 1:], jnp.zeros((n_heads, n_drow, period - n_dcol), F32),
                               rpb[..., :WIN_W - 1]], axis=-1).astype(F32)
    toeplitz = jnp.tile(wrapped, GRID_W)[..., :GRID_W * (period - 1)]
    toeplitz = toeplitz.reshape(n_heads, n_drow, GRID_W, period - 1)[..., :GRID_W]
    qc = np.arange(GRID_W)[:, None]
    kc = np.arange(GRID_W)[None, :]
    win_start = np.clip(qc - WIN_W // 2, 0, GRID_W - WIN_W)
    col_ok = (kc >= win_start) & (kc < win_start + WIN_W)
    masked = jnp.where(col_ok[None, None], toeplitz, NEG_INF)
    bias = jnp.stack([masked[:, o:o + WIN_H] for o in range(WIN_H)], axis=0)
    bias = jnp.transpose(bias, (0, 1, 3, 2, 4))
    return bias.reshape(WIN_H, NA_HEADS, GRID_W, WIN_TOKENS)


KV_WINDOW_ROWS = 3 * GROUP_ROWS


def _kv_window_start(g, rows):
    return jnp.clip(g * GROUP_ROWS - GROUP_ROWS, 0, rows - KV_WINDOW_ROWS)


def _attn_kernel(q_ref, k_ref, v_ref, bias_ref, gain_ref, o_ref, acc, *, rows):
    g = pl.program_id(1)
    win_start = _kv_window_start(g, rows)
    first_head = lax.broadcasted_iota(jnp.int32, (ROW_TOKENS, 2 * NA_HEAD_DIM), 1) < NA_HEAD_DIM
    head_pairs = [slice(hp * 2 * NA_HEAD_DIM, (hp + 1) * 2 * NA_HEAD_DIM) for hp in range(NA_HEADS // 2)]

    def row_body(rr, carry):
        r = g * GROUP_ROWS + rr
        start = jnp.clip(r - WIN_H // 2, 0, rows - WIN_H)
        koff = pl.multiple_of((start - win_start) * ROW_TOKENS, ROW_TOKENS)
        row_class = start - r + WIN_H - 1
        qoff = pl.multiple_of(rr * ROW_TOKENS, ROW_TOKENS)
        scores = []
        for hp, lanes in enumerate(head_pairs):
            q2 = q_ref[pl.ds(qoff, ROW_TOKENS), lanes]
            kw = k_ref[pl.ds(koff, WIN_TOKENS), lanes]
            for hh in range(2):
                keep = first_head if hh == 0 else jnp.logical_not(first_head)
                qm = jnp.where(keep, q2, jnp.zeros_like(q2))
                s = lax.dot_general(qm, kw, (((1,), (1,)), ((), ())), preferred_element_type=F32)
                scores.append(s + bias_ref[row_class, 2 * hp + hh])
        probs, denoms = [], []
        for s in scores:
            p = jnp.exp(s - jnp.max(s, axis=-1, keepdims=True))
            denoms.append(jnp.sum(p, axis=-1, keepdims=True))
            probs.append(p.astype(BF16))
        for hp, lanes in enumerate(head_pairs):
            vw = v_ref[pl.ds(koff, WIN_TOKENS), lanes]
            outs = [jnp.dot(probs[2 * hp + hh], vw, preferred_element_type=F32) / denoms[2 * hp + hh]
                    for hh in range(2)]
            acc[pl.ds(qoff, ROW_TOKENS), lanes] = jnp.where(first_head, outs[0], outs[1])
        return carry

    lax.fori_loop(0, GROUP_ROWS, row_body, 0)
    o_ref[...] = _rms(acc[...], gain_ref[...]).astype(BF16)


def _attention(q, k, v, bias, gain, batch, seq_len):
    rows = seq_len // GRID_W
    q3 = q.reshape(batch, seq_len, D_MIX)
    k3 = k.reshape(batch, seq_len, D_MIX)
    v3 = v.reshape(batch, seq_len, D_MIX)
    cur = pl.BlockSpec((None, GROUP_TOKENS, D_MIX), lambda b, g: (b, g, 0))
    window = pl.BlockSpec((None, pl.Element(KV_WINDOW_ROWS * ROW_TOKENS), pl.Element(D_MIX)),
                          lambda b, g: (b, _kv_window_start(g, rows) * ROW_TOKENS, 0))
    out = pl.pallas_call(
        functools.partial(_attn_kernel, rows=rows),
        out_shape=jax.ShapeDtypeStruct((batch, seq_len, D_MIX), BF16),
        grid=(batch, rows // GROUP_ROWS),
        in_specs=[
            cur, window, window,
            pl.BlockSpec((WIN_H, NA_HEADS, GRID_W, WIN_TOKENS), lambda b, g: (0, 0, 0, 0),
                         pipeline_mode=pl.Buffered(1)),
            pl.BlockSpec((1, D_MIX), lambda b, g: (0, 0)),
        ],
        out_specs=cur,
        scratch_shapes=[pltpu.VMEM((GROUP_TOKENS, D_MIX), F32)],
        compiler_params=_cparams(2),
        name="natten",
    )(q3, k3, v3, bias, gain.reshape(1, D_MIX))
    return out.reshape(batch * seq_len, D_MIX)


def _split_hi_lo(x):
    hi = x.astype(ml_dtypes.bfloat16)
    lo = (x - hi.astype(np.float64)).astype(ml_dtypes.bfloat16)
    return hi, lo


def _stack_hi_lo(m):
    hi, lo = _split_hi_lo(m)
    return np.concatenate([hi, lo], axis=-2)


def _embed(re, im):
    return np.concatenate([np.concatenate([re, -im], axis=-1),
                           np.concatenate([im, re], axis=-1)], axis=-2)


@functools.lru_cache(maxsize=None)
def _fft_tables(seq_len):
    n = 2 * seq_len
    n2 = LANE_BLOCK
    n1 = n // n2
    i2 = np.arange(n2)[:, None, None]
    k1 = np.arange(n1)[None, :, None]
    i1 = np.arange(n1 // 2)[None, None, :]
    ang = -2.0 * np.pi * ((k1 * (n2 * i1 + i2)) % n) / n
    gr, gi = np.cos(ang), np.sin(ang)
    g_fwd = _stack_hi_lo(_embed(gr, gi))
    g_inv = _stack_hi_lo(_embed(np.swapaxes(gr, 1, 2) / n, -np.swapaxes(gi, 1, 2) / n))
    jk = np.outer(np.arange(n2), np.arange(n2))
    ang2 = -2.0 * np.pi * (jk % n2) / n2
    fr, fi = np.cos(ang2), np.sin(ang2)
    f_fwd = _stack_hi_lo(_embed(fr, fi))
    f_inv = _stack_hi_lo(_embed(fr, -fi))
    return n1, n2, g_fwd, g_inv, f_fwd, f_inv


def _dft3(m_hl, x, m):
    x_hi = x.astype(BF16)
    x_lo = (x - x_hi.astype(F32)).astype(BF16)
    t = jnp.dot(m_hl, x_hi, preferred_element_type=F32)
    return t[:m] + t[m:] + jnp.dot(m_hl[:m], x_lo, preferred_element_type=F32)


def _stage_a_forward(x_ref, g_ref, a_ref, *, n1):
    for i in range(FFT_NB):
        x = jnp.concatenate([_load_strided(x_ref, (0,), i, n1 // 2),
                             _load_strided(x_ref, (1,), i, n1 // 2)], axis=0)
        _store_strided(a_ref, i, _dft3(g_ref[i], x, 2 * n1))


def _stage_a_inverse(d_ref, gi_ref, y_ref, *, n1):
    for i in range(FFT_NB):
        _store_strided(y_ref, i, _dft3(gi_ref[i], _load_strided(d_ref, (), i, 2 * n1), n1))


def _k1_kernel(x_ref, g_ref, a_ref, *, n1):
    _stage_a_forward(x_ref, g_ref, a_ref, n1=n1)


FFT_TILES = FFT_CT // LANES


def _seq_spec(n1):
    return pl.BlockSpec((None, 2, None, FFT_TILES, n1 // 2 * FFT_NB, LANES),
                        lambda c, j, p: (p, 0, j, c, 0, 0))


def _spec_spec(n1):
    return pl.BlockSpec((None, None, FFT_TILES, 2 * n1 * FFT_NB, LANES), lambda c, j, p: (p, j, c, 0, 0))


def _fft_stage_a(x6, g_fwd, n1):
    pairs, _, slabs, tiles, _, _ = x6.shape
    return pl.pallas_call(
        functools.partial(_k1_kernel, n1=n1),
        out_shape=jax.ShapeDtypeStruct((pairs, slabs, tiles, 2 * n1 * FFT_NB, LANES), F32),
        grid=(tiles // FFT_TILES, slabs, pairs),
        in_specs=[
            _seq_spec(n1),
            pl.BlockSpec((FFT_NB, 4 * n1, n1), lambda c, j, p: (j, 0, 0)),
        ],
        out_specs=_spec_spec(n1),
        compiler_params=_cparams(3),
        name="hy_stage_a",
    )(x6, g_fwd)


def _load_low_index(ref, part, kk):
    return jnp.concatenate([ref[:, t, part, kk].reshape(LANE_BLOCK, LANES) for t in range(ref.shape[1])],
                           axis=1)


def _store_low_index(ref, part, kk, val):
    for t in range(ref.shape[1]):
        ref[:, t, part, kk] = val[:, t * LANES:(t + 1) * LANES].reshape(SLABS, FFT_NB, LANES)


def _k2_kernel(a_ref, kf_ref, f_ref, fi_ref, d_ref, *, kb):
    n2 = LANE_BLOCK
    f_hl = f_ref[...]
    fi_hl = fi_ref[...]

    def body(group, carry):
        ks = [group * FFT_K_GROUP + u for u in range(FFT_K_GROUP)]
        spectra = [_dft3(f_hl, jnp.concatenate([_load_low_index(a_ref, 0, kk),
                                                _load_low_index(a_ref, 1, kk)], axis=0), 2 * n2)
                   for kk in ks]
        products = []
        for kk, c in zip(ks, spectra):
            cr, ci = c[:n2], c[n2:]
            kr, ki = kf_ref[0, kk], kf_ref[1, kk]
            products.append(jnp.concatenate([cr * kr - ci * ki, cr * ki + ci * kr], axis=0))
        for kk, y in zip(ks, products):
            d = _dft3(fi_hl, y, 2 * n2)
            _store_low_index(d_ref, 0, kk, d[:n2])
            _store_low_index(d_ref, 1, kk, d[n2:])
        return carry

    lax.fori_loop(0, kb // FFT_K_GROUP, body, 0)


def _fft_stage_c(a4, kf, f_fwd, f_inv, order, n1):
    pairs, slabs, tiles, _, _ = a4.shape
    n2 = LANE_BLOCK
    kb = FFT_KB
    ch_blocks = tiles // FFT_TILES
    a7 = a4.reshape(pairs, slabs, tiles, 2, n1, FFT_NB, LANES)
    spec = pl.BlockSpec((None, slabs, FFT_TILES, 2, kb, FFT_NB, LANES),
                        lambda c, k, p: (p, 0, c, 0, k, 0, 0))
    d7 = pl.pallas_call(
        functools.partial(_k2_kernel, kb=kb),
        out_shape=jax.ShapeDtypeStruct(a7.shape, F32),
        grid=(ch_blocks, n1 // kb, pairs),
        in_specs=[
            spec,
            pl.BlockSpec((2, kb, n2, FFT_CT), lambda c, k, p: (0, k, 0, order * ch_blocks + c)),
            pl.BlockSpec((4 * n2, 2 * n2), lambda c, k, p: (0, 0)),
            pl.BlockSpec((4 * n2, 2 * n2), lambda c, k, p: (0, 0)),
        ],
        out_specs=spec,
        compiler_params=_cparams(3),
        name="hy_stage_c",
    )(a7, kf, f_fwd, f_inv)
    return d7.reshape(a4.shape)


def _k3_kernel(d_ref, gi_ref, z_ref, x_ref, skip_ref, *rest, n1, forward):
    if forward:
        g_ref, o_ref, a_ref, y_ref = rest
    else:
        o_ref, y_ref = rest
    _stage_a_inverse(d_ref, gi_ref, y_ref, n1=n1)
    rows = n1 // 2 * FFT_NB
    for part in range(2):
        for t in range(FFT_TILES):
            conv = y_ref[t, part * rows:(part + 1) * rows, :]
            o_ref[part, t] = x_ref[part, t] * (conv + skip_ref[t] * z_ref[part, t])
    if forward:
        _stage_a_forward(o_ref, g_ref, a_ref, n1=n1)


def _fft_stage_a_inverse(d4, g_inv, z6, x6, skip_row, n1, g_fwd=None):
    pairs, slabs, tiles, _, _ = d4.shape
    forward = g_fwd is not None
    in_specs = [
        _spec_spec(n1),
        pl.BlockSpec((FFT_NB, 2 * n1, 2 * n1), lambda c, j, p: (j, 0, 0)),
        _seq_spec(n1),
        _seq_spec(n1),
        pl.BlockSpec((FFT_TILES, 1, LANES), lambda c, j, p: (c, 0, 0)),
    ]
    args = [d4, g_inv, z6, x6, skip_row]
    out_shape = [jax.ShapeDtypeStruct(z6.shape, F32)]
    out_specs = [_seq_spec(n1)]
    if forward:
        in_specs.append(pl.BlockSpec((FFT_NB, 4 * n1, n1), lambda c, j, p: (j, 0, 0)))
        args.append(g_fwd)
        out_shape.append(jax.ShapeDtypeStruct(d4.shape, F32))
        out_specs.append(_spec_spec(n1))
    return pl.pallas_call(
        functools.partial(_k3_kernel, n1=n1, forward=forward),
        out_shape=out_shape,
        grid=(tiles // FFT_TILES, slabs, pairs),
        in_specs=in_specs,
        out_specs=out_specs,
        scratch_shapes=[pltpu.VMEM((FFT_TILES, n1 * FFT_NB, LANES), F32)],
        compiler_params=_cparams(3),
        name="hy_stage_a_inv_fwd" if forward else "hy_stage_a_inv",
    )(*args)


def _filt_kernel(zf_ref, zb_ref, w1_ref, b1_ref, w2_ref, b2_ref, w3_ref, b3_ref, wo_ref, freq_ref,
                 delta_ref, h_ref, l1_ref, *, tl):
    i = pl.program_id(0)
    freq = freq_ref[...]
    half = wo_ref.shape[1] // 2

    def dot(a, b):
        return jnp.dot(a, b, precision=HIGHEST, preferred_element_type=F32)

    def taps(z, wo):
        h = jnp.sin(freq * (dot(z, w1_ref[...]) + b1_ref[...]))
        h = jnp.sin(freq * (dot(h, w2_ref[...]) + b2_ref[...]))
        h = jnp.sin(freq * (dot(h, w3_ref[...]) + b3_ref[...]))
        return dot(h, wo) * jnp.exp(-z[:, 0:1] * delta_ref[...])

    hf = taps(zf_ref[...], wo_ref[:, :half])
    hb = taps(zb_ref[...], wo_ref[:, half:])
    row = i * tl + lax.broadcasted_iota(jnp.int32, (tl, 1), 0)
    hb = jnp.where(row == 0, 0.0, hb)
    _store_slabs(h_ref.at[0, 0], hf)
    _store_slabs(h_ref.at[1, 0], hb)
    h_ref[0, 1] = jnp.zeros(h_ref.shape[2:], F32)
    h_ref[1, 1] = jnp.zeros(h_ref.shape[2:], F32)

    @pl.when(i == 0)
    def _():
        l1_ref[...] = jnp.zeros_like(l1_ref)

    l1_ref[...] += (jnp.sum(jnp.abs(hf), axis=0, keepdims=True)
                    + jnp.sum(jnp.abs(hb), axis=0, keepdims=True))


def _pad_to(x, shape):
    return jnp.pad(x, [(0, s - d) for d, s in zip(x.shape, shape)])


def _filter_taps(seq_len, w1, b1, w2, b2, w3, b3, wo, freq):
    t = jnp.linspace(0.0, 1.0, seq_len, dtype=F32)[:, None]
    w = 2.0 * math.pi * jnp.arange(seq_len, dtype=F32)[:, None] / seq_len
    f = jnp.linspace(1e-4, HY_BANDS - 1, HY_BANDS, dtype=F32)[None, :]
    z = _pad_to(jnp.concatenate([t, jnp.cos(f * w), -jnp.sin(f * w)], axis=-1), (seq_len, FILT_PAD))
    z_prev = jnp.roll(z, 1, axis=0)
    deltas = jnp.abs(jnp.linspace(math.log(HY_TARGET) / HY_FAST_DECAY,
                                  math.log(HY_TARGET) / HY_SLOW_DECAY, D_MIX, dtype=F32))
    n_cols = HY_ORDER * D_MIX
    pad2 = (FILT_PAD, FILT_PAD)
    row = lambda v: _pad_to(v.reshape(1, -1), (1, FILT_PAD))
    tl = FILT_TILE
    const = lambda shape: pl.BlockSpec(shape, lambda i: (0,) * len(shape))
    return pl.pallas_call(
        functools.partial(_filt_kernel, tl=tl),
        out_shape=[jax.ShapeDtypeStruct((2, 2, SLABS, n_cols // LANES, seq_len // SLABS, LANES), F32),
                   jax.ShapeDtypeStruct((1, n_cols), F32)],
        grid=(seq_len // tl,),
        in_specs=[
            pl.BlockSpec((tl, FILT_PAD), lambda i: (i, 0)),
            pl.BlockSpec((tl, FILT_PAD), lambda i: (i, 0)),
            const(pad2), const((1, FILT_PAD)), const(pad2), const((1, FILT_PAD)),
            const(pad2), const((1, FILT_PAD)), const((FILT_PAD, 2 * n_cols)), const((1, FILT_PAD)),
            const((1, n_cols)),
        ],
        out_specs=[pl.BlockSpec((2, 2, SLABS, n_cols // LANES, tl // LANE_BLOCK * FFT_NB, LANES),
                                lambda i: (0, 0, 0, 0, i, 0)),
                   const((1, n_cols))],
        compiler_params=_cparams(1),
        name="hy_filter_taps",
    )(z, z_prev, _pad_to(w1, pad2), row(b1), _pad_to(w2, pad2), row(b2), _pad_to(w3, pad2), row(b3),
      _pad_to(wo, (FILT_PAD, 2 * n_cols)), row(freq), jnp.tile(deltas, HY_ORDER).reshape(1, n_cols))


def _k2f_kernel(af_ref, ab_ref, l1_ref, f_ref, kf_ref, *, kb):
    n2 = LANE_BLOCK
    f_hl = f_ref[...]
    inv_l1 = 1.0 / l1_ref[...]

    def spectrum(ref, kk):
        x = jnp.concatenate([_load_low_index(ref, 0, kk), _load_low_index(ref, 1, kk)], axis=0)
        return _dft3(f_hl, x, 2 * n2)

    def body(group, carry):
        ks = [group * FFT_K_GROUP + u for u in range(FFT_K_GROUP)]
        forward = [spectrum(af_ref, kk) for kk in ks]
        backward = [spectrum(ab_ref, kk) for kk in ks]
        for kk, cf, cb in zip(ks, forward, backward):
            kf_ref[0, kk] = (cf[:n2] + cb[:n2]) * inv_l1
            kf_ref[1, kk] = (cf[n2:] - cb[n2:]) * inv_l1
        return carry

    lax.fori_loop(0, kb // FFT_K_GROUP, body, 0)


def _filter_spectrum(a4, l1, f_fwd, n1):
    _, slabs, tiles, _, _ = a4.shape
    cols = tiles * LANES
    n2 = LANE_BLOCK
    kb, ct = FFT_KB, FFT_CT
    a6 = a4.reshape(2, slabs, tiles, 2, n1, FFT_NB, LANES)
    spec = lambda d: pl.BlockSpec((None, slabs, FFT_TILES, 2, kb, FFT_NB, LANES),
                                  lambda c, k: (d, 0, c, 0, k, 0, 0))
    return pl.pallas_call(
        functools.partial(_k2f_kernel, kb=kb),
        out_shape=jax.ShapeDtypeStruct((2, n1, n2, cols), F32),
        grid=(cols // ct, n1 // kb),
        in_specs=[
            spec(0),
            spec(1),
            pl.BlockSpec((1, ct), lambda c, k: (0, c)),
            pl.BlockSpec((4 * n2, 2 * n2), lambda c, k: (0, 0)),
        ],
        out_specs=pl.BlockSpec((2, kb, n2, ct), lambda c, k: (0, k, 0, c)),
        compiler_params=_cparams(2),
        name="hy_filter_spectrum",
    )(a6, a6, l1, f_fwd)


def _hyena(hv, hx1, hx2, skip, filt_params, seq_len):
    n1, _, g_fwd, g_inv, f_fwd, f_inv = _fft_tables(seq_len)
    g_fwd, g_inv, f_fwd, f_inv = (jnp.asarray(m) for m in (g_fwd, g_inv, f_fwd, f_inv))
    taps, l1 = _filter_taps(seq_len, *filt_params)
    kf = _filter_spectrum(_fft_stage_a(taps, g_fwd, n1), l1, f_fwd, n1)
    as_pairs = lambda a: a.reshape((a.shape[0] // 2, 2) + a.shape[1:])
    z0, x1, x2 = as_pairs(hv), as_pairs(hx1), as_pairs(hx2)
    d = _fft_stage_c(_fft_stage_a(z0, g_fwd, n1), kf, f_fwd, f_inv, 0, n1)
    skip_rows = skip.reshape(HY_ORDER, D_MIX // LANES, 1, LANES)
    z1, a = _fft_stage_a_inverse(d, g_inv, z0, x1, skip_rows[0], n1, g_fwd=g_fwd)
    d = _fft_stage_c(a, kf, f_fwd, f_inv, 1, n1)
    (z2,) = _fft_stage_a_inverse(d, g_inv, z1, x2, skip_rows[1], n1)
    return z2.reshape(hv.shape)


def _outproj_kernel(x_ref, an_ref, hz_ref, mod_ref, gain_ref, wa_ref, wh_ref, o_ref):
    hn = _rms(_load_slabs(hz_ref), gain_ref[...]).astype(BF16)
    mixed = (jnp.dot(an_ref[...], wa_ref[...], preferred_element_type=F32)
             + jnp.dot(hn, wh_ref[...], preferred_element_type=F32))
    o_ref[...] = x_ref[...] + mod_ref[5:6, :] * mixed


def _outproj(x2d, attn_n, hz, mod, hy_gain, w_attn, w_hy, seq_len):
    t = x2d.shape[0]
    tm = TOKEN_TILE
    tiles_per_seq = seq_len // tm
    const = lambda shape: pl.BlockSpec(shape, lambda i: (0, 0))
    return pl.pallas_call(
        _outproj_kernel,
        out_shape=jax.ShapeDtypeStruct((t, D_MODEL), F32),
        grid=(t // tm,),
        in_specs=[
            pl.BlockSpec((tm, D_MODEL), lambda i: (i, 0)),
            pl.BlockSpec((tm, D_MIX), lambda i: (i, 0)),
            pl.BlockSpec((None, SLABS, D_MIX // LANES, tm // LANE_BLOCK * FFT_NB, LANES),
                         lambda i: (i // tiles_per_seq, 0, 0, i % tiles_per_seq, 0)),
            pl.BlockSpec((None, N_MOD, D_MODEL), lambda i: (i * tm // seq_len, 0, 0)),
            const((1, D_MIX)),
            const((D_MIX, D_MODEL)),
            const((D_MIX, D_MODEL)),
        ],
        out_specs=pl.BlockSpec((tm, D_MODEL), lambda i: (i, 0)),
        compiler_params=_cparams(1),
        name="outproj",
    )(x2d, attn_n, hz, mod, hy_gain.reshape(1, D_MIX), w_attn, w_hy)


def _trunk(x, mod, p, final_norm):
    batch, seq_len, _ = x.shape
    x2d = x.reshape(batch * seq_len, D_MODEL)
    x2d = _ffn(x2d, mod, p["ffn1_norm"], p["ffn1_w_gate"], p["ffn1_w_up"], p["ffn1_w_down"],
               seq_len, mod_base=0)
    q, k, v, hv, hx1, hx2 = _inproj(x2d, mod, p["mix_norm"], p["w_qkv"], p["w_hy"],
                                    p["hy_conv_w"], p["hy_conv_b"], seq_len)
    attn_n = _attention(q, k, v, p["attn_bias"], p["attn_out_norm"], batch, seq_len)
    hz = _hyena(hv, hx1, hx2, p["hy_skip"], p["hy_filter"], seq_len)
    x2d = _outproj(x2d, attn_n, hz, mod, p["hy_out_norm"], p["w_out_attn"], p["w_out_hy"], seq_len)
    y = _ffn(x2d, mod, p["ffn2_norm"], p["ffn2_w_gate"], p["ffn2_w_up"], p["ffn2_w_down"],
             seq_len, mod_base=6, final_gain=final_norm)
    return y.reshape(batch, seq_len, D_MODEL)


def kernel(x_prompt, x_sample, c_prompt, c_sample, w_ada, b_ada, ffn1_norm, ffn1_w_gate, ffn1_w_up,
           ffn1_w_down, mix_norm, w_in, na_rpb, hy_conv_w, hy_conv_b, hy_w1, hy_b1, hy_w2, hy_b2,
           hy_w3, hy_b3, hy_wo, hy_sin_freq, hy_skip, attn_out_norm, hy_out_norm, w_out, ffn2_norm,
           ffn2_w_gate, ffn2_w_up, ffn2_w_down, final_norm):
    assert w_ada.shape[0] == 1, "single-layer encoder"
    n_prompt = c_prompt.shape[0]
    mod_all = _ada(jnp.concatenate([c_prompt, c_sample], axis=0), w_ada[0], b_ada[0])
    mod_all = mod_all.reshape(-1, N_MOD, D_MODEL)
    bf = lambda w: w[0].astype(BF16)
    p = {
        "ffn1_norm": ffn1_norm[0], "ffn1_w_gate": bf(ffn1_w_gate), "ffn1_w_up": bf(ffn1_w_up),
        "ffn1_w_down": bf(ffn1_w_down),
        "mix_norm": mix_norm[0],
        "w_qkv": w_in[0, :, :3 * D_MIX].astype(BF16), "w_hy": w_in[0, :, 3 * D_MIX:].astype(BF16),
        "attn_bias": _attn_bias_table(na_rpb[0]),
        "hy_conv_w": hy_conv_w[0], "hy_conv_b": hy_conv_b[0],
        "hy_filter": (hy_w1[0], hy_b1[0], hy_w2[0], hy_b2[0], hy_w3[0], hy_b3[0], hy_wo[0],
                      hy_sin_freq[0]),
        "hy_skip": hy_skip[0],
        "attn_out_norm": attn_out_norm[0], "hy_out_norm": hy_out_norm[0],
        "w_out_attn": w_out[0, :D_MIX].astype(BF16), "w_out_hy": w_out[0, D_MIX:].astype(BF16),
        "ffn2_norm": ffn2_norm[0], "ffn2_w_gate": bf(ffn2_w_gate), "ffn2_w_up": bf(ffn2_w_up),
        "ffn2_w_down": bf(ffn2_w_down),
    }
    y_prompt = _trunk(x_prompt, mod_all[:n_prompt], p, final_norm)
    y_sample = _trunk(x_sample, mod_all[n_prompt:], p, final_norm)
    return (y_prompt, y_sample)
```

```python
import functools
import math

import ml_dtypes
import numpy as np
import jax
import jax.numpy as jnp
from jax import lax
from jax.experimental import pallas as pl
from jax.experimental.pallas import tpu as pltpu

F32 = jnp.float32
BF16 = jnp.bfloat16
HIGHEST = lax.Precision.HIGHEST

D_MODEL = 1024
GRID_W = 64
D_MIX = 512
NA_HEADS = 8
NA_HEAD_DIM = D_MIX // NA_HEADS
WIN_H = 8
WIN_W = 16
HY_ORDER = 2
HY_BANDS = 8
HY_EMB = 1 + 2 * HY_BANDS
HY_FAST_DECAY = 0.3
HY_SLOW_DECAY = 1.5
HY_TARGET = 1e-2
D_FF = ((8 * D_MODEL // 3 + 127) // 128) * 128
N_MOD = 9
EPS = 1e-6
NEG_INF = -1e30

V7X_VMEM_LIMIT_BYTES = 56 * 1024 * 1024
TOKEN_TILE = 512
V7X_MXU_WIDTH = 256
_FF_SPLIT = (D_FF // V7X_MXU_WIDTH + 1) // 2 * V7X_MXU_WIDTH
FF_CHUNKS = ((0, _FF_SPLIT), (_FF_SPLIT, D_FF))
HALO = 16
LANE_BLOCK = 128
FFT_CT = 256
FFT_NB = 16
FFT_KB = 8
FFT_K_GROUP = 4
FILT_TILE = 512
FILT_PAD = 128


def _cparams(n_axes):
    return pltpu.CompilerParams(
        dimension_semantics=("arbitrary",) * n_axes,
        vmem_limit_bytes=V7X_VMEM_LIMIT_BYTES,
    )


def _rms(x, gain):
    ms = jnp.mean(x * x, axis=-1, keepdims=True)
    return x * lax.rsqrt(ms + EPS) * gain


def _silu(x):
    return x / (1.0 + jnp.exp(-x))


SLABS = LANE_BLOCK // FFT_NB
LANES = 128


def _store_slabs(ref, tile):
    for i1 in range(tile.shape[0] // LANE_BLOCK):
        for t in range(tile.shape[1] // LANES):
            rows = tile[i1 * LANE_BLOCK:(i1 + 1) * LANE_BLOCK, t * LANES:(t + 1) * LANES]
            ref[:, t, i1 * FFT_NB:(i1 + 1) * FFT_NB, :] = rows.reshape(SLABS, FFT_NB, LANES)


def _load_slabs(ref):
    _, tiles, rows, _ = ref.shape
    return jnp.concatenate(
        [jnp.concatenate([ref[:, t, i1 * FFT_NB:(i1 + 1) * FFT_NB, :].reshape(LANE_BLOCK, LANES)
                          for t in range(tiles)], axis=1)
         for i1 in range(rows // FFT_NB)], axis=0)


def _load_strided(ref, lead, start, size):
    tiles = ref.shape[len(lead)]
    return jnp.concatenate([ref[lead + (t, pl.ds(start, size, stride=FFT_NB), slice(None))]
                            for t in range(tiles)], axis=1)


def _store_strided(ref, start, val):
    for t in range(ref.shape[0]):
        ref[t, pl.ds(start, val.shape[0], stride=FFT_NB), :] = val[:, t * LANES:(t + 1) * LANES]


def _ada_kernel(c_ref, w_ref, b_ref, o_ref):
    s = _silu(c_ref[...])
    o_ref[...] = jnp.dot(s, w_ref[...], precision=HIGHEST, preferred_element_type=F32) + b_ref[...]


def _ada(c_all, w_ada, b_ada):
    rows = c_all.shape[0]
    n_out = w_ada.shape[1]
    tn = D_MODEL
    return pl.pallas_call(
        _ada_kernel,
        out_shape=jax.ShapeDtypeStruct((rows, n_out), F32),
        grid=(n_out // tn,),
        in_specs=[
            pl.BlockSpec((rows, D_MODEL), lambda j: (0, 0)),
            pl.BlockSpec((D_MODEL, tn), lambda j: (0, j)),
            pl.BlockSpec((1, tn), lambda j: (0, j)),
        ],
        out_specs=pl.BlockSpec((rows, tn), lambda j: (0, j)),
        compiler_params=_cparams(1),
        name="ada_mod",
    )(c_all, w_ada, b_ada.reshape(1, n_out))


def _ffn_kernel(x_ref, mod_ref, gain_ref, wg_ref, wu_ref, wd_ref, *rest, mod_base, final):
    if final:
        fn_ref, o_ref = rest
    else:
        (o_ref,) = rest
    x = x_ref[...]
    shift = mod_ref[mod_base:mod_base + 1, :]
    scale = mod_ref[mod_base + 1:mod_base + 2, :]
    gate = mod_ref[mod_base + 2:mod_base + 3, :]
    hb = (_rms(x, gain_ref[...]) * (1.0 + scale) + shift).astype(BF16)
    acc = None
    for c0, c1 in FF_CHUNKS:
        g = jnp.dot(hb, wg_ref[:, c0:c1], preferred_element_type=F32)
        u = jnp.dot(hb, wu_ref[:, c0:c1], preferred_element_type=F32)
        a = (_silu(g) * u).astype(BF16)
        d = jnp.dot(a, wd_ref[c0:c1, :], preferred_element_type=F32)
        acc = d if acc is None else acc + d
    y = x + 0.5 * gate * acc
    if final:
        y = _rms(y, fn_ref[...])
    o_ref[...] = y


def _ffn(x2d, mod, gain, wg, wu, wd, seq_len, mod_base, final_gain=None):
    t = x2d.shape[0]
    tm = TOKEN_TILE
    final = final_gain is not None
    resident = lambda shape: pl.BlockSpec(shape, lambda i: (0, 0), pipeline_mode=pl.Buffered(1))
    in_specs = [
        pl.BlockSpec((tm, D_MODEL), lambda i: (i, 0)),
        pl.BlockSpec((None, N_MOD, D_MODEL), lambda i: (i * tm // seq_len, 0, 0)),
        resident((1, D_MODEL)),
        resident((D_MODEL, D_FF)),
        resident((D_MODEL, D_FF)),
        resident((D_FF, D_MODEL)),
    ]
    args = [x2d, mod, gain.reshape(1, D_MODEL), wg, wu, wd]
    if final:
        in_specs.append(resident((1, D_MODEL)))
        args.append(final_gain.reshape(1, D_MODEL))
    return pl.pallas_call(
        functools.partial(_ffn_kernel, mod_base=mod_base, final=final),
        out_shape=jax.ShapeDtypeStruct((t, D_MODEL), F32),
        grid=(t // tm,),
        in_specs=in_specs,
        out_specs=pl.BlockSpec((tm, D_MODEL), lambda i: (i, 0)),
        compiler_params=_cparams(1),
        name="ffn_final" if final else "ffn",
    )(*args)


def _inproj_kernel(x_ref, xp_ref, xn_ref, mod_ref, gain_ref, wqkv_ref, why_ref, cw_ref, cb_ref,
                   q_ref, k_ref, v_ref, hv_ref, hx1_ref, hx2_ref, ext_ref, u_ref,
                   *, tiles_per_seq, tm):
    pos = pl.program_id(0) % tiles_per_seq
    gain = gain_ref[...]
    shift = mod_ref[3:4, :]
    scale = 1.0 + mod_ref[4:5, :]

    def normed(x):
        return _rms(x, gain) * scale + shift

    hb = normed(x_ref[...]).astype(BF16)
    qkv = jnp.dot(hb, wqkv_ref[...], preferred_element_type=F32)
    q_ref[...] = (qkv[:, :D_MIX] * (NA_HEAD_DIM ** -0.5)).astype(BF16)
    k_ref[...] = qkv[:, D_MIX:2 * D_MIX].astype(BF16)
    v_ref[...] = qkv[:, 2 * D_MIX:].astype(BF16)

    has_prev = jnp.where(pos != 0, 1.0, 0.0)
    has_next = jnp.where(pos != tiles_per_seq - 1, 1.0, 0.0)
    ext_ref[0:HALO, :] = (normed(xp_ref[...]) * has_prev).astype(BF16)
    ext_ref[HALO:HALO + tm, :] = hb
    ext_ref[HALO + tm:, :] = (normed(xn_ref[...]) * has_next).astype(BF16)
    u_ref[...] = jnp.dot(ext_ref[...], why_ref[...], preferred_element_type=F32)
    conv = (cb_ref[...]
            + u_ref[pl.ds(HALO - 1, tm), :] * cw_ref[0:1, :]
            + u_ref[pl.ds(HALO, tm), :] * cw_ref[1:2, :]
            + u_ref[pl.ds(HALO + 1, tm), :] * cw_ref[2:3, :])
    for part, ref in enumerate((hv_ref, hx1_ref, hx2_ref)):
        _store_slabs(ref, conv[:, part * D_MIX:(part + 1) * D_MIX])


def _inproj(x2d, mod, gain, w_qkv, w_hy, conv_w, conv_b, seq_len):
    t = x2d.shape[0]
    tm = TOKEN_TILE
    n_halo_blocks = t // HALO
    per_tile = tm // HALO
    const = lambda shape: pl.BlockSpec(shape, lambda i: (0, 0))
    tiles_per_seq = seq_len // tm
    tile_out = jax.ShapeDtypeStruct((t, D_MIX), BF16)
    tile_spec = pl.BlockSpec((tm, D_MIX), lambda i: (i, 0))
    slab_rows = tm // LANE_BLOCK * FFT_NB
    slab_out = jax.ShapeDtypeStruct((t // seq_len, SLABS, D_MIX // LANES, seq_len // SLABS, LANES), F32)
    slab_spec = pl.BlockSpec((None, SLABS, D_MIX // LANES, slab_rows, LANES),
                             lambda i: (i // tiles_per_seq, 0, 0, i % tiles_per_seq, 0))
    return pl.pallas_call(
        functools.partial(_inproj_kernel, tiles_per_seq=tiles_per_seq, tm=tm),
        out_shape=[tile_out] * 3 + [slab_out] * 3,
        grid=(t // tm,),
        in_specs=[
            pl.BlockSpec((tm, D_MODEL), lambda i: (i, 0)),
            pl.BlockSpec((HALO, D_MODEL), lambda i: (jnp.maximum(i * per_tile - 1, 0), 0)),
            pl.BlockSpec((HALO, D_MODEL),
                         lambda i: (jnp.minimum((i + 1) * per_tile, n_halo_blocks - 1), 0)),
            pl.BlockSpec((None, N_MOD, D_MODEL), lambda i: (i * tm // seq_len, 0, 0)),
            const((1, D_MODEL)),
            const((D_MODEL, 3 * D_MIX)),
            const((D_MODEL, 3 * D_MIX)),
            const((3, 3 * D_MIX)),
            const((1, 3 * D_MIX)),
        ],
        out_specs=[tile_spec] * 3 + [slab_spec] * 3,
        scratch_shapes=[
            pltpu.VMEM((tm + 2 * HALO, D_MODEL), BF16),
            pltpu.VMEM((tm + 2 * HALO, 3 * D_MIX), F32),
        ],
        compiler_params=_cparams(1),
        name="inproj",
    )(x2d, x2d, x2d, mod, gain.reshape(1, D_MODEL), w_qkv, w_hy, conv_w,
      conv_b.reshape(1, 3 * D_MIX))


ROW_TOKENS = GRID_W
GROUP_ROWS = WIN_H
GROUP_TOKENS = GROUP_ROWS * ROW_TOKENS
WIN_TOKENS = WIN_H * ROW_TOKENS


def _attn_bias_table(rpb):
    n_heads, n_drow, n_dcol = rpb.shape
    period = 2 * GRID_W - 1
    wrapped = jnp.concatenate([rpb[..., WIN_W - 1:], jnp.zeros((n_heads, n_drow, period - n_dcol), F32),
                               rpb[..., :WIN_W - 1]], axis=-1).astype(F32)
    toeplitz = jnp.tile(wrapped, GRID_W)[..., :GRID_W * (period - 1)]
    toeplitz = toeplitz.reshape(n_heads, n_drow, GRID_W, period - 1)[..., :GRID_W]
    qc = np.arange(GRID_W)[:, None]
    kc = np.arange(GRID_W)[None, :]
    win_start = np.clip(qc - WIN_W // 2, 0, GRID_W - WIN_W)
    col_ok = (kc >= win_start) & (kc < win_start + WIN_W)
    masked = jnp.where(col_ok[None, None], toeplitz, NEG_INF)
    bias = jnp.stack([masked[:, o:o + WIN_H] for o in range(WIN_H)], axis=0)
    bias = jnp.transpose(bias, (0, 1, 3, 2, 4))
    return bias.reshape(WIN_H, NA_HEADS, GRID_W, WIN_TOKENS)


NATTEN_UNROLL = 4
KV_WINDOW_ROWS = 3 * GROUP_ROWS


def _kv_window_start(g, rows):
    return jnp.clip(g * GROUP_ROWS - GROUP_ROWS, 0, rows - KV_WINDOW_ROWS)


def _attn_kernel(q_ref, k_ref, v_ref, bias_ref, gain_ref, o_ref, acc, *, rows):
    g = pl.program_id(1)
    win_start = _kv_window_start(g, rows)
    first_head = lax.broadcasted_iota(jnp.int32, (ROW_TOKENS, 2 * NA_HEAD_DIM), 1) < NA_HEAD_DIM
    head_pairs = [slice(hp * 2 * NA_HEAD_DIM, (hp + 1) * 2 * NA_HEAD_DIM) for hp in range(NA_HEADS // 2)]

    def row_body(rr, carry):
        r = g * GROUP_ROWS + rr
        start = jnp.clip(r - WIN_H // 2, 0, rows - WIN_H)
        koff = pl.multiple_of((start - win_start) * ROW_TOKENS, ROW_TOKENS)
        row_class = start - r + WIN_H - 1
        qoff = pl.multiple_of(rr * ROW_TOKENS, ROW_TOKENS)
        scores = []
        for hp, lanes in enumerate(head_pairs):
            q2 = q_ref[pl.ds(qoff, ROW_TOKENS), lanes]
            kw = k_ref[pl.ds(koff, WIN_TOKENS), lanes]
            for hh in range(2):
                keep = first_head if hh == 0 else jnp.logical_not(first_head)
                qm = jnp.where(keep, q2, jnp.zeros_like(q2))
                s = lax.dot_general(qm, kw, (((1,), (1,)), ((), ())), preferred_element_type=F32)
                scores.append(s + bias_ref[row_class, 2 * hp + hh])
        probs, denoms = [], []
        for s in scores:
            p = jnp.exp(s - jnp.max(s, axis=-1, keepdims=True))
            denoms.append(jnp.sum(p, axis=-1, keepdims=True))
            probs.append(p.astype(BF16))
        for hp, lanes in enumerate(head_pairs):
            vw = v_ref[pl.ds(koff, WIN_TOKENS), lanes]
            outs = [jnp.dot(probs[2 * hp + hh], vw, preferred_element_type=F32) / denoms[2 * hp + hh]
                    for hh in range(2)]
            acc[pl.ds(qoff, ROW_TOKENS), lanes] = jnp.where(first_head, outs[0], outs[1])
        return carry

    lax.fori_loop(0, GROUP_ROWS, row_body, 0, unroll=NATTEN_UNROLL)
    o_ref[...] = _rms(acc[...], gain_ref[...]).astype(BF16)


def _attention(q, k, v, bias, gain, batch, seq_len):
    rows = seq_len // GRID_W
    q3 = q.reshape(batch, seq_len, D_MIX)
    k3 = k.reshape(batch, seq_len, D_MIX)
    v3 = v.reshape(batch, seq_len, D_MIX)
    cur = pl.BlockSpec((None, GROUP_TOKENS, D_MIX), lambda b, g: (b, g, 0))
    window = pl.BlockSpec((None, pl.Element(KV_WINDOW_ROWS * ROW_TOKENS), pl.Element(D_MIX)),
                          lambda b, g: (b, _kv_window_start(g, rows) * ROW_TOKENS, 0))
    out = pl.pallas_call(
        functools.partial(_attn_kernel, rows=rows),
        out_shape=jax.ShapeDtypeStruct((batch, seq_len, D_MIX), BF16),
        grid=(batch, rows // GROUP_ROWS),
        in_specs=[
            cur, window, window,
            pl.BlockSpec((WIN_H, NA_HEADS, GRID_W, WIN_TOKENS), lambda b, g: (0, 0, 0, 0),
                         pipeline_mode=pl.Buffered(1)),
            pl.BlockSpec((1, D_MIX), lambda b, g: (0, 0)),
        ],
        out_specs=cur,
        scratch_shapes=[pltpu.VMEM((GROUP_TOKENS, D_MIX), F32)],
        compiler_params=_cparams(2),
        name="natten",
    )(q3, k3, v3, bias, gain.reshape(1, D_MIX))
    return out.reshape(batch * seq_len, D_MIX)


def _split_hi_lo(x):
    hi = x.astype(ml_dtypes.bfloat16)
    lo = (x - hi.astype(np.float64)).astype(ml_dtypes.bfloat16)
    return hi, lo


def _stack_hi_lo(m):
    hi, lo = _split_hi_lo(m)
    return np.concatenate([hi, lo], axis=-2)


def _embed(re, im):
    return np.concatenate([np.concatenate([re, -im], axis=-1),
                           np.concatenate([im, re], axis=-1)], axis=-2)


@functools.lru_cache(maxsize=None)
def _fft_tables(seq_len):
    n = 2 * seq_len
    n2 = LANE_BLOCK
    n1 = n // n2
    i2 = np.arange(n2)[:, None, None]
    k1 = np.arange(n1)[None, :, None]
    i1 = np.arange(n1 // 2)[None, None, :]
    ang = -2.0 * np.pi * ((k1 * (n2 * i1 + i2)) % n) / n
    gr, gi = np.cos(ang), np.sin(ang)
    g_fwd = _stack_hi_lo(_embed(gr, gi))
    g_inv = _stack_hi_lo(_embed(np.swapaxes(gr, 1, 2) / n, -np.swapaxes(gi, 1, 2) / n))
    jk = np.outer(np.arange(n2), np.arange(n2))
    ang2 = -2.0 * np.pi * (jk % n2) / n2
    fr, fi = np.cos(ang2), np.sin(ang2)
    f_fwd = _stack_hi_lo(_embed(fr, fi))
    f_inv = _stack_hi_lo(_embed(fr, -fi))
    return n1, n2, g_fwd, g_inv, f_fwd, f_inv


def _dft3(m_hl, x, m):
    x_hi = x.astype(BF16)
    x_lo = (x - x_hi.astype(F32)).astype(BF16)
    t = jnp.dot(m_hl, x_hi, preferred_element_type=F32)
    return t[:m] + t[m:] + jnp.dot(m_hl[:m], x_lo, preferred_element_type=F32)


def _stage_a_forward(x_ref, g_ref, a_ref, *, n1):
    for i in range(FFT_NB):
        x = jnp.concatenate([_load_strided(x_ref, (0,), i, n1 // 2),
                             _load_strided(x_ref, (1,), i, n1 // 2)], axis=0)
        _store_strided(a_ref, i, _dft3(g_ref[i], x, 2 * n1))


def _stage_a_inverse(d_ref, gi_ref, y_ref, *, n1):
    for i in range(FFT_NB):
        _store_strided(y_ref, i, _dft3(gi_ref[i], _load_strided(d_ref, (), i, 2 * n1), n1))


def _k1_kernel(x_ref, g_ref, a_ref, *, n1):
    _stage_a_forward(x_ref, g_ref, a_ref, n1=n1)


FFT_TILES = FFT_CT // LANES


def _seq_spec(n1):
    return pl.BlockSpec((None, 2, None, FFT_TILES, n1 // 2 * FFT_NB, LANES),
                        lambda c, j, p: (p, 0, j, c, 0, 0))


def _spec_spec(n1):
    return pl.BlockSpec((None, None, FFT_TILES, 2 * n1 * FFT_NB, LANES), lambda c, j, p: (p, j, c, 0, 0))


def _fft_stage_a(x6, g_fwd, n1):
    pairs, _, slabs, tiles, _, _ = x6.shape
    return pl.pallas_call(
        functools.partial(_k1_kernel, n1=n1),
        out_shape=jax.ShapeDtypeStruct((pairs, slabs, tiles, 2 * n1 * FFT_NB, LANES), F32),
        grid=(tiles // FFT_TILES, slabs, pairs),
        in_specs=[
            _seq_spec(n1),
            pl.BlockSpec((FFT_NB, 4 * n1, n1), lambda c, j, p: (j, 0, 0)),
        ],
        out_specs=_spec_spec(n1),
        compiler_params=_cparams(3),
        name="hy_stage_a",
    )(x6, g_fwd)


def _load_low_index(ref, part, kk):
    return jnp.concatenate([ref[:, t, part, kk].reshape(LANE_BLOCK, LANES) for t in range(ref.shape[1])],
                           axis=1)


def _store_low_index(ref, part, kk, val):
    for t in range(ref.shape[1]):
        ref[:, t, part, kk] = val[:, t * LANES:(t + 1) * LANES].reshape(SLABS, FFT_NB, LANES)


def _k2_kernel(a_ref, kf_ref, f_ref, fi_ref, d_ref, *, kb):
    n2 = LANE_BLOCK
    f_hl = f_ref[...]
    fi_hl = fi_ref[...]

    def body(group, carry):
        ks = [group * FFT_K_GROUP + u for u in range(FFT_K_GROUP)]
        spectra = [_dft3(f_hl, jnp.concatenate([_load_low_index(a_ref, 0, kk),
                                                _load_low_index(a_ref, 1, kk)], axis=0), 2 * n2)
                   for kk in ks]
        products = []
        for kk, c in zip(ks, spectra):
            cr, ci = c[:n2], c[n2:]
            kr, ki = kf_ref[0, kk], kf_ref[1, kk]
            products.append(jnp.concatenate([cr * kr - ci * ki, cr * ki + ci * kr], axis=0))
        for kk, y in zip(ks, products):
            d = _dft3(fi_hl, y, 2 * n2)
            _store_low_index(d_ref, 0, kk, d[:n2])
            _store_low_index(d_ref, 1, kk, d[n2:])
        return carry

    lax.fori_loop(0, kb // FFT_K_GROUP, body, 0)


def _fft_stage_c(a4, kf, f_fwd, f_inv, order, n1):
    pairs, slabs, tiles, _, _ = a4.shape
    n2 = LANE_BLOCK
    kb = FFT_KB
    ch_blocks = tiles // FFT_TILES
    a7 = a4.reshape(pairs, slabs, tiles, 2, n1, FFT_NB, LANES)
    spec = pl.BlockSpec((None, slabs, FFT_TILES, 2, kb, FFT_NB, LANES),
                        lambda c, k, p: (p, 0, c, 0, k, 0, 0))
    d7 = pl.pallas_call(
        functools.partial(_k2_kernel, kb=kb),
        out_shape=jax.ShapeDtypeStruct(a7.shape, F32),
        grid=(ch_blocks, n1 // kb, pairs),
        in_specs=[
            spec,
            pl.BlockSpec((2, kb, n2, FFT_CT), lambda c, k, p: (0, k, 0, order * ch_blocks + c)),
            pl.BlockSpec((4 * n2, 2 * n2), lambda c, k, p: (0, 0)),
            pl.BlockSpec((4 * n2, 2 * n2), lambda c, k, p: (0, 0)),
        ],
        out_specs=spec,
        compiler_params=_cparams(3),
        name="hy_stage_c",
    )(a7, kf, f_fwd, f_inv)
    return d7.reshape(a4.shape)


def _k3_kernel(d_ref, gi_ref, z_ref, x_ref, skip_ref, *rest, n1, forward):
    if forward:
        g_ref, o_ref, a_ref, y_ref = rest
    else:
        o_ref, y_ref = rest
    _stage_a_inverse(d_ref, gi_ref, y_ref, n1=n1)
    rows = n1 // 2 * FFT_NB
    for part in range(2):
        for t in range(FFT_TILES):
            conv = y_ref[t, part * rows:(part + 1) * rows, :]
            o_ref[part, t] = x_ref[part, t] * (conv + skip_ref[t] * z_ref[part, t])
    if forward:
        _stage_a_forward(o_ref, g_ref, a_ref, n1=n1)


def _fft_stage_a_inverse(d4, g_inv, z6, x6, skip_row, n1, g_fwd=None):
    pairs, slabs, tiles, _, _ = d4.shape
    forward = g_fwd is not None
    in_specs = [
        _spec_spec(n1),
        pl.BlockSpec((FFT_NB, 2 * n1, 2 * n1), lambda c, j, p: (j, 0, 0)),
        _seq_spec(n1),
        _seq_spec(n1),
        pl.BlockSpec((FFT_TILES, 1, LANES), lambda c, j, p: (c, 0, 0)),
    ]
    args = [d4, g_inv, z6, x6, skip_row]
    out_shape = [jax.ShapeDtypeStruct(z6.shape, F32)]
    out_specs = [_seq_spec(n1)]
    if forward:
        in_specs.append(pl.BlockSpec((FFT_NB, 4 * n1, n1), lambda c, j, p: (j, 0, 0)))
        args.append(g_fwd)
        out_shape.append(jax.ShapeDtypeStruct(d4.shape, F32))
        out_specs.append(_spec_spec(n1))
    return pl.pallas_call(
        functools.partial(_k3_kernel, n1=n1, forward=forward),
        out_shape=out_shape,
        grid=(tiles // FFT_TILES, slabs, pairs),
        in_specs=in_specs,
        out_specs=out_specs,
        scratch_shapes=[pltpu.VMEM((FFT_TILES, n1 * FFT_NB, LANES), F32)],
        compiler_params=_cparams(3),
        name="hy_stage_a_inv_fwd" if forward else "hy_stage_a_inv",
    )(*args)


def _filt_kernel(zf_ref, zb_ref, w1_ref, b1_ref, w2_ref, b2_ref, w3_ref, b3_ref, wo_ref, freq_ref,
                 delta_ref, h_ref, l1_ref, *, tl):
    i = pl.program_id(0)
    freq = freq_ref[...]
    half = wo_ref.shape[1] // 2

    def dot(a, b):
        return jnp.dot(a, b, precision=HIGHEST, preferred_element_type=F32)

    def taps(z, wo):
        h = jnp.sin(freq * (dot(z, w1_ref[...]) + b1_ref[...]))
        h = jnp.sin(freq * (dot(h, w2_ref[...]) + b2_ref[...]))
        h = jnp.sin(freq * (dot(h, w3_ref[...]) + b3_ref[...]))
        return dot(h, wo) * jnp.exp(-z[:, 0:1] * delta_ref[...])

    hf = taps(zf_ref[...], wo_ref[:, :half])
    hb = taps(zb_ref[...], wo_ref[:, half:])
    row = i * tl + lax.broadcasted_iota(jnp.int32, (tl, 1), 0)
    hb = jnp.where(row == 0, 0.0, hb)
    _store_slabs(h_ref.at[0, 0], hf)
    _store_slabs(h_ref.at[1, 0], hb)
    h_ref[0, 1] = jnp.zeros(h_ref.shape[2:], F32)
    h_ref[1, 1] = jnp.zeros(h_ref.shape[2:], F32)

    @pl.when(i == 0)
    def _():
        l1_ref[...] = jnp.zeros_like(l1_ref)

    l1_ref[...] += (jnp.sum(jnp.abs(hf), axis=0, keepdims=True)
                    + jnp.sum(jnp.abs(hb), axis=0, keepdims=True))


def _pad_to(x, shape):
    return jnp.pad(x, [(0, s - d) for d, s in zip(x.shape, shape)])


def _filter_taps(seq_len, w1, b1, w2, b2, w3, b3, wo, freq):
    t = jnp.linspace(0.0, 1.0, seq_len, dtype=F32)[:, None]
    w = 2.0 * math.pi * jnp.arange(seq_len, dtype=F32)[:, None] / seq_len
    f = jnp.linspace(1e-4, HY_BANDS - 1, HY_BANDS, dtype=F32)[None, :]
    z = _pad_to(jnp.concatenate([t, jnp.cos(f * w), -jnp.sin(f * w)], axis=-1), (seq_len, FILT_PAD))
    z_prev = jnp.roll(z, 1, axis=0)
    deltas = jnp.abs(jnp.linspace(math.log(HY_TARGET) / HY_FAST_DECAY,
                                  math.log(HY_TARGET) / HY_SLOW_DECAY, D_MIX, dtype=F32))
    n_cols = HY_ORDER * D_MIX
    pad2 = (FILT_PAD, FILT_PAD)
    row = lambda v: _pad_to(v.reshape(1, -1), (1, FILT_PAD))
    tl = FILT_TILE
    const = lambda shape: pl.BlockSpec(shape, lambda i: (0,) * len(shape))
    return pl.pallas_call(
        functools.partial(_filt_kernel, tl=tl),
        out_shape=[jax.ShapeDtypeStruct((2, 2, SLABS, n_cols // LANES, seq_len // SLABS, LANES), F32),
                   jax.ShapeDtypeStruct((1, n_cols), F32)],
        grid=(seq_len // tl,),
        in_specs=[
            pl.BlockSpec((tl, FILT_PAD), lambda i: (i, 0)),
            pl.BlockSpec((tl, FILT_PAD), lambda i: (i, 0)),
            const(pad2), const((1, FILT_PAD)), const(pad2), const((1, FILT_PAD)),
            const(pad2), const((1, FILT_PAD)), const((FILT_PAD, 2 * n_cols)), const((1, FILT_PAD)),
            const((1, n_cols)),
        ],
        out_specs=[pl.BlockSpec((2, 2, SLABS, n_cols // LANES, tl // LANE_BLOCK * FFT_NB, LANES),
                                lambda i: (0, 0, 0, 0, i, 0)),
                   const((1, n_cols))],
        compiler_params=_cparams(1),
        name="hy_filter_taps",
    )(z, z_prev, _pad_to(w1, pad2), row(b1), _pad_to(w2, pad2), row(b2), _pad_to(w3, pad2), row(b3),
      _pad_to(wo, (FILT_PAD, 2 * n_cols)), row(freq), jnp.tile(deltas, HY_ORDER).reshape(1, n_cols))


def _k2f_kernel(af_ref, ab_ref, l1_ref, f_ref, kf_ref, *, kb):
    n2 = LANE_BLOCK
    f_hl = f_ref[...]
    inv_l1 = 1.0 / l1_ref[...]

    def spectrum(ref, kk):
        x = jnp.concatenate([_load_low_index(ref, 0, kk), _load_low_index(ref, 1, kk)], axis=0)
        return _dft3(f_hl, x, 2 * n2)

    def body(group, carry):
        ks = [group * FFT_K_GROUP + u for u in range(FFT_K_GROUP)]
        forward = [spectrum(af_ref, kk) for kk in ks]
        backward = [spectrum(ab_ref, kk) for kk in ks]
        for kk, cf, cb in zip(ks, forward, backward):
            kf_ref[0, kk] = (cf[:n2] + cb[:n2]) * inv_l1
            kf_ref[1, kk] = (cf[n2:] - cb[n2:]) * inv_l1
        return carry

    lax.fori_loop(0, kb // FFT_K_GROUP, body, 0)


def _filter_spectrum(a4, l1, f_fwd, n1):
    _, slabs, tiles, _, _ = a4.shape
    cols = tiles * LANES
    n2 = LANE_BLOCK
    kb, ct = FFT_KB, FFT_CT
    a6 = a4.reshape(2, slabs, tiles, 2, n1, FFT_NB, LANES)
    spec = lambda d: pl.BlockSpec((None, slabs, FFT_TILES, 2, kb, FFT_NB, LANES),
                                  lambda c, k: (d, 0, c, 0, k, 0, 0))
    return pl.pallas_call(
        functools.partial(_k2f_kernel, kb=kb),
        out_shape=jax.ShapeDtypeStruct((2, n1, n2, cols), F32),
        grid=(cols // ct, n1 // kb),
        in_specs=[
            spec(0),
            spec(1),
            pl.BlockSpec((1, ct), lambda c, k: (0, c)),
            pl.BlockSpec((4 * n2, 2 * n2), lambda c, k: (0, 0)),
        ],
        out_specs=pl.BlockSpec((2, kb, n2, ct), lambda c, k: (0, k, 0, c)),
        compiler_params=_cparams(2),
        name="hy_filter_spectrum",
    )(a6, a6, l1, f_fwd)


def _hyena(hv, hx1, hx2, skip, filt_params, seq_len):
    n1, _, g_fwd, g_inv, f_fwd, f_inv = _fft_tables(seq_len)
    g_fwd, g_inv, f_fwd, f_inv = (jnp.asarray(m) for m in (g_fwd, g_inv, f_fwd, f_inv))
    taps, l1 = _filter_taps(seq_len, *filt_params)
    kf = _filter_spectrum(_fft_stage_a(taps, g_fwd, n1), l1, f_fwd, n1)
    as_pairs = lambda a: a.reshape((a.shape[0] // 2, 2) + a.shape[1:])
    z0, x1, x2 = as_pairs(hv), as_pairs(hx1), as_pairs(hx2)
    d = _fft_stage_c(_fft_stage_a(z0, g_fwd, n1), kf, f_fwd, f_inv, 0, n1)
    skip_rows = skip.reshape(HY_ORDER, D_MIX // LANES, 1, LANES)
    z1, a = _fft_stage_a_inverse(d, g_inv, z0, x1, skip_rows[0], n1, g_fwd=g_fwd)
    d = _fft_stage_c(a, kf, f_fwd, f_inv, 1, n1)
    (z2,) = _fft_stage_a_inverse(d, g_inv, z1, x2, skip_rows[1], n1)
    return z2.reshape(hv.shape)


def _outproj_kernel(x_ref, an_ref, hz_ref, mod_ref, gain_ref, wa_ref, wh_ref, o_ref):
    hn = _rms(_load_slabs(hz_ref), gain_ref[...]).astype(BF16)
    mixed = (jnp.dot(an_ref[...], wa_ref[...], preferred_element_type=F32)
             + jnp.dot(hn, wh_ref[...], preferred_element_type=F32))
    o_ref[...] = x_ref[...] + mod_ref[5:6, :] * mixed


def _outproj(x2d, attn_n, hz, mod, hy_gain, w_attn, w_hy, seq_len):
    t = x2d.shape[0]
    tm = TOKEN_TILE
    tiles_per_seq = seq_len // tm
    const = lambda shape: pl.BlockSpec(shape, lambda i: (0, 0))
    return pl.pallas_call(
        _outproj_kernel,
        out_shape=jax.ShapeDtypeStruct((t, D_MODEL), F32),
        grid=(t // tm,),
        in_specs=[
            pl.BlockSpec((tm, D_MODEL), lambda i: (i, 0)),
            pl.BlockSpec((tm, D_MIX), lambda i: (i, 0)),
            pl.BlockSpec((None, SLABS, D_MIX // LANES, tm // LANE_BLOCK * FFT_NB, LANES),
                         lambda i: (i // tiles_per_seq, 0, 0, i % tiles_per_seq, 0)),
            pl.BlockSpec((None, N_MOD, D_MODEL), lambda i: (i * tm // seq_len, 0, 0)),
            const((1, D_MIX)),
            const((D_MIX, D_MODEL)),
            const((D_MIX, D_MODEL)),
        ],
        out_specs=pl.BlockSpec((tm, D_MODEL), lambda i: (i, 0)),
        compiler_params=_cparams(1),
        name="outproj",
    )(x2d, attn_n, hz, mod, hy_gain.reshape(1, D_MIX), w_attn, w_hy)


def _trunk(x, mod, p, final_norm):
    batch, seq_len, _ = x.shape
    x2d = x.reshape(batch * seq_len, D_MODEL)
    x2d = _ffn(x2d, mod, p["ffn1_norm"], p["ffn1_w_gate"], p["ffn1_w_up"], p["ffn1_w_down"],
               seq_len, mod_base=0)
    q, k, v, hv, hx1, hx2 = _inproj(x2d, mod, p["mix_norm"], p["w_qkv"], p["w_hy"],
                                    p["hy_conv_w"], p["hy_conv_b"], seq_len)
    attn_n = _attention(q, k, v, p["attn_bias"], p["attn_out_norm"], batch, seq_len)
    hz = _hyena(hv, hx1, hx2, p["hy_skip"], p["hy_filter"], seq_len)
    x2d = _outproj(x2d, attn_n, hz, mod, p["hy_out_norm"], p["w_out_attn"], p["w_out_hy"], seq_len)
    y = _ffn(x2d, mod, p["ffn2_norm"], p["ffn2_w_gate"], p["ffn2_w_up"], p["ffn2_w_down"],
             seq_len, mod_base=6, final_gain=final_norm)
    return y.reshape(batch, seq_len, D_MODEL)


def kernel(x_prompt, x_sample, c_prompt, c_sample, w_ada, b_ada, ffn1_norm, ffn1_w_gate, ffn1_w_up,
           ffn1_w_down, mix_norm, w_in, na_rpb, hy_conv_w, hy_conv_b, hy_w1, hy_b1, hy_w2, hy_b2,
           hy_w3, hy_b3, hy_wo, hy_sin_freq, hy_skip, attn_out_norm, hy_out_norm, w_out, ffn2_norm,
           ffn2_w_gate, ffn2_w_up, ffn2_w_down, final_norm):
    assert w_ada.shape[0] == 1, "single-layer encoder"
    n_prompt = c_prompt.shape[0]
    mod_all = _ada(jnp.concatenate([c_prompt, c_sample], axis=0), w_ada[0], b_ada[0])
    mod_all = mod_all.reshape(-1, N_MOD, D_MODEL)
    bf = lambda w: w[0].astype(BF16)
    p = {
        "ffn1_norm": ffn1_norm[0], "ffn1_w_gate": bf(ffn1_w_gate), "ffn1_w_up": bf(ffn1_w_up),
        "ffn1_w_down": bf(ffn1_w_down),
        "mix_norm": mix_norm[0],
        "w_qkv": w_in[0, :, :3 * D_MIX].astype(BF16), "w_hy": w_in[0, :, 3 * D_MIX:].astype(BF16),
        "attn_bias": _attn_bias_table(na_rpb[0]),
        "hy_conv_w": hy_conv_w[0], "hy_conv_b": hy_conv_b[0],
        "hy_filter": (hy_w1[0], hy_b1[0], hy_w2[0], hy_b2[0], hy_w3[0], hy_b3[0], hy_wo[0],
                      hy_sin_freq[0]),
        "hy_skip": hy_skip[0],
        "attn_out_norm": attn_out_norm[0], "hy_out_norm": hy_out_norm[0],
        "w_out_attn": w_out[0, :D_MIX].astype(BF16), "w_out_hy": w_out[0, D_MIX:].astype(BF16),
        "ffn2_norm": ffn2_norm[0], "ffn2_w_gate": bf(ffn2_w_gate), "ffn2_w_up": bf(ffn2_w_up),
        "ffn2_w_down": bf(ffn2_w_down),
    }
    y_prompt = _trunk(x_prompt, mod_all[:n_prompt], p, final_norm)
    y_sample = _trunk(x_sample, mod_all[n_prompt:], p, final_norm)
    return (y_prompt, y_sample)
```

```python
import functools
import math

import ml_dtypes
import numpy as np
import jax
import jax.numpy as jnp
from jax import lax
from jax.experimental import pallas as pl
from jax.experimental.pallas import tpu as pltpu

F32 = jnp.float32
BF16 = jnp.bfloat16
HIGHEST = lax.Precision.HIGHEST

D_MODEL = 1024
GRID_W = 64
D_MIX = 512
NA_HEADS = 8
NA_HEAD_DIM = D_MIX // NA_HEADS
WIN_H = 8
WIN_W = 16
HY_ORDER = 2
HY_BANDS = 8
HY_EMB = 1 + 2 * HY_BANDS
HY_FAST_DECAY = 0.3
HY_SLOW_DECAY = 1.5
HY_TARGET = 1e-2
D_FF = ((8 * D_MODEL // 3 + 127) // 128) * 128
N_MOD = 9
EPS = 1e-6
NEG_INF = -1e30

V7X_VMEM_LIMIT_BYTES = 56 * 1024 * 1024
TOKEN_TILE = 512
V7X_MXU_WIDTH = 256
_FF_SPLIT = (D_FF // V7X_MXU_WIDTH + 1) // 2 * V7X_MXU_WIDTH
FF_CHUNKS = ((0, _FF_SPLIT), (_FF_SPLIT, D_FF))
HALO = 16
LANE_BLOCK = 128
FFT_CT = 256
FFT_NB = 16
FFT_KB = 8
FFT_K_GROUP = 4
FILT_TILE = 512
FILT_HALF = 64
FILT_PAD = 2 * FILT_HALF


def _cparams(n_axes):
    return pltpu.CompilerParams(
        dimension_semantics=("arbitrary",) * n_axes,
        vmem_limit_bytes=V7X_VMEM_LIMIT_BYTES,
    )


def _rms(x, gain):
    ms = jnp.mean(x * x, axis=-1, keepdims=True)
    return x * lax.rsqrt(ms + EPS) * gain


def _silu(x):
    return x / (1.0 + jnp.exp(-x))


SLABS = LANE_BLOCK // FFT_NB
LANES = 128


def _store_slabs(ref, tile):
    for i1 in range(tile.shape[0] // LANE_BLOCK):
        for t in range(tile.shape[1] // LANES):
            rows = tile[i1 * LANE_BLOCK:(i1 + 1) * LANE_BLOCK, t * LANES:(t + 1) * LANES]
            ref[:, t, i1 * FFT_NB:(i1 + 1) * FFT_NB, :] = rows.reshape(SLABS, FFT_NB, LANES)


def _load_slabs(ref):
    _, tiles, rows, _ = ref.shape
    return jnp.concatenate(
        [jnp.concatenate([ref[:, t, i1 * FFT_NB:(i1 + 1) * FFT_NB, :].reshape(LANE_BLOCK, LANES)
                          for t in range(tiles)], axis=1)
         for i1 in range(rows // FFT_NB)], axis=0)


def _load_strided(ref, lead, start, size):
    tiles = ref.shape[len(lead)]
    return jnp.concatenate([ref[lead + (t, pl.ds(start, size, stride=FFT_NB), slice(None))]
                            for t in range(tiles)], axis=1)


def _store_strided(ref, start, val):
    for t in range(ref.shape[0]):
        ref[t, pl.ds(start, val.shape[0], stride=FFT_NB), :] = val[:, t * LANES:(t + 1) * LANES]


def _ada_kernel(c_ref, w_ref, b_ref, o_ref):
    s = _silu(c_ref[...])
    o_ref[...] = jnp.dot(s, w_ref[...], precision=HIGHEST, preferred_element_type=F32) + b_ref[...]


def _ada(c_all, w_ada, b_ada):
    rows = c_all.shape[0]
    n_out = w_ada.shape[1]
    tn = D_MODEL
    return pl.pallas_call(
        _ada_kernel,
        out_shape=jax.ShapeDtypeStruct((rows, n_out), F32),
        grid=(n_out // tn,),
        in_specs=[
            pl.BlockSpec((rows, D_MODEL), lambda j: (0, 0)),
            pl.BlockSpec((D_MODEL, tn), lambda j: (0, j)),
            pl.BlockSpec((1, tn), lambda j: (0, j)),
        ],
        out_specs=pl.BlockSpec((rows, tn), lambda j: (0, j)),
        compiler_params=_cparams(1),
        name="ada_mod",
    )(c_all, w_ada, b_ada.reshape(1, n_out))


def _ffn_residual(x, mod_ref, mod_base, gain_ref, wg_ref, wu_ref, wd_ref):
    shift = mod_ref[mod_base:mod_base + 1, :]
    scale = mod_ref[mod_base + 1:mod_base + 2, :]
    gate = mod_ref[mod_base + 2:mod_base + 3, :]
    hb = (_rms(x, gain_ref[...]) * (1.0 + scale) + shift).astype(BF16)
    acc = None
    for c0, c1 in FF_CHUNKS:
        g = jnp.dot(hb, wg_ref[:, c0:c1], preferred_element_type=F32)
        u = jnp.dot(hb, wu_ref[:, c0:c1], preferred_element_type=F32)
        a = (_silu(g) * u).astype(BF16)
        d = jnp.dot(a, wd_ref[c0:c1, :], preferred_element_type=F32)
        acc = d if acc is None else acc + d
    return x + 0.5 * gate * acc


def _ffn1_kernel(x_ref, mod_ref, gain_ref, wg_ref, wu_ref, wd_ref, o_ref):
    o_ref[...] = _ffn_residual(x_ref[...], mod_ref, 0, gain_ref, wg_ref, wu_ref, wd_ref)


def _mix_ffn2_kernel(x_ref, an_ref, hz_ref, mod_ref, hy_gain_ref, wa_ref, wh_ref,
                     gain_ref, wg_ref, wu_ref, wd_ref, fn_ref, o_ref):
    hn = _rms(_load_slabs(hz_ref), hy_gain_ref[...]).astype(BF16)
    mixed = (jnp.dot(an_ref[...], wa_ref[...], preferred_element_type=F32)
             + jnp.dot(hn, wh_ref[...], preferred_element_type=F32))
    x = x_ref[...] + mod_ref[5:6, :] * mixed
    y = _ffn_residual(x, mod_ref, 6, gain_ref, wg_ref, wu_ref, wd_ref)
    o_ref[...] = _rms(y, fn_ref[...])


def _resident(shape):
    return pl.BlockSpec(shape, lambda i: (0, 0), pipeline_mode=pl.Buffered(1))


def _token_spec(tm, width):
    return pl.BlockSpec((tm, width), lambda i: (i, 0))


def _mod_spec(tm, seq_len):
    return pl.BlockSpec((None, N_MOD, D_MODEL), lambda i: (i * tm // seq_len, 0, 0))


def _ffn1(x2d, mod, gain, wg, wu, wd, seq_len):
    t = x2d.shape[0]
    tm = TOKEN_TILE
    return pl.pallas_call(
        _ffn1_kernel,
        out_shape=jax.ShapeDtypeStruct((t, D_MODEL), F32),
        grid=(t // tm,),
        in_specs=[
            _token_spec(tm, D_MODEL), _mod_spec(tm, seq_len), _resident((1, D_MODEL)),
            _resident((D_MODEL, D_FF)), _resident((D_MODEL, D_FF)), _resident((D_FF, D_MODEL)),
        ],
        out_specs=_token_spec(tm, D_MODEL),
        compiler_params=_cparams(1),
        name="ffn1",
    )(x2d, mod, gain.reshape(1, D_MODEL), wg, wu, wd)


def _mix_ffn2(x2d, attn_n, hz, mod, hy_gain, w_attn, w_hy, gain, wg, wu, wd, final_gain, seq_len):
    t = x2d.shape[0]
    tm = TOKEN_TILE
    tiles_per_seq = seq_len // tm
    return pl.pallas_call(
        _mix_ffn2_kernel,
        out_shape=jax.ShapeDtypeStruct((t, D_MODEL), F32),
        grid=(t // tm,),
        in_specs=[
            _token_spec(tm, D_MODEL),
            _token_spec(tm, D_MIX),
            pl.BlockSpec((None, SLABS, D_MIX // LANES, tm // LANE_BLOCK * FFT_NB, LANES),
                         lambda i: (i // tiles_per_seq, 0, 0, i % tiles_per_seq, 0)),
            _mod_spec(tm, seq_len),
            _resident((1, D_MIX)), _resident((D_MIX, D_MODEL)), _resident((D_MIX, D_MODEL)),
            _resident((1, D_MODEL)),
            _resident((D_MODEL, D_FF)), _resident((D_MODEL, D_FF)), _resident((D_FF, D_MODEL)),
            _resident((1, D_MODEL)),
        ],
        out_specs=_token_spec(tm, D_MODEL),
        compiler_params=_cparams(1),
        name="mix_ffn2",
    )(x2d, attn_n, hz, mod, hy_gain.reshape(1, D_MIX), w_attn, w_hy, gain.reshape(1, D_MODEL),
      wg, wu, wd, final_gain.reshape(1, D_MODEL))


def _inproj_kernel(x_ref, xp_ref, xn_ref, mod_ref, gain_ref, wqkv_ref, why_ref, cw_ref, cb_ref,
                   q_ref, k_ref, v_ref, hv_ref, hx1_ref, hx2_ref, ext_ref, u_ref,
                   *, tiles_per_seq, tm):
    pos = pl.program_id(0) % tiles_per_seq
    gain = gain_ref[...]
    shift = mod_ref[3:4, :]
    scale = 1.0 + mod_ref[4:5, :]

    def normed(x):
        return _rms(x, gain) * scale + shift

    hb = normed(x_ref[...]).astype(BF16)
    qkv = jnp.dot(hb, wqkv_ref[...], preferred_element_type=F32)
    q_ref[...] = (qkv[:, :D_MIX] * (NA_HEAD_DIM ** -0.5)).astype(BF16)
    k_ref[...] = qkv[:, D_MIX:2 * D_MIX].astype(BF16)
    v_ref[...] = qkv[:, 2 * D_MIX:].astype(BF16)

    has_prev = jnp.where(pos != 0, 1.0, 0.0)
    has_next = jnp.where(pos != tiles_per_seq - 1, 1.0, 0.0)
    ext_ref[0:HALO, :] = (normed(xp_ref[...]) * has_prev).astype(BF16)
    ext_ref[HALO:HALO + tm, :] = hb
    ext_ref[HALO + tm:, :] = (normed(xn_ref[...]) * has_next).astype(BF16)
    u_ref[...] = jnp.dot(ext_ref[...], why_ref[...], preferred_element_type=F32)
    conv = (cb_ref[...]
            + u_ref[pl.ds(HALO - 1, tm), :] * cw_ref[0:1, :]
            + u_ref[pl.ds(HALO, tm), :] * cw_ref[1:2, :]
            + u_ref[pl.ds(HALO + 1, tm), :] * cw_ref[2:3, :])
    for part, ref in enumerate((hv_ref, hx1_ref, hx2_ref)):
        _store_slabs(ref, conv[:, part * D_MIX:(part + 1) * D_MIX])


def _inproj(x2d, mod, gain, w_qkv, w_hy, conv_w, conv_b, seq_len):
    t = x2d.shape[0]
    tm = TOKEN_TILE
    n_halo_blocks = t // HALO
    per_tile = tm // HALO
    const = lambda shape: pl.BlockSpec(shape, lambda i: (0, 0))
    tiles_per_seq = seq_len // tm
    tile_out = jax.ShapeDtypeStruct((t, D_MIX), BF16)
    tile_spec = pl.BlockSpec((tm, D_MIX), lambda i: (i, 0))
    slab_rows = tm // LANE_BLOCK * FFT_NB
    slab_out = jax.ShapeDtypeStruct((t // seq_len, SLABS, D_MIX // LANES, seq_len // SLABS, LANES), F32)
    slab_spec = pl.BlockSpec((None, SLABS, D_MIX // LANES, slab_rows, LANES),
                             lambda i: (i // tiles_per_seq, 0, 0, i % tiles_per_seq, 0))
    return pl.pallas_call(
        functools.partial(_inproj_kernel, tiles_per_seq=tiles_per_seq, tm=tm),
        out_shape=[tile_out] * 3 + [slab_out] * 3,
        grid=(t // tm,),
        in_specs=[
            pl.BlockSpec((tm, D_MODEL), lambda i: (i, 0)),
            pl.BlockSpec((HALO, D_MODEL), lambda i: (jnp.maximum(i * per_tile - 1, 0), 0)),
            pl.BlockSpec((HALO, D_MODEL),
                         lambda i: (jnp.minimum((i + 1) * per_tile, n_halo_blocks - 1), 0)),
            pl.BlockSpec((None, N_MOD, D_MODEL), lambda i: (i * tm // seq_len, 0, 0)),
            const((1, D_MODEL)),
            const((D_MODEL, 3 * D_MIX)),
            const((D_MODEL, 3 * D_MIX)),
            const((3, 3 * D_MIX)),
            const((1, 3 * D_MIX)),
        ],
        out_specs=[tile_spec] * 3 + [slab_spec] * 3,
        scratch_shapes=[
            pltpu.VMEM((tm + 2 * HALO, D_MODEL), BF16),
            pltpu.VMEM((tm + 2 * HALO, 3 * D_MIX), F32),
        ],
        compiler_params=_cparams(1),
        name="inproj",
    )(x2d, x2d, x2d, mod, gain.reshape(1, D_MODEL), w_qkv, w_hy, conv_w,
      conv_b.reshape(1, 3 * D_MIX))


ROW_TOKENS = GRID_W
GROUP_ROWS = WIN_H
GROUP_TOKENS = GROUP_ROWS * ROW_TOKENS
WIN_TOKENS = WIN_H * ROW_TOKENS


def _attn_bias_table(rpb):
    n_heads, n_drow, n_dcol = rpb.shape
    period = 2 * GRID_W - 1
    wrapped = jnp.concatenate([rpb[..., WIN_W - 1:], jnp.zeros((n_heads, n_drow, period - n_dcol), F32),
                               rpb[..., :WIN_W - 1]], axis=-1).astype(F32)
    toeplitz = jnp.tile(wrapped, GRID_W)[..., :GRID_W * (period - 1)]
    toeplitz = toeplitz.reshape(n_heads, n_drow, GRID_W, period - 1)[..., :GRID_W]
    qc = np.arange(GRID_W)[:, None]
    kc = np.arange(GRID_W)[None, :]
    win_start = np.clip(qc - WIN_W // 2, 0, GRID_W - WIN_W)
    col_ok = (kc >= win_start) & (kc < win_start + WIN_W)
    masked = jnp.where(col_ok[None, None], toeplitz, NEG_INF)
    bias = jnp.stack([masked[:, o:o + WIN_H] for o in range(WIN_H)], axis=0)
    bias = jnp.transpose(bias, (0, 1, 3, 2, 4))
    return bias.reshape(WIN_H, NA_HEADS, GRID_W, WIN_TOKENS)


NATTEN_UNROLL = 4
KV_WINDOW_ROWS = 3 * GROUP_ROWS


def _kv_window_start(g, rows):
    return jnp.clip(g * GROUP_ROWS - GROUP_ROWS, 0, rows - KV_WINDOW_ROWS)


def _attn_kernel(q_ref, k_ref, v_ref, bias_ref, gain_ref, o_ref, acc, *, rows):
    g = pl.program_id(1)
    win_start = _kv_window_start(g, rows)
    first_head = lax.broadcasted_iota(jnp.int32, (ROW_TOKENS, 2 * NA_HEAD_DIM), 1) < NA_HEAD_DIM
    head_pairs = [slice(hp * 2 * NA_HEAD_DIM, (hp + 1) * 2 * NA_HEAD_DIM) for hp in range(NA_HEADS // 2)]

    def row_body(rr, carry):
        r = g * GROUP_ROWS + rr
        start = jnp.clip(r - WIN_H // 2, 0, rows - WIN_H)
        koff = pl.multiple_of((start - win_start) * ROW_TOKENS, ROW_TOKENS)
        row_class = start - r + WIN_H - 1
        qoff = pl.multiple_of(rr * ROW_TOKENS, ROW_TOKENS)
        scores = []
        for hp, lanes in enumerate(head_pairs):
            q2 = q_ref[pl.ds(qoff, ROW_TOKENS), lanes]
            kw = k_ref[pl.ds(koff, WIN_TOKENS), lanes]
            for hh in range(2):
                keep = first_head if hh == 0 else jnp.logical_not(first_head)
                qm = jnp.where(keep, q2, jnp.zeros_like(q2))
                s = lax.dot_general(qm, kw, (((1,), (1,)), ((), ())), preferred_element_type=F32)
                scores.append(s + bias_ref[row_class, 2 * hp + hh])
        probs, denoms = [], []
        for s in scores:
            p = jnp.exp(s - jnp.max(s, axis=-1, keepdims=True))
            denoms.append(jnp.sum(p, axis=-1, keepdims=True))
            probs.append(p.astype(BF16))
        for hp, lanes in enumerate(head_pairs):
            vw = v_ref[pl.ds(koff, WIN_TOKENS), lanes]
            outs = [jnp.dot(probs[2 * hp + hh], vw, preferred_element_type=F32) / denoms[2 * hp + hh]
                    for hh in range(2)]
            acc[pl.ds(qoff, ROW_TOKENS), lanes] = jnp.where(first_head, outs[0], outs[1])
        return carry

    lax.fori_loop(0, GROUP_ROWS, row_body, 0, unroll=NATTEN_UNROLL)
    o_ref[...] = _rms(acc[...], gain_ref[...]).astype(BF16)


def _attention(q, k, v, bias, gain, batch, seq_len):
    rows = seq_len // GRID_W
    q3 = q.reshape(batch, seq_len, D_MIX)
    k3 = k.reshape(batch, seq_len, D_MIX)
    v3 = v.reshape(batch, seq_len, D_MIX)
    cur = pl.BlockSpec((None, GROUP_TOKENS, D_MIX), lambda b, g: (b, g, 0))
    window = pl.BlockSpec((None, pl.Element(KV_WINDOW_ROWS * ROW_TOKENS), pl.Element(D_MIX)),
                          lambda b, g: (b, _kv_window_start(g, rows) * ROW_TOKENS, 0))
    out = pl.pallas_call(
        functools.partial(_attn_kernel, rows=rows),
        out_shape=jax.ShapeDtypeStruct((batch, seq_len, D_MIX), BF16),
        grid=(batch, rows // GROUP_ROWS),
        in_specs=[
            cur, window, window,
            pl.BlockSpec((WIN_H, NA_HEADS, GRID_W, WIN_TOKENS), lambda b, g: (0, 0, 0, 0),
                         pipeline_mode=pl.Buffered(1)),
            pl.BlockSpec((1, D_MIX), lambda b, g: (0, 0)),
        ],
        out_specs=cur,
        scratch_shapes=[pltpu.VMEM((GROUP_TOKENS, D_MIX), F32)],
        compiler_params=_cparams(2),
        name="natten",
    )(q3, k3, v3, bias, gain.reshape(1, D_MIX))
    return out.reshape(batch * seq_len, D_MIX)


def _split_hi_lo(x):
    hi = x.astype(ml_dtypes.bfloat16)
    lo = (x - hi.astype(np.float64)).astype(ml_dtypes.bfloat16)
    return hi, lo


def _stack_hi_lo(m):
    hi, lo = _split_hi_lo(m)
    return np.concatenate([hi, lo], axis=-2)


def _embed(re, im):
    return np.concatenate([np.concatenate([re, -im], axis=-1),
                           np.concatenate([im, re], axis=-1)], axis=-2)


@functools.lru_cache(maxsize=None)
def _fft_tables(seq_len):
    n = 2 * seq_len
    n2 = LANE_BLOCK
    n1 = n // n2
    i2 = np.arange(n2)[:, None, None]
    k1 = np.arange(n1)[None, :, None]
    i1 = np.arange(n1 // 2)[None, None, :]
    ang = -2.0 * np.pi * ((k1 * (n2 * i1 + i2)) % n) / n
    gr, gi = np.cos(ang), np.sin(ang)
    g_fwd = _stack_hi_lo(_embed(gr, gi))
    g_inv = _stack_hi_lo(_embed(np.swapaxes(gr, 1, 2) / n, -np.swapaxes(gi, 1, 2) / n))
    jk = np.outer(np.arange(n2), np.arange(n2))
    ang2 = -2.0 * np.pi * (jk % n2) / n2
    fr, fi = np.cos(ang2), np.sin(ang2)
    f_fwd = _stack_hi_lo(_embed(fr, fi))
    f_inv = _stack_hi_lo(_embed(fr, -fi))
    return n1, n2, g_fwd, g_inv, f_fwd, f_inv


def _dft3(m_hl, x, m):
    x_hi = x.astype(BF16)
    x_lo = (x - x_hi.astype(F32)).astype(BF16)
    t = jnp.dot(m_hl, x_hi, preferred_element_type=F32)
    return t[:m] + t[m:] + jnp.dot(m_hl[:m], x_lo, preferred_element_type=F32)


def _stage_a_forward(x_ref, g_ref, a_ref, *, n1):
    for i in range(FFT_NB):
        x = jnp.concatenate([_load_strided(x_ref, (0,), i, n1 // 2),
                             _load_strided(x_ref, (1,), i, n1 // 2)], axis=0)
        _store_strided(a_ref, i, _dft3(g_ref[i], x, 2 * n1))


def _stage_a_inverse(d_ref, gi_ref, y_ref, *, n1):
    for i in range(FFT_NB):
        _store_strided(y_ref, i, _dft3(gi_ref[i], _load_strided(d_ref, (), i, 2 * n1), n1))


def _k1_kernel(x_ref, g_ref, a_ref, *, n1):
    _stage_a_forward(x_ref, g_ref, a_ref, n1=n1)


FFT_TILES = FFT_CT // LANES


def _seq_spec(n1):
    return pl.BlockSpec((None, 2, None, FFT_TILES, n1 // 2 * FFT_NB, LANES),
                        lambda c, j, p: (p, 0, j, c, 0, 0))


def _spec_spec(n1):
    return pl.BlockSpec((None, None, FFT_TILES, 2 * n1 * FFT_NB, LANES), lambda c, j, p: (p, j, c, 0, 0))


def _fft_stage_a(x6, g_fwd, n1):
    pairs, _, slabs, tiles, _, _ = x6.shape
    return pl.pallas_call(
        functools.partial(_k1_kernel, n1=n1),
        out_shape=jax.ShapeDtypeStruct((pairs, slabs, tiles, 2 * n1 * FFT_NB, LANES), F32),
        grid=(tiles // FFT_TILES, slabs, pairs),
        in_specs=[
            _seq_spec(n1),
            pl.BlockSpec((FFT_NB, 4 * n1, n1), lambda c, j, p: (j, 0, 0)),
        ],
        out_specs=_spec_spec(n1),
        compiler_params=_cparams(3),
        name="hy_stage_a",
    )(x6, g_fwd)


def _load_low_index(ref, part, kk):
    return jnp.concatenate([ref[:, t, part, kk].reshape(LANE_BLOCK, LANES) for t in range(ref.shape[1])],
                           axis=1)


def _store_low_index(ref, part, kk, val):
    for t in range(ref.shape[1]):
        ref[:, t, part, kk] = val[:, t * LANES:(t + 1) * LANES].reshape(SLABS, FFT_NB, LANES)


def _k2_kernel(a_ref, kf_ref, f_ref, fi_ref, d_ref, *, kb):
    n2 = LANE_BLOCK
    f_hl = f_ref[...]
    fi_hl = fi_ref[...]

    def body(group, carry):
        ks = [group * FFT_K_GROUP + u for u in range(FFT_K_GROUP)]
        spectra = [_dft3(f_hl, jnp.concatenate([_load_low_index(a_ref, 0, kk),
                                                _load_low_index(a_ref, 1, kk)], axis=0), 2 * n2)
                   for kk in ks]
        products = []
        for kk, c in zip(ks, spectra):
            cr, ci = c[:n2], c[n2:]
            kr, ki = kf_ref[0, kk], kf_ref[1, kk]
            products.append(jnp.concatenate([cr * kr - ci * ki, cr * ki + ci * kr], axis=0))
        for kk, y in zip(ks, products):
            d = _dft3(fi_hl, y, 2 * n2)
            _store_low_index(d_ref, 0, kk, d[:n2])
            _store_low_index(d_ref, 1, kk, d[n2:])
        return carry

    lax.fori_loop(0, kb // FFT_K_GROUP, body, 0)


def _fft_stage_c(a4, kf, f_fwd, f_inv, order, n1):
    pairs, slabs, tiles, _, _ = a4.shape
    n2 = LANE_BLOCK
    kb = FFT_KB
    ch_blocks = tiles // FFT_TILES
    a7 = a4.reshape(pairs, slabs, tiles, 2, n1, FFT_NB, LANES)
    spec = pl.BlockSpec((None, slabs, FFT_TILES, 2, kb, FFT_NB, LANES),
                        lambda c, k, p: (p, 0, c, 0, k, 0, 0))
    d7 = pl.pallas_call(
        functools.partial(_k2_kernel, kb=kb),
        out_shape=jax.ShapeDtypeStruct(a7.shape, F32),
        grid=(ch_blocks, n1 // kb, pairs),
        in_specs=[
            spec,
            pl.BlockSpec((2, kb, n2, FFT_CT), lambda c, k, p: (0, k, 0, order * ch_blocks + c)),
            pl.BlockSpec((4 * n2, 2 * n2), lambda c, k, p: (0, 0)),
            pl.BlockSpec((4 * n2, 2 * n2), lambda c, k, p: (0, 0)),
        ],
        out_specs=spec,
        compiler_params=_cparams(3),
        name="hy_stage_c",
    )(a7, kf, f_fwd, f_inv)
    return d7.reshape(a4.shape)


def _k3_kernel(d_ref, gi_ref, z_ref, x_ref, skip_ref, *rest, n1, forward):
    if forward:
        g_ref, o_ref, a_ref, y_ref = rest
    else:
        o_ref, y_ref = rest
    _stage_a_inverse(d_ref, gi_ref, y_ref, n1=n1)
    rows = n1 // 2 * FFT_NB
    for part in range(2):
        for t in range(FFT_TILES):
            conv = y_ref[t, part * rows:(part + 1) * rows, :]
            o_ref[part, t] = x_ref[part, t] * (conv + skip_ref[t] * z_ref[part, t])
    if forward:
        _stage_a_forward(o_ref, g_ref, a_ref, n1=n1)


def _fft_stage_a_inverse(d4, g_inv, z6, x6, skip_row, n1, g_fwd=None):
    pairs, slabs, tiles, _, _ = d4.shape
    forward = g_fwd is not None
    in_specs = [
        _spec_spec(n1),
        pl.BlockSpec((FFT_NB, 2 * n1, 2 * n1), lambda c, j, p: (j, 0, 0)),
        _seq_spec(n1),
        _seq_spec(n1),
        pl.BlockSpec((FFT_TILES, 1, LANES), lambda c, j, p: (c, 0, 0)),
    ]
    args = [d4, g_inv, z6, x6, skip_row]
    out_shape = [jax.ShapeDtypeStruct(z6.shape, F32)]
    out_specs = [_seq_spec(n1)]
    if forward:
        in_specs.append(pl.BlockSpec((FFT_NB, 4 * n1, n1), lambda c, j, p: (j, 0, 0)))
        args.append(g_fwd)
        out_shape.append(jax.ShapeDtypeStruct(d4.shape, F32))
        out_specs.append(_spec_spec(n1))
    return pl.pallas_call(
        functools.partial(_k3_kernel, n1=n1, forward=forward),
        out_shape=out_shape,
        grid=(tiles // FFT_TILES, slabs, pairs),
        in_specs=in_specs,
        out_specs=out_specs,
        scratch_shapes=[pltpu.VMEM((FFT_TILES, n1 * FFT_NB, LANES), F32)],
        compiler_params=_cparams(3),
        name="hy_stage_a_inv_fwd" if forward else "hy_stage_a_inv",
    )(*args)


def _filt_kernel(z_ref, w1_ref, b1_ref, w2_ref, b2_ref, w3_ref, b3_ref, wo_ref, freq_ref,
                 delta_ref, h_ref, l1_ref, *, tl):
    i = pl.program_id(0)
    freq = freq_ref[...]

    def dot(a, b):
        return jnp.dot(a, b, precision=HIGHEST, preferred_element_type=F32)

    z = z_ref[...]
    h = jnp.sin(freq * (dot(z, w1_ref[...]) + b1_ref[...]))
    h = jnp.sin(freq * (dot(h, w2_ref[...]) + b2_ref[...]))
    h = jnp.sin(freq * (dot(h, w3_ref[...]) + b3_ref[...]))
    hf = dot(h, wo_ref[0]) * jnp.exp(-z[:, 0:1] * delta_ref[...])
    hb = dot(h, wo_ref[1]) * jnp.exp(-z[:, FILT_HALF:FILT_HALF + 1] * delta_ref[...])
    row = i * tl + lax.broadcasted_iota(jnp.int32, (tl, 1), 0)
    hb = jnp.where(row == 0, 0.0, hb)
    _store_slabs(h_ref.at[0, 0], hf)
    _store_slabs(h_ref.at[1, 0], hb)
    h_ref[0, 1] = jnp.zeros(h_ref.shape[2:], F32)
    h_ref[1, 1] = jnp.zeros(h_ref.shape[2:], F32)

    @pl.when(i == 0)
    def _():
        l1_ref[...] = jnp.zeros_like(l1_ref)

    l1_ref[...] += (jnp.sum(jnp.abs(hf), axis=0, keepdims=True)
                    + jnp.sum(jnp.abs(hb), axis=0, keepdims=True))


def _pad_to(x, shape):
    return jnp.pad(x, [(0, s - d) for d, s in zip(x.shape, shape)])


def _filter_taps(seq_len, w1, b1, w2, b2, w3, b3, wo, freq):
    t = jnp.linspace(0.0, 1.0, seq_len, dtype=F32)[:, None]
    w = 2.0 * math.pi * jnp.arange(seq_len, dtype=F32)[:, None] / seq_len
    f = jnp.linspace(1e-4, HY_BANDS - 1, HY_BANDS, dtype=F32)[None, :]
    z = _pad_to(jnp.concatenate([t, jnp.cos(f * w), -jnp.sin(f * w)], axis=-1), (seq_len, FILT_HALF))
    z = jnp.concatenate([z, jnp.roll(z, 1, axis=0)], axis=1)
    deltas = jnp.abs(jnp.linspace(math.log(HY_TARGET) / HY_FAST_DECAY,
                                  math.log(HY_TARGET) / HY_SLOW_DECAY, D_MIX, dtype=F32))
    n_cols = HY_ORDER * D_MIX

    def both(m):
        m = _pad_to(m, (FILT_HALF, FILT_HALF))
        zero = jnp.zeros_like(m)
        return jnp.concatenate([jnp.concatenate([m, zero], axis=1),
                                jnp.concatenate([zero, m], axis=1)], axis=0)

    row = lambda v: jnp.tile(_pad_to(v.reshape(1, -1), (1, FILT_HALF)), (1, 2))
    wo_p = _pad_to(wo, (FILT_HALF, 2 * n_cols))
    zero = jnp.zeros((FILT_HALF, n_cols), F32)
    wo_dirs = jnp.stack([jnp.concatenate([wo_p[:, :n_cols], zero], axis=0),
                         jnp.concatenate([zero, wo_p[:, n_cols:]], axis=0)], axis=0)
    pad2 = (FILT_PAD, FILT_PAD)
    tl = FILT_TILE
    const = lambda shape: pl.BlockSpec(shape, lambda i: (0,) * len(shape))
    return pl.pallas_call(
        functools.partial(_filt_kernel, tl=tl),
        out_shape=[jax.ShapeDtypeStruct((2, 2, SLABS, n_cols // LANES, seq_len // SLABS, LANES), F32),
                   jax.ShapeDtypeStruct((1, n_cols), F32)],
        grid=(seq_len // tl,),
        in_specs=[
            pl.BlockSpec((tl, FILT_PAD), lambda i: (i, 0)),
            const(pad2), const((1, FILT_PAD)), const(pad2), const((1, FILT_PAD)),
            const(pad2), const((1, FILT_PAD)), const((2, FILT_PAD, n_cols)), const((1, FILT_PAD)),
            const((1, n_cols)),
        ],
        out_specs=[pl.BlockSpec((2, 2, SLABS, n_cols // LANES, tl // LANE_BLOCK * FFT_NB, LANES),
                                lambda i: (0, 0, 0, 0, i, 0)),
                   const((1, n_cols))],
        compiler_params=_cparams(1),
        name="hy_filter_taps",
    )(z, both(w1), row(b1), both(w2), row(b2), both(w3), row(b3), wo_dirs, row(freq),
      jnp.tile(deltas, HY_ORDER).reshape(1, n_cols))


def _k2f_kernel(af_ref, ab_ref, l1_ref, f_ref, kf_ref, *, kb):
    n2 = LANE_BLOCK
    f_hl = f_ref[...]
    inv_l1 = 1.0 / l1_ref[...]

    def spectrum(ref, kk):
        x = jnp.concatenate([_load_low_index(ref, 0, kk), _load_low_index(ref, 1, kk)], axis=0)
        return _dft3(f_hl, x, 2 * n2)

    def body(group, carry):
        ks = [group * FFT_K_GROUP + u for u in range(FFT_K_GROUP)]
        forward = [spectrum(af_ref, kk) for kk in ks]
        backward = [spectrum(ab_ref, kk) for kk in ks]
        for kk, cf, cb in zip(ks, forward, backward):
            kf_ref[0, kk] = (cf[:n2] + cb[:n2]) * inv_l1
            kf_ref[1, kk] = (cf[n2:] - cb[n2:]) * inv_l1
        return carry

    lax.fori_loop(0, kb // FFT_K_GROUP, body, 0)


def _filter_spectrum(a4, l1, f_fwd, n1):
    _, slabs, tiles, _, _ = a4.shape
    cols = tiles * LANES
    n2 = LANE_BLOCK
    kb, ct = FFT_KB, FFT_CT
    a6 = a4.reshape(2, slabs, tiles, 2, n1, FFT_NB, LANES)
    spec = lambda d: pl.BlockSpec((None, slabs, FFT_TILES, 2, kb, FFT_NB, LANES),
                                  lambda c, k: (d, 0, c, 0, k, 0, 0))
    return pl.pallas_call(
        functools.partial(_k2f_kernel, kb=kb),
        out_shape=jax.ShapeDtypeStruct((2, n1, n2, cols), F32),
        grid=(cols // ct, n1 // kb),
        in_specs=[
            spec(0),
            spec(1),
            pl.BlockSpec((1, ct), lambda c, k: (0, c)),
            pl.BlockSpec((4 * n2, 2 * n2), lambda c, k: (0, 0)),
        ],
        out_specs=pl.BlockSpec((2, kb, n2, ct), lambda c, k: (0, k, 0, c)),
        compiler_params=_cparams(2),
        name="hy_filter_spectrum",
    )(a6, a6, l1, f_fwd)


def _hyena(hv, hx1, hx2, skip, filt_params, seq_len):
    n1, _, g_fwd, g_inv, f_fwd, f_inv = _fft_tables(seq_len)
    g_fwd, g_inv, f_fwd, f_inv = (jnp.asarray(m) for m in (g_fwd, g_inv, f_fwd, f_inv))
    taps, l1 = _filter_taps(seq_len, *filt_params)
    kf = _filter_spectrum(_fft_stage_a(taps, g_fwd, n1), l1, f_fwd, n1)
    as_pairs = lambda a: a.reshape((a.shape[0] // 2, 2) + a.shape[1:])
    z0, x1, x2 = as_pairs(hv), as_pairs(hx1), as_pairs(hx2)
    d = _fft_stage_c(_fft_stage_a(z0, g_fwd, n1), kf, f_fwd, f_inv, 0, n1)
    skip_rows = skip.reshape(HY_ORDER, D_MIX // LANES, 1, LANES)
    z1, a = _fft_stage_a_inverse(d, g_inv, z0, x1, skip_rows[0], n1, g_fwd=g_fwd)
    d = _fft_stage_c(a, kf, f_fwd, f_inv, 1, n1)
    (z2,) = _fft_stage_a_inverse(d, g_inv, z1, x2, skip_rows[1], n1)
    return z2.reshape(hv.shape)


def _trunk(x, mod, p, final_norm):
    batch, seq_len, _ = x.shape
    x2d = x.reshape(batch * seq_len, D_MODEL)
    x2d = _ffn1(x2d, mod, p["ffn1_norm"], p["ffn1_w_gate"], p["ffn1_w_up"], p["ffn1_w_down"], seq_len)
    q, k, v, hv, hx1, hx2 = _inproj(x2d, mod, p["mix_norm"], p["w_qkv"], p["w_hy"],
                                    p["hy_conv_w"], p["hy_conv_b"], seq_len)
    attn_n = _attention(q, k, v, p["attn_bias"], p["attn_out_norm"], batch, seq_len)
    hz = _hyena(hv, hx1, hx2, p["hy_skip"], p["hy_filter"], seq_len)
    y = _mix_ffn2(x2d, attn_n, hz, mod, p["hy_out_norm"], p["w_out_attn"], p["w_out_hy"],
                  p["ffn2_norm"], p["ffn2_w_gate"], p["ffn2_w_up"], p["ffn2_w_down"], final_norm, seq_len)
    return y.reshape(batch, seq_len, D_MODEL)


def kernel(x_prompt, x_sample, c_prompt, c_sample, w_ada, b_ada, ffn1_norm, ffn1_w_gate, ffn1_w_up,
           ffn1_w_down, mix_norm, w_in, na_rpb, hy_conv_w, hy_conv_b, hy_w1, hy_b1, hy_w2, hy_b2,
           hy_w3, hy_b3, hy_wo, hy_sin_freq, hy_skip, attn_out_norm, hy_out_norm, w_out, ffn2_norm,
           ffn2_w_gate, ffn2_w_up, ffn2_w_down, final_norm):
    assert w_ada.shape[0] == 1, "single-layer encoder"
    n_prompt = c_prompt.shape[0]
    mod_all = _ada(jnp.concatenate([c_prompt, c_sample], axis=0), w_ada[0], b_ada[0])
    mod_all = mod_all.reshape(-1, N_MOD, D_MODEL)
    bf = lambda w: w[0].astype(BF16)
    p = {
        "ffn1_norm": ffn1_norm[0], "ffn1_w_gate": bf(ffn1_w_gate), "ffn1_w_up": bf(ffn1_w_up),
        "ffn1_w_down": bf(ffn1_w_down),
        "mix_norm": mix_norm[0],
        "w_qkv": w_in[0, :, :3 * D_MIX].astype(BF16), "w_hy": w_in[0, :, 3 * D_MIX:].astype(BF16),
        "attn_bias": _attn_bias_table(na_rpb[0]),
        "hy_conv_w": hy_conv_w[0], "hy_conv_b": hy_conv_b[0],
        "hy_filter": (hy_w1[0], hy_b1[0], hy_w2[0], hy_b2[0], hy_w3[0], hy_b3[0], hy_wo[0],
                      hy_sin_freq[0]),
        "hy_skip": hy_skip[0],
        "attn_out_norm": attn_out_norm[0], "hy_out_norm": hy_out_norm[0],
        "w_out_attn": w_out[0, :D_MIX].astype(BF16), "w_out_hy": w_out[0, D_MIX:].astype(BF16),
        "ffn2_norm": ffn2_norm[0], "ffn2_w_gate": bf(ffn2_w_gate), "ffn2_w_up": bf(ffn2_w_up),
        "ffn2_w_down": bf(ffn2_w_down),
    }
    y_prompt = _trunk(x_prompt, mod_all[:n_prompt], p, final_norm)
    y_sample = _trunk(x_sample, mod_all[n_prompt:], p, final_norm)
    return (y_prompt, y_sample)
```

```python
import functools
import math

import ml_dtypes
import numpy as np
import jax
import jax.numpy as jnp
from jax import lax
from jax.experimental import pallas as pl
from jax.experimental.pallas import tpu as pltpu

F32 = jnp.float32
BF16 = jnp.bfloat16
HIGHEST = lax.Precision.HIGHEST

D_MODEL = 1024
GRID_W = 64
D_MIX = 512
NA_HEADS = 8
NA_HEAD_DIM = D_MIX // NA_HEADS
WIN_H = 8
WIN_W = 16
HY_ORDER = 2
HY_BANDS = 8
HY_EMB = 1 + 2 * HY_BANDS
HY_FAST_DECAY = 0.3
HY_SLOW_DECAY = 1.5
HY_TARGET = 1e-2
D_FF = ((8 * D_MODEL // 3 + 127) // 128) * 128
N_MOD = 9
EPS = 1e-6
NEG_INF = -1e30
LOG2_E = math.log2(math.e)

V7X_VMEM_LIMIT_BYTES = 56 * 1024 * 1024
TOKEN_TILE = 512
FFN_TILE = 1024
V7X_MXU_WIDTH = 256
_FF_SPLIT = (D_FF // V7X_MXU_WIDTH + 1) // 2 * V7X_MXU_WIDTH
FF_CHUNKS = ((0, _FF_SPLIT), (_FF_SPLIT, D_FF))
HALO = 16
LANE_BLOCK = 128
FFT_CT = 256
FFT_NB = 16
FFT_KB = 8
FFT_K_GROUP = 4
FILT_TILE = 512
FILT_HALF = 64
FILT_PAD = 2 * FILT_HALF


def _cparams(n_axes):
    return pltpu.CompilerParams(
        dimension_semantics=("arbitrary",) * n_axes,
        vmem_limit_bytes=V7X_VMEM_LIMIT_BYTES,
    )


def _rms(x, gain):
    ms = jnp.mean(x * x, axis=-1, keepdims=True)
    return x * lax.rsqrt(ms + EPS) * gain


def _silu(x):
    return x / (1.0 + jnp.exp(-x))


SLABS = LANE_BLOCK // FFT_NB
LANES = 128


def _store_slabs(ref, tile):
    for i1 in range(tile.shape[0] // LANE_BLOCK):
        for t in range(tile.shape[1] // LANES):
            rows = tile[i1 * LANE_BLOCK:(i1 + 1) * LANE_BLOCK, t * LANES:(t + 1) * LANES]
            ref[:, t, i1 * FFT_NB:(i1 + 1) * FFT_NB, :] = rows.reshape(SLABS, FFT_NB, LANES)


def _load_slabs(ref):
    _, tiles, rows, _ = ref.shape
    return jnp.concatenate(
        [jnp.concatenate([ref[:, t, i1 * FFT_NB:(i1 + 1) * FFT_NB, :].reshape(LANE_BLOCK, LANES)
                          for t in range(tiles)], axis=1)
         for i1 in range(rows // FFT_NB)], axis=0)


def _load_strided(ref, lead, start, size):
    tiles = ref.shape[len(lead)]
    return jnp.concatenate([ref[lead + (t, pl.ds(start, size, stride=FFT_NB), slice(None))]
                            for t in range(tiles)], axis=1)


def _store_strided(ref, start, val):
    for t in range(ref.shape[0]):
        ref[t, pl.ds(start, val.shape[0], stride=FFT_NB), :] = val[:, t * LANES:(t + 1) * LANES]


def _ada_kernel(c_ref, w_ref, b_ref, o_ref):
    s = _silu(c_ref[...])
    o_ref[...] = jnp.dot(s, w_ref[...], precision=HIGHEST, preferred_element_type=F32) + b_ref[...]


def _ada(c_all, w_ada, b_ada):
    rows = c_all.shape[0]
    n_out = w_ada.shape[1]
    tn = D_MODEL
    return pl.pallas_call(
        _ada_kernel,
        out_shape=jax.ShapeDtypeStruct((rows, n_out), F32),
        grid=(n_out // tn,),
        in_specs=[
            pl.BlockSpec((rows, D_MODEL), lambda j: (0, 0)),
            pl.BlockSpec((D_MODEL, tn), lambda j: (0, j)),
            pl.BlockSpec((1, tn), lambda j: (0, j)),
        ],
        out_specs=pl.BlockSpec((rows, tn), lambda j: (0, j)),
        compiler_params=_cparams(1),
        name="ada_mod",
    )(c_all, w_ada, b_ada.reshape(1, n_out))


def _ffn_residual(x, mod_ref, mod_base, gain_ref, wg_ref, wu_ref, wd_ref):
    shift = mod_ref[mod_base:mod_base + 1, :]
    scale = mod_ref[mod_base + 1:mod_base + 2, :]
    gate = mod_ref[mod_base + 2:mod_base + 3, :]
    hb = (_rms(x, gain_ref[...]) * (1.0 + scale) + shift).astype(BF16)
    acc = None
    for c0, c1 in FF_CHUNKS:
        g = jnp.dot(hb, wg_ref[:, c0:c1], preferred_element_type=F32)
        u = jnp.dot(hb, wu_ref[:, c0:c1], preferred_element_type=F32)
        a = (_silu(g) * u).astype(BF16)
        d = jnp.dot(a, wd_ref[c0:c1, :], preferred_element_type=F32)
        acc = d if acc is None else acc + d
    return x + 0.5 * gate * acc


def _ffn1_kernel(x_ref, mod_ref, gain_ref, wg_ref, wu_ref, wd_ref, o_ref):
    o_ref[...] = _ffn_residual(x_ref[...], mod_ref, 0, gain_ref, wg_ref, wu_ref, wd_ref)


def _mix_ffn2_kernel(x_ref, an_ref, hz_ref, mod_ref, hy_gain_ref, wa_ref, wh_ref,
                     gain_ref, wg_ref, wu_ref, wd_ref, fn_ref, o_ref):
    hn = _rms(_load_slabs(hz_ref), hy_gain_ref[...]).astype(BF16)
    mixed = (jnp.dot(an_ref[...], wa_ref[...], preferred_element_type=F32)
             + jnp.dot(hn, wh_ref[...], preferred_element_type=F32))
    x = x_ref[...] + mod_ref[5:6, :] * mixed
    y = _ffn_residual(x, mod_ref, 6, gain_ref, wg_ref, wu_ref, wd_ref)
    o_ref[...] = _rms(y, fn_ref[...])


def _resident(shape):
    return pl.BlockSpec(shape, lambda i: (0, 0), pipeline_mode=pl.Buffered(1))


def _token_spec(tm, width):
    return pl.BlockSpec((tm, width), lambda i: (i, 0))


def _mod_spec(tm, seq_len):
    return pl.BlockSpec((None, N_MOD, D_MODEL), lambda i: (i * tm // seq_len, 0, 0))


def _ffn1(x2d, mod, gain, wg, wu, wd, seq_len):
    t = x2d.shape[0]
    tm = FFN_TILE
    return pl.pallas_call(
        _ffn1_kernel,
        out_shape=jax.ShapeDtypeStruct((t, D_MODEL), F32),
        grid=(t // tm,),
        in_specs=[
            _token_spec(tm, D_MODEL), _mod_spec(tm, seq_len), _resident((1, D_MODEL)),
            _resident((D_MODEL, D_FF)), _resident((D_MODEL, D_FF)), _resident((D_FF, D_MODEL)),
        ],
        out_specs=_token_spec(tm, D_MODEL),
        compiler_params=_cparams(1),
        name="ffn1",
    )(x2d, mod, gain.reshape(1, D_MODEL), wg, wu, wd)


def _mix_ffn2(x2d, attn_n, hz, mod, hy_gain, w_attn, w_hy, gain, wg, wu, wd, final_gain, seq_len):
    t = x2d.shape[0]
    tm = FFN_TILE
    tiles_per_seq = seq_len // tm
    return pl.pallas_call(
        _mix_ffn2_kernel,
        out_shape=jax.ShapeDtypeStruct((t, D_MODEL), F32),
        grid=(t // tm,),
        in_specs=[
            _token_spec(tm, D_MODEL),
            _token_spec(tm, D_MIX),
            pl.BlockSpec((None, SLABS, D_MIX // LANES, tm // LANE_BLOCK * FFT_NB, LANES),
                         lambda i: (i // tiles_per_seq, 0, 0, i % tiles_per_seq, 0)),
            _mod_spec(tm, seq_len),
            _resident((1, D_MIX)), _resident((D_MIX, D_MODEL)), _resident((D_MIX, D_MODEL)),
            _resident((1, D_MODEL)),
            _resident((D_MODEL, D_FF)), _resident((D_MODEL, D_FF)), _resident((D_FF, D_MODEL)),
            _resident((1, D_MODEL)),
        ],
        out_specs=_token_spec(tm, D_MODEL),
        compiler_params=_cparams(1),
        name="mix_ffn2",
    )(x2d, attn_n, hz, mod, hy_gain.reshape(1, D_MIX), w_attn, w_hy, gain.reshape(1, D_MODEL),
      wg, wu, wd, final_gain.reshape(1, D_MODEL))


def _inproj_kernel(x_ref, xp_ref, xn_ref, mod_ref, gain_ref, wqkv_ref, why_ref, cw_ref, cb_ref,
                   q_ref, k_ref, v_ref, hv_ref, hx1_ref, hx2_ref, ext_ref, u_ref,
                   *, tiles_per_seq, tm):
    pos = pl.program_id(0) % tiles_per_seq
    gain = gain_ref[...]
    shift = mod_ref[3:4, :]
    scale = 1.0 + mod_ref[4:5, :]

    def normed(x):
        return _rms(x, gain) * scale + shift

    hb = normed(x_ref[...]).astype(BF16)
    qkv = jnp.dot(hb, wqkv_ref[...], preferred_element_type=F32)
    q_ref[...] = (qkv[:, :D_MIX] * (NA_HEAD_DIM ** -0.5 * LOG2_E)).astype(BF16)
    k_ref[...] = qkv[:, D_MIX:2 * D_MIX].astype(BF16)
    v_ref[...] = qkv[:, 2 * D_MIX:].astype(BF16)

    has_prev = jnp.where(pos != 0, 1.0, 0.0)
    has_next = jnp.where(pos != tiles_per_seq - 1, 1.0, 0.0)
    ext_ref[0:HALO, :] = (normed(xp_ref[...]) * has_prev).astype(BF16)
    ext_ref[HALO:HALO + tm, :] = hb
    ext_ref[HALO + tm:, :] = (normed(xn_ref[...]) * has_next).astype(BF16)
    u_ref[...] = jnp.dot(ext_ref[...], why_ref[...], preferred_element_type=F32)
    conv = (cb_ref[...]
            + u_ref[pl.ds(HALO - 1, tm), :] * cw_ref[0:1, :]
            + u_ref[pl.ds(HALO, tm), :] * cw_ref[1:2, :]
            + u_ref[pl.ds(HALO + 1, tm), :] * cw_ref[2:3, :])
    for part, ref in enumerate((hv_ref, hx1_ref, hx2_ref)):
        _store_slabs(ref, conv[:, part * D_MIX:(part + 1) * D_MIX])


def _inproj(x2d, mod, gain, w_qkv, w_hy, conv_w, conv_b, seq_len):
    t = x2d.shape[0]
    tm = TOKEN_TILE
    n_halo_blocks = t // HALO
    per_tile = tm // HALO
    const = lambda shape: pl.BlockSpec(shape, lambda i: (0, 0))
    tiles_per_seq = seq_len // tm
    tile_out = jax.ShapeDtypeStruct((t, D_MIX), BF16)
    tile_spec = pl.BlockSpec((tm, D_MIX), lambda i: (i, 0))
    slab_rows = tm // LANE_BLOCK * FFT_NB
    slab_out = jax.ShapeDtypeStruct((t // seq_len, SLABS, D_MIX // LANES, seq_len // SLABS, LANES), F32)
    slab_spec = pl.BlockSpec((None, SLABS, D_MIX // LANES, slab_rows, LANES),
                             lambda i: (i // tiles_per_seq, 0, 0, i % tiles_per_seq, 0))
    return pl.pallas_call(
        functools.partial(_inproj_kernel, tiles_per_seq=tiles_per_seq, tm=tm),
        out_shape=[tile_out] * 3 + [slab_out] * 3,
        grid=(t // tm,),
        in_specs=[
            pl.BlockSpec((tm, D_MODEL), lambda i: (i, 0)),
            pl.BlockSpec((HALO, D_MODEL), lambda i: (jnp.maximum(i * per_tile - 1, 0), 0)),
            pl.BlockSpec((HALO, D_MODEL),
                         lambda i: (jnp.minimum((i + 1) * per_tile, n_halo_blocks - 1), 0)),
            pl.BlockSpec((None, N_MOD, D_MODEL), lambda i: (i * tm // seq_len, 0, 0)),
            const((1, D_MODEL)),
            const((D_MODEL, 3 * D_MIX)),
            const((D_MODEL, 3 * D_MIX)),
            const((3, 3 * D_MIX)),
            const((1, 3 * D_MIX)),
        ],
        out_specs=[tile_spec] * 3 + [slab_spec] * 3,
        scratch_shapes=[
            pltpu.VMEM((tm + 2 * HALO, D_MODEL), BF16),
            pltpu.VMEM((tm + 2 * HALO, 3 * D_MIX), F32),
        ],
        compiler_params=_cparams(1),
        name="inproj",
    )(x2d, x2d, x2d, mod, gain.reshape(1, D_MODEL), w_qkv, w_hy, conv_w,
      conv_b.reshape(1, 3 * D_MIX))


ROW_TOKENS = GRID_W
GROUP_ROWS = WIN_H
GROUP_TOKENS = GROUP_ROWS * ROW_TOKENS
WIN_TOKENS = WIN_H * ROW_TOKENS


def _attn_bias_table(rpb):
    n_heads, n_drow, n_dcol = rpb.shape
    period = 2 * GRID_W - 1
    wrapped = jnp.concatenate([rpb[..., WIN_W - 1:], jnp.zeros((n_heads, n_drow, period - n_dcol), F32),
                               rpb[..., :WIN_W - 1]], axis=-1).astype(F32)
    toeplitz = jnp.tile(wrapped, GRID_W)[..., :GRID_W * (period - 1)]
    toeplitz = toeplitz.reshape(n_heads, n_drow, GRID_W, period - 1)[..., :GRID_W]
    qc = np.arange(GRID_W)[:, None]
    kc = np.arange(GRID_W)[None, :]
    win_start = np.clip(qc - WIN_W // 2, 0, GRID_W - WIN_W)
    col_ok = (kc >= win_start) & (kc < win_start + WIN_W)
    masked = jnp.where(col_ok[None, None], toeplitz * LOG2_E, NEG_INF)
    bias = jnp.stack([masked[:, o:o + WIN_H] for o in range(WIN_H)], axis=0)
    bias = jnp.transpose(bias, (0, 1, 3, 2, 4))
    return bias.reshape(WIN_H, NA_HEADS, GRID_W, WIN_TOKENS)


NATTEN_UNROLL = 4
KV_WINDOW_ROWS = 3 * GROUP_ROWS


def _kv_window_start(g, rows):
    return jnp.clip(g * GROUP_ROWS - GROUP_ROWS, 0, rows - KV_WINDOW_ROWS)


def _attn_kernel(q_ref, k_ref, v_ref, bias_ref, gain_ref, o_ref, acc, *, rows):
    g = pl.program_id(1)
    win_start = _kv_window_start(g, rows)
    first_head = lax.broadcasted_iota(jnp.int32, (ROW_TOKENS, 2 * NA_HEAD_DIM), 1) < NA_HEAD_DIM
    head_pairs = [slice(hp * 2 * NA_HEAD_DIM, (hp + 1) * 2 * NA_HEAD_DIM) for hp in range(NA_HEADS // 2)]

    def row_body(rr, carry):
        r = g * GROUP_ROWS + rr
        start = jnp.clip(r - WIN_H // 2, 0, rows - WIN_H)
        koff = pl.multiple_of((start - win_start) * ROW_TOKENS, ROW_TOKENS)
        row_class = start - r + WIN_H - 1
        qoff = pl.multiple_of(rr * ROW_TOKENS, ROW_TOKENS)
        scores = []
        for hp, lanes in enumerate(head_pairs):
            q2 = q_ref[pl.ds(qoff, ROW_TOKENS), lanes]
            kw = k_ref[pl.ds(koff, WIN_TOKENS), lanes]
            for hh in range(2):
                keep = first_head if hh == 0 else jnp.logical_not(first_head)
                qm = jnp.where(keep, q2, jnp.zeros_like(q2))
                s = lax.dot_general(qm, kw, (((1,), (1,)), ((), ())), preferred_element_type=F32)
                scores.append(s + bias_ref[row_class, 2 * hp + hh])
        probs, denoms = [], []
        for s in scores:
            p = jnp.exp2(s - jnp.max(s, axis=-1, keepdims=True))
            denoms.append(jnp.sum(p, axis=-1, keepdims=True))
            probs.append(p.astype(BF16))
        for hp, lanes in enumerate(head_pairs):
            vw = v_ref[pl.ds(koff, WIN_TOKENS), lanes]
            outs = [jnp.dot(probs[2 * hp + hh], vw, preferred_element_type=F32) / denoms[2 * hp + hh]
                    for hh in range(2)]
            acc[pl.ds(qoff, ROW_TOKENS), lanes] = jnp.where(first_head, outs[0], outs[1])
        return carry

    lax.fori_loop(0, GROUP_ROWS, row_body, 0, unroll=NATTEN_UNROLL)
    o_ref[...] = _rms(acc[...], gain_ref[...]).astype(BF16)


def _attention(q, k, v, bias, gain, batch, seq_len):
    rows = seq_len // GRID_W
    q3 = q.reshape(batch, seq_len, D_MIX)
    k3 = k.reshape(batch, seq_len, D_MIX)
    v3 = v.reshape(batch, seq_len, D_MIX)
    cur = pl.BlockSpec((None, GROUP_TOKENS, D_MIX), lambda b, g: (b, g, 0))
    window = pl.BlockSpec((None, pl.Element(KV_WINDOW_ROWS * ROW_TOKENS), pl.Element(D_MIX)),
                          lambda b, g: (b, _kv_window_start(g, rows) * ROW_TOKENS, 0))
    out = pl.pallas_call(
        functools.partial(_attn_kernel, rows=rows),
        out_shape=jax.ShapeDtypeStruct((batch, seq_len, D_MIX), BF16),
        grid=(batch, rows // GROUP_ROWS),
        in_specs=[
            cur, window, window,
            pl.BlockSpec((WIN_H, NA_HEADS, GRID_W, WIN_TOKENS), lambda b, g: (0, 0, 0, 0),
                         pipeline_mode=pl.Buffered(1)),
            pl.BlockSpec((1, D_MIX), lambda b, g: (0, 0)),
        ],
        out_specs=cur,
        scratch_shapes=[pltpu.VMEM((GROUP_TOKENS, D_MIX), F32)],
        compiler_params=_cparams(2),
        name="natten",
    )(q3, k3, v3, bias, gain.reshape(1, D_MIX))
    return out.reshape(batch * seq_len, D_MIX)


def _split_hi_lo(x):
    hi = x.astype(ml_dtypes.bfloat16)
    lo = (x - hi.astype(np.float64)).astype(ml_dtypes.bfloat16)
    return hi, lo


def _stack_hi_lo(m):
    hi, lo = _split_hi_lo(m)
    return np.concatenate([hi, lo], axis=-2)


def _embed(re, im):
    return np.concatenate([np.concatenate([re, -im], axis=-1),
                           np.concatenate([im, re], axis=-1)], axis=-2)


@functools.lru_cache(maxsize=None)
def _fft_tables(seq_len):
    n = 2 * seq_len
    n2 = LANE_BLOCK
    n1 = n // n2
    i2 = np.arange(n2)[:, None, None]
    k1 = np.arange(n1)[None, :, None]
    i1 = np.arange(n1)[None, None, :]
    ang = -2.0 * np.pi * ((k1 * (n2 * i1 + i2)) % n) / n
    gr_full, gi_full = np.cos(ang), np.sin(ang)
    g_real = _stack_hi_lo(np.concatenate([gr_full, gi_full], axis=1))
    gr, gi = gr_full[..., :n1 // 2], gi_full[..., :n1 // 2]
    g_fwd = _stack_hi_lo(_embed(gr, gi))
    g_inv = _stack_hi_lo(_embed(np.swapaxes(gr, 1, 2) / n, -np.swapaxes(gi, 1, 2) / n))
    jk = np.outer(np.arange(n2), np.arange(n2))
    ang2 = -2.0 * np.pi * (jk % n2) / n2
    fr, fi = np.cos(ang2), np.sin(ang2)
    f_fwd = _stack_hi_lo(_embed(fr, fi))
    f_inv = _stack_hi_lo(_embed(fr, -fi))
    return n1, g_fwd, g_real, g_inv, f_fwd, f_inv


def _dft3(m_hl, x, m):
    x_hi = x.astype(BF16)
    x_lo = (x - x_hi.astype(F32)).astype(BF16)
    t = jnp.dot(m_hl, x_hi, preferred_element_type=F32)
    return t[:m] + t[m:] + jnp.dot(m_hl[:m], x_lo, preferred_element_type=F32)


def _stage_a_forward(x_ref, g_ref, a_ref, *, n1):
    for i in range(FFT_NB):
        x = jnp.concatenate([_load_strided(x_ref, (0,), i, n1 // 2),
                             _load_strided(x_ref, (1,), i, n1 // 2)], axis=0)
        _store_strided(a_ref, i, _dft3(g_ref[i], x, 2 * n1))


def _stage_a_inverse(d_ref, gi_ref, y_ref, *, n1):
    for i in range(FFT_NB):
        _store_strided(y_ref, i, _dft3(gi_ref[i], _load_strided(d_ref, (), i, 2 * n1), n1))


def _k1_kernel(x_ref, g_ref, a_ref, *, n1):
    _stage_a_forward(x_ref, g_ref, a_ref, n1=n1)


FFT_TILES = FFT_CT // LANES


def _seq_spec(n1):
    return pl.BlockSpec((None, 2, None, FFT_TILES, n1 // 2 * FFT_NB, LANES),
                        lambda c, j, p: (p, 0, j, c, 0, 0))


def _spec_spec(n1):
    return pl.BlockSpec((None, None, FFT_TILES, 2 * n1 * FFT_NB, LANES), lambda c, j, p: (p, j, c, 0, 0))


def _fft_stage_a(x6, g_fwd, n1):
    pairs, _, slabs, tiles, _, _ = x6.shape
    return pl.pallas_call(
        functools.partial(_k1_kernel, n1=n1),
        out_shape=jax.ShapeDtypeStruct((pairs, slabs, tiles, 2 * n1 * FFT_NB, LANES), F32),
        grid=(tiles // FFT_TILES, slabs, pairs),
        in_specs=[
            _seq_spec(n1),
            pl.BlockSpec((FFT_NB, 4 * n1, n1), lambda c, j, p: (j, 0, 0)),
        ],
        out_specs=_spec_spec(n1),
        compiler_params=_cparams(3),
        name="hy_stage_a",
    )(x6, g_fwd)


def _load_low_index(ref, part, kk):
    return jnp.concatenate([ref[:, t, part, kk].reshape(LANE_BLOCK, LANES) for t in range(ref.shape[1])],
                           axis=1)


def _store_low_index(ref, part, kk, val):
    for t in range(ref.shape[1]):
        ref[:, t, part, kk] = val[:, t * LANES:(t + 1) * LANES].reshape(SLABS, FFT_NB, LANES)


def _k2_kernel(a_ref, kf_ref, f_ref, fi_ref, d_ref, *, kb):
    n2 = LANE_BLOCK
    f_hl = f_ref[...]
    fi_hl = fi_ref[...]

    def body(group, carry):
        ks = [group * FFT_K_GROUP + u for u in range(FFT_K_GROUP)]
        spectra = [_dft3(f_hl, jnp.concatenate([_load_low_index(a_ref, 0, kk),
                                                _load_low_index(a_ref, 1, kk)], axis=0), 2 * n2)
                   for kk in ks]
        products = []
        for kk, c in zip(ks, spectra):
            cr, ci = c[:n2], c[n2:]
            kr, ki = kf_ref[0, kk], kf_ref[1, kk]
            products.append(jnp.concatenate([cr * kr - ci * ki, cr * ki + ci * kr], axis=0))
        for kk, y in zip(ks, products):
            d = _dft3(fi_hl, y, 2 * n2)
            _store_low_index(d_ref, 0, kk, d[:n2])
            _store_low_index(d_ref, 1, kk, d[n2:])
        return carry

    lax.fori_loop(0, kb // FFT_K_GROUP, body, 0)


def _fft_stage_c(a4, kf, f_fwd, f_inv, order, n1):
    pairs, slabs, tiles, _, _ = a4.shape
    n2 = LANE_BLOCK
    kb = FFT_KB
    ch_blocks = tiles // FFT_TILES
    a7 = a4.reshape(pairs, slabs, tiles, 2, n1, FFT_NB, LANES)
    spec = pl.BlockSpec((None, slabs, FFT_TILES, 2, kb, FFT_NB, LANES),
                        lambda c, k, p: (p, 0, c, 0, k, 0, 0))
    d7 = pl.pallas_call(
        functools.partial(_k2_kernel, kb=kb),
        out_shape=jax.ShapeDtypeStruct(a7.shape, F32),
        grid=(ch_blocks, n1 // kb, pairs),
        in_specs=[
            spec,
            pl.BlockSpec((2, kb, n2, FFT_CT), lambda c, k, p: (0, k, 0, order * ch_blocks + c)),
            pl.BlockSpec((4 * n2, 2 * n2), lambda c, k, p: (0, 0)),
            pl.BlockSpec((4 * n2, 2 * n2), lambda c, k, p: (0, 0)),
        ],
        out_specs=spec,
        compiler_params=_cparams(3),
        name="hy_stage_c",
    )(a7, kf, f_fwd, f_inv)
    return d7.reshape(a4.shape)


def _k3_kernel(d_ref, gi_ref, z_ref, x_ref, skip_ref, *rest, n1, forward):
    if forward:
        g_ref, o_ref, a_ref, y_ref = rest
    else:
        o_ref, y_ref = rest
    _stage_a_inverse(d_ref, gi_ref, y_ref, n1=n1)
    rows = n1 // 2 * FFT_NB
    for part in range(2):
        for t in range(FFT_TILES):
            conv = y_ref[t, part * rows:(part + 1) * rows, :]
            o_ref[part, t] = x_ref[part, t] * (conv + skip_ref[t] * z_ref[part, t])
    if forward:
        _stage_a_forward(o_ref, g_ref, a_ref, n1=n1)


def _fft_stage_a_inverse(d4, g_inv, z6, x6, skip_row, n1, g_fwd=None):
    pairs, slabs, tiles, _, _ = d4.shape
    forward = g_fwd is not None
    in_specs = [
        _spec_spec(n1),
        pl.BlockSpec((FFT_NB, 2 * n1, 2 * n1), lambda c, j, p: (j, 0, 0)),
        _seq_spec(n1),
        _seq_spec(n1),
        pl.BlockSpec((FFT_TILES, 1, LANES), lambda c, j, p: (c, 0, 0)),
    ]
    args = [d4, g_inv, z6, x6, skip_row]
    out_shape = [jax.ShapeDtypeStruct(z6.shape, F32)]
    out_specs = [_seq_spec(n1)]
    if forward:
        in_specs.append(pl.BlockSpec((FFT_NB, 4 * n1, n1), lambda c, j, p: (j, 0, 0)))
        args.append(g_fwd)
        out_shape.append(jax.ShapeDtypeStruct(d4.shape, F32))
        out_specs.append(_spec_spec(n1))
    return pl.pallas_call(
        functools.partial(_k3_kernel, n1=n1, forward=forward),
        out_shape=out_shape,
        grid=(tiles // FFT_TILES, slabs, pairs),
        in_specs=in_specs,
        out_specs=out_specs,
        scratch_shapes=[pltpu.VMEM((FFT_TILES, n1 * FFT_NB, LANES), F32)],
        compiler_params=_cparams(3),
        name="hy_stage_a_inv_fwd" if forward else "hy_stage_a_inv",
    )(*args)


def _filt_kernel(z_ref, w1_ref, b1_ref, w2_ref, b2_ref, w3_ref, b3_ref, wo_ref, freq_ref,
                 delta_ref, h_ref, l1_ref, *, tl):
    i = pl.program_id(0)
    freq = freq_ref[...]

    def dot(a, b):
        return jnp.dot(a, b, precision=HIGHEST, preferred_element_type=F32)

    z = z_ref[...]
    h = jnp.sin(freq * (dot(z, w1_ref[...]) + b1_ref[...]))
    h = jnp.sin(freq * (dot(h, w2_ref[...]) + b2_ref[...]))
    h = jnp.sin(freq * (dot(h, w3_ref[...]) + b3_ref[...]))
    hf = dot(h, wo_ref[0]) * jnp.exp(-z[:, 0:1] * delta_ref[...])
    hb = dot(h, wo_ref[1]) * jnp.exp(-z[:, FILT_HALF:FILT_HALF + 1] * delta_ref[...])
    row = i * tl + lax.broadcasted_iota(jnp.int32, (tl, 1), 0)
    hb = jnp.where(row == 0, 0.0, hb)
    _store_slabs(h_ref.at[0], hf)
    _store_slabs(h_ref.at[1], hb)

    @pl.when(i == 0)
    def _():
        l1_ref[...] = jnp.zeros_like(l1_ref)

    l1_ref[...] += (jnp.sum(jnp.abs(hf), axis=0, keepdims=True)
                    + jnp.sum(jnp.abs(hb), axis=0, keepdims=True))


def _pad_to(x, shape):
    return jnp.pad(x, [(0, s - d) for d, s in zip(x.shape, shape)])


def _filter_taps(seq_len, w1, b1, w2, b2, w3, b3, wo, freq):
    t = jnp.linspace(0.0, 1.0, seq_len, dtype=F32)[:, None]
    w = 2.0 * math.pi * jnp.arange(seq_len, dtype=F32)[:, None] / seq_len
    f = jnp.linspace(1e-4, HY_BANDS - 1, HY_BANDS, dtype=F32)[None, :]
    z = _pad_to(jnp.concatenate([t, jnp.cos(f * w), -jnp.sin(f * w)], axis=-1), (seq_len, FILT_HALF))
    z = jnp.concatenate([z, z[::-1]], axis=1)
    deltas = jnp.abs(jnp.linspace(math.log(HY_TARGET) / HY_FAST_DECAY,
                                  math.log(HY_TARGET) / HY_SLOW_DECAY, D_MIX, dtype=F32))
    n_cols = HY_ORDER * D_MIX

    def both(m):
        m = _pad_to(m, (FILT_HALF, FILT_HALF))
        zero = jnp.zeros_like(m)
        return jnp.concatenate([jnp.concatenate([m, zero], axis=1),
                                jnp.concatenate([zero, m], axis=1)], axis=0)

    row = lambda v: jnp.tile(_pad_to(v.reshape(1, -1), (1, FILT_HALF)), (1, 2))
    wo_p = _pad_to(wo, (FILT_HALF, 2 * n_cols))
    zero = jnp.zeros((FILT_HALF, n_cols), F32)
    wo_dirs = jnp.stack([jnp.concatenate([wo_p[:, :n_cols], zero], axis=0),
                         jnp.concatenate([zero, wo_p[:, n_cols:]], axis=0)], axis=0)
    pad2 = (FILT_PAD, FILT_PAD)
    tl = FILT_TILE
    const = lambda shape: pl.BlockSpec(shape, lambda i: (0,) * len(shape))
    return pl.pallas_call(
        functools.partial(_filt_kernel, tl=tl),
        out_shape=[jax.ShapeDtypeStruct((2, SLABS, n_cols // LANES, seq_len // SLABS, LANES), F32),
                   jax.ShapeDtypeStruct((1, n_cols), F32)],
        grid=(seq_len // tl,),
        in_specs=[
            pl.BlockSpec((tl, FILT_PAD), lambda i: (i, 0)),
            const(pad2), const((1, FILT_PAD)), const(pad2), const((1, FILT_PAD)),
            const(pad2), const((1, FILT_PAD)), const((2, FILT_PAD, n_cols)), const((1, FILT_PAD)),
            const((1, n_cols)),
        ],
        out_specs=[pl.BlockSpec((2, SLABS, n_cols // LANES, tl // LANE_BLOCK * FFT_NB, LANES),
                                lambda i: (0, 0, 0, i, 0)),
                   const((1, n_cols))],
        compiler_params=_cparams(1),
        name="hy_filter_taps",
    )(z, both(w1), row(b1), both(w2), row(b2), both(w3), row(b3), wo_dirs, row(freq),
      jnp.tile(deltas, HY_ORDER).reshape(1, n_cols))


def _k2f_kernel(a_ref, l1_ref, f_ref, kf_ref, *, kb):
    n2 = LANE_BLOCK
    f_hl = f_ref[...]
    inv_l1 = 1.0 / l1_ref[...]

    def body(group, carry):
        ks = [group * FFT_K_GROUP + u for u in range(FFT_K_GROUP)]
        spectra = [_dft3(f_hl, jnp.concatenate([_load_low_index(a_ref, 0, kk),
                                                _load_low_index(a_ref, 1, kk)], axis=0), 2 * n2)
                   for kk in ks]
        for kk, c in zip(ks, spectra):
            kf_ref[0, kk] = c[:n2] * inv_l1
            kf_ref[1, kk] = c[n2:] * inv_l1
        return carry

    lax.fori_loop(0, kb // FFT_K_GROUP, body, 0)


def _filter_spectrum(a4, l1, f_fwd, n1):
    _, slabs, tiles, _, _ = a4.shape
    cols = tiles * LANES
    n2 = LANE_BLOCK
    kb, ct = FFT_KB, FFT_CT
    a6 = a4.reshape(slabs, tiles, 2, n1, FFT_NB, LANES)
    return pl.pallas_call(
        functools.partial(_k2f_kernel, kb=kb),
        out_shape=jax.ShapeDtypeStruct((2, n1, n2, cols), F32),
        grid=(cols // ct, n1 // kb),
        in_specs=[
            pl.BlockSpec((slabs, FFT_TILES, 2, kb, FFT_NB, LANES), lambda c, k: (0, c, 0, k, 0, 0)),
            pl.BlockSpec((1, ct), lambda c, k: (0, c)),
            pl.BlockSpec((4 * n2, 2 * n2), lambda c, k: (0, 0)),
        ],
        out_specs=pl.BlockSpec((2, kb, n2, ct), lambda c, k: (0, k, 0, c)),
        compiler_params=_cparams(2),
        name="hy_filter_spectrum",
    )(a6, l1, f_fwd)


def _hyena(hv, hx1, hx2, skip, filt_params, seq_len):
    n1, *tables = _fft_tables(seq_len)
    g_fwd, g_real, g_inv, f_fwd, f_inv = (jnp.asarray(m) for m in tables)
    taps, l1 = _filter_taps(seq_len, *filt_params)
    kf = _filter_spectrum(_fft_stage_a(taps[None], g_real, n1), l1, f_fwd, n1)
    as_pairs = lambda a: a.reshape((a.shape[0] // 2, 2) + a.shape[1:])
    z0, x1, x2 = as_pairs(hv), as_pairs(hx1), as_pairs(hx2)
    d = _fft_stage_c(_fft_stage_a(z0, g_fwd, n1), kf, f_fwd, f_inv, 0, n1)
    skip_rows = skip.reshape(HY_ORDER, D_MIX // LANES, 1, LANES)
    z1, a = _fft_stage_a_inverse(d, g_inv, z0, x1, skip_rows[0], n1, g_fwd=g_fwd)
    d = _fft_stage_c(a, kf, f_fwd, f_inv, 1, n1)
    (z2,) = _fft_stage_a_inverse(d, g_inv, z1, x2, skip_rows[1], n1)
    return z2.reshape(hv.shape)


def _trunk(x, mod, p, final_norm):
    batch, seq_len, _ = x.shape
    x2d = x.reshape(batch * seq_len, D_MODEL)
    x2d = _ffn1(x2d, mod, p["ffn1_norm"], p["ffn1_w_gate"], p["ffn1_w_up"], p["ffn1_w_down"], seq_len)
    q, k, v, hv, hx1, hx2 = _inproj(x2d, mod, p["mix_norm"], p["w_qkv"], p["w_hy"],
                                    p["hy_conv_w"], p["hy_conv_b"], seq_len)
    attn_n = _attention(q, k, v, p["attn_bias"], p["attn_out_norm"], batch, seq_len)
    hz = _hyena(hv, hx1, hx2, p["hy_skip"], p["hy_filter"], seq_len)
    y = _mix_ffn2(x2d, attn_n, hz, mod, p["hy_out_norm"], p["w_out_attn"], p["w_out_hy"],
                  p["ffn2_norm"], p["ffn2_w_gate"], p["ffn2_w_up"], p["ffn2_w_down"], final_norm, seq_len)
    return y.reshape(batch, seq_len, D_MODEL)


def kernel(x_prompt, x_sample, c_prompt, c_sample, w_ada, b_ada, ffn1_norm, ffn1_w_gate, ffn1_w_up,
           ffn1_w_down, mix_norm, w_in, na_rpb, hy_conv_w, hy_conv_b, hy_w1, hy_b1, hy_w2, hy_b2,
           hy_w3, hy_b3, hy_wo, hy_sin_freq, hy_skip, attn_out_norm, hy_out_norm, w_out, ffn2_norm,
           ffn2_w_gate, ffn2_w_up, ffn2_w_down, final_norm):
    assert w_ada.shape[0] == 1, "single-layer encoder"
    n_prompt = c_prompt.shape[0]
    mod_all = _ada(jnp.concatenate([c_prompt, c_sample], axis=0), w_ada[0], b_ada[0])
    mod_all = mod_all.reshape(-1, N_MOD, D_MODEL)
    bf = lambda w: w[0].astype(BF16)
    p = {
        "ffn1_norm": ffn1_norm[0], "ffn1_w_gate": bf(ffn1_w_gate), "ffn1_w_up": bf(ffn1_w_up),
        "ffn1_w_down": bf(ffn1_w_down),
        "mix_norm": mix_norm[0],
        "w_qkv": w_in[0, :, :3 * D_MIX].astype(BF16), "w_hy": w_in[0, :, 3 * D_MIX:].astype(BF16),
        "attn_bias": _attn_bias_table(na_rpb[0]),
        "hy_conv_w": hy_conv_w[0], "hy_conv_b": hy_conv_b[0],
        "hy_filter": (hy_w1[0], hy_b1[0], hy_w2[0], hy_b2[0], hy_w3[0], hy_b3[0], hy_wo[0],
                      hy_sin_freq[0]),
        "hy_skip": hy_skip[0],
        "attn_out_norm": attn_out_norm[0], "hy_out_norm": hy_out_norm[0],
        "w_out_attn": w_out[0, :D_MIX].astype(BF16), "w_out_hy": w_out[0, D_MIX:].astype(BF16),
        "ffn2_norm": ffn2_norm[0], "ffn2_w_gate": bf(ffn2_w_gate), "ffn2_w_up": bf(ffn2_w_up),
        "ffn2_w_down": bf(ffn2_w_down),
    }
    y_prompt = _trunk(x_prompt, mod_all[:n_prompt], p, final_norm)
    y_sample = _trunk(x_sample, mod_all[n_prompt:], p, final_norm)
    return (y_prompt, y_sample)
```

```python
import functools
import math

import ml_dtypes
import numpy as np
import jax
import jax.numpy as jnp
from jax import lax
from jax.experimental import pallas as pl
from jax.experimental.pallas import tpu as pltpu

F32 = jnp.float32
BF16 = jnp.bfloat16
HIGHEST = lax.Precision.HIGHEST

D_MODEL = 1024
GRID_W = 64
D_MIX = 512
NA_HEADS = 8
NA_HEAD_DIM = D_MIX // NA_HEADS
WIN_H = 8
WIN_W = 16
HY_ORDER = 2
HY_BANDS = 8
HY_EMB = 1 + 2 * HY_BANDS
HY_FAST_DECAY = 0.3
HY_SLOW_DECAY = 1.5
HY_TARGET = 1e-2
D_FF = ((8 * D_MODEL // 3 + 127) // 128) * 128
N_MOD = 9
EPS = 1e-6
NEG_INF = -1e30
LOG2_E = math.log2(math.e)

V7X_VMEM_LIMIT_BYTES = 56 * 1024 * 1024
TOKEN_TILE = 512
FFN_TILE = 1024
V7X_MXU_WIDTH = 256
_FF_SPLIT = (D_FF // V7X_MXU_WIDTH + 1) // 2 * V7X_MXU_WIDTH
FF_CHUNKS = ((0, _FF_SPLIT), (_FF_SPLIT, D_FF))
HALO = 16
LANE_BLOCK = 128
FFT_CT = 256
FFT_NB = 16
FFT_KB = 16
FFT_K_GROUP = 4
FILT_TILE = 512
FILT_HALF = 64
FILT_PAD = 2 * FILT_HALF


def _cparams(n_axes):
    return pltpu.CompilerParams(
        dimension_semantics=("arbitrary",) * n_axes,
        vmem_limit_bytes=V7X_VMEM_LIMIT_BYTES,
    )


def _rms(x, gain):
    ms = jnp.mean(x * x, axis=-1, keepdims=True)
    return x * lax.rsqrt(ms + EPS) * gain


def _silu(x):
    return x / (1.0 + jnp.exp(-x))


SLABS = LANE_BLOCK // FFT_NB
LANES = 128


def _store_slabs(ref, tile):
    for i1 in range(tile.shape[0] // LANE_BLOCK):
        for t in range(tile.shape[1] // LANES):
            rows = tile[i1 * LANE_BLOCK:(i1 + 1) * LANE_BLOCK, t * LANES:(t + 1) * LANES]
            ref[:, t, i1 * FFT_NB:(i1 + 1) * FFT_NB, :] = rows.reshape(SLABS, FFT_NB, LANES)


def _load_slabs(ref):
    _, tiles, rows, _ = ref.shape
    return jnp.concatenate(
        [jnp.concatenate([ref[:, t, i1 * FFT_NB:(i1 + 1) * FFT_NB, :].reshape(LANE_BLOCK, LANES)
                          for t in range(tiles)], axis=1)
         for i1 in range(rows // FFT_NB)], axis=0)


def _load_strided(ref, lead, start, size):
    tiles = ref.shape[len(lead)]
    return jnp.concatenate([ref[lead + (t, pl.ds(start, size, stride=FFT_NB), slice(None))]
                            for t in range(tiles)], axis=1)


def _store_strided(ref, start, val):
    for t in range(ref.shape[0]):
        ref[t, pl.ds(start, val.shape[0], stride=FFT_NB), :] = val[:, t * LANES:(t + 1) * LANES]


def _ada_kernel(c_ref, w_ref, b_ref, o_ref):
    s = _silu(c_ref[...])
    o_ref[...] = jnp.dot(s, w_ref[...], precision=HIGHEST, preferred_element_type=F32) + b_ref[...]


def _ada(c_all, w_ada, b_ada):
    rows = c_all.shape[0]
    n_out = w_ada.shape[1]
    tn = D_MODEL
    return pl.pallas_call(
        _ada_kernel,
        out_shape=jax.ShapeDtypeStruct((rows, n_out), F32),
        grid=(n_out // tn,),
        in_specs=[
            pl.BlockSpec((rows, D_MODEL), lambda j: (0, 0)),
            pl.BlockSpec((D_MODEL, tn), lambda j: (0, j)),
            pl.BlockSpec((1, tn), lambda j: (0, j)),
        ],
        out_specs=pl.BlockSpec((rows, tn), lambda j: (0, j)),
        compiler_params=_cparams(1),
        name="ada_mod",
    )(c_all, w_ada, b_ada.reshape(1, n_out))


def _ffn_residual(x, mod_ref, mod_base, gain_ref, wg_ref, wu_ref, wd_ref):
    shift = mod_ref[mod_base:mod_base + 1, :]
    scale = mod_ref[mod_base + 1:mod_base + 2, :]
    gate = mod_ref[mod_base + 2:mod_base + 3, :]
    hb = (_rms(x, gain_ref[...]) * (1.0 + scale) + shift).astype(BF16)
    acc = None
    for c0, c1 in FF_CHUNKS:
        g = jnp.dot(hb, wg_ref[:, c0:c1], preferred_element_type=F32)
        u = jnp.dot(hb, wu_ref[:, c0:c1], preferred_element_type=F32)
        a = (_silu(g) * u).astype(BF16)
        d = jnp.dot(a, wd_ref[c0:c1, :], preferred_element_type=F32)
        acc = d if acc is None else acc + d
    return x + 0.5 * gate * acc


def _ffn1_kernel(x_ref, mod_ref, gain_ref, wg_ref, wu_ref, wd_ref, o_ref):
    o_ref[...] = _ffn_residual(x_ref[...], mod_ref, 0, gain_ref, wg_ref, wu_ref, wd_ref)


def _mix_ffn2_kernel(x_ref, an_ref, hz_ref, mod_ref, hy_gain_ref, wa_ref, wh_ref,
                     gain_ref, wg_ref, wu_ref, wd_ref, fn_ref, o_ref):
    hn = _rms(_load_slabs(hz_ref), hy_gain_ref[...]).astype(BF16)
    mixed = (jnp.dot(an_ref[...], wa_ref[...], preferred_element_type=F32)
             + jnp.dot(hn, wh_ref[...], preferred_element_type=F32))
    x = x_ref[...] + mod_ref[5:6, :] * mixed
    y = _ffn_residual(x, mod_ref, 6, gain_ref, wg_ref, wu_ref, wd_ref)
    o_ref[...] = _rms(y, fn_ref[...])


def _resident(shape):
    return pl.BlockSpec(shape, lambda i: (0, 0), pipeline_mode=pl.Buffered(1))


def _token_spec(tm, width):
    return pl.BlockSpec((tm, width), lambda i: (i, 0))


def _mod_spec(tm, seq_len):
    return pl.BlockSpec((None, N_MOD, D_MODEL), lambda i: (i * tm // seq_len, 0, 0))


def _ffn1(x2d, mod, gain, wg, wu, wd, seq_len):
    t = x2d.shape[0]
    tm = FFN_TILE
    return pl.pallas_call(
        _ffn1_kernel,
        out_shape=jax.ShapeDtypeStruct((t, D_MODEL), F32),
        grid=(t // tm,),
        in_specs=[
            _token_spec(tm, D_MODEL), _mod_spec(tm, seq_len), _resident((1, D_MODEL)),
            _resident((D_MODEL, D_FF)), _resident((D_MODEL, D_FF)), _resident((D_FF, D_MODEL)),
        ],
        out_specs=_token_spec(tm, D_MODEL),
        compiler_params=_cparams(1),
        name="ffn1",
    )(x2d, mod, gain.reshape(1, D_MODEL), wg, wu, wd)


def _mix_ffn2(x2d, attn_n, hz, mod, hy_gain, w_attn, w_hy, gain, wg, wu, wd, final_gain, seq_len):
    t = x2d.shape[0]
    tm = FFN_TILE
    tiles_per_seq = seq_len // tm
    return pl.pallas_call(
        _mix_ffn2_kernel,
        out_shape=jax.ShapeDtypeStruct((t, D_MODEL), F32),
        grid=(t // tm,),
        in_specs=[
            _token_spec(tm, D_MODEL),
            _token_spec(tm, D_MIX),
            pl.BlockSpec((None, SLABS, D_MIX // LANES, tm // LANE_BLOCK * FFT_NB, LANES),
                         lambda i: (i // tiles_per_seq, 0, 0, i % tiles_per_seq, 0)),
            _mod_spec(tm, seq_len),
            _resident((1, D_MIX)), _resident((D_MIX, D_MODEL)), _resident((D_MIX, D_MODEL)),
            _resident((1, D_MODEL)),
            _resident((D_MODEL, D_FF)), _resident((D_MODEL, D_FF)), _resident((D_FF, D_MODEL)),
            _resident((1, D_MODEL)),
        ],
        out_specs=_token_spec(tm, D_MODEL),
        compiler_params=_cparams(1),
        name="mix_ffn2",
    )(x2d, attn_n, hz, mod, hy_gain.reshape(1, D_MIX), w_attn, w_hy, gain.reshape(1, D_MODEL),
      wg, wu, wd, final_gain.reshape(1, D_MODEL))


def _inproj_kernel(x_ref, xp_ref, xn_ref, mod_ref, gain_ref, wqkv_ref, why_ref, cw_ref, cb_ref,
                   q_ref, k_ref, v_ref, hv_ref, hx1_ref, hx2_ref, ext_ref, u_ref,
                   *, tiles_per_seq, tm):
    pos = pl.program_id(0) % tiles_per_seq
    gain = gain_ref[...]
    shift = mod_ref[3:4, :]
    scale = 1.0 + mod_ref[4:5, :]

    def normed(x):
        return _rms(x, gain) * scale + shift

    hb = normed(x_ref[...]).astype(BF16)
    qkv = jnp.dot(hb, wqkv_ref[...], preferred_element_type=F32)
    q_ref[...] = (qkv[:, :D_MIX] * (NA_HEAD_DIM ** -0.5 * LOG2_E)).astype(BF16)
    k_ref[...] = qkv[:, D_MIX:2 * D_MIX].astype(BF16)
    v_ref[...] = qkv[:, 2 * D_MIX:].astype(BF16)

    has_prev = jnp.where(pos != 0, 1.0, 0.0)
    has_next = jnp.where(pos != tiles_per_seq - 1, 1.0, 0.0)
    ext_ref[0:HALO, :] = (normed(xp_ref[...]) * has_prev).astype(BF16)
    ext_ref[HALO:HALO + tm, :] = hb
    ext_ref[HALO + tm:, :] = (normed(xn_ref[...]) * has_next).astype(BF16)
    u_ref[...] = jnp.dot(ext_ref[...], why_ref[...], preferred_element_type=F32)
    conv = (cb_ref[...]
            + u_ref[pl.ds(HALO - 1, tm), :] * cw_ref[0:1, :]
            + u_ref[pl.ds(HALO, tm), :] * cw_ref[1:2, :]
            + u_ref[pl.ds(HALO + 1, tm), :] * cw_ref[2:3, :])
    for part, ref in enumerate((hv_ref, hx1_ref, hx2_ref)):
        _store_slabs(ref, conv[:, part * D_MIX:(part + 1) * D_MIX])


def _inproj(x2d, mod, gain, w_qkv, w_hy, conv_w, conv_b, seq_len):
    t = x2d.shape[0]
    tm = TOKEN_TILE
    n_halo_blocks = t // HALO
    per_tile = tm // HALO
    const = lambda shape: pl.BlockSpec(shape, lambda i: (0, 0))
    tiles_per_seq = seq_len // tm
    tile_out = jax.ShapeDtypeStruct((t, D_MIX), BF16)
    tile_spec = pl.BlockSpec((tm, D_MIX), lambda i: (i, 0))
    slab_rows = tm // LANE_BLOCK * FFT_NB
    slab_out = jax.ShapeDtypeStruct((t // seq_len, SLABS, D_MIX // LANES, seq_len // SLABS, LANES), F32)
    slab_spec = pl.BlockSpec((None, SLABS, D_MIX // LANES, slab_rows, LANES),
                             lambda i: (i // tiles_per_seq, 0, 0, i % tiles_per_seq, 0))
    return pl.pallas_call(
        functools.partial(_inproj_kernel, tiles_per_seq=tiles_per_seq, tm=tm),
        out_shape=[tile_out] * 3 + [slab_out] * 3,
        grid=(t // tm,),
        in_specs=[
            pl.BlockSpec((tm, D_MODEL), lambda i: (i, 0)),
            pl.BlockSpec((HALO, D_MODEL), lambda i: (jnp.maximum(i * per_tile - 1, 0), 0)),
            pl.BlockSpec((HALO, D_MODEL),
                         lambda i: (jnp.minimum((i + 1) * per_tile, n_halo_blocks - 1), 0)),
            pl.BlockSpec((None, N_MOD, D_MODEL), lambda i: (i * tm // seq_len, 0, 0)),
            const((1, D_MODEL)),
            const((D_MODEL, 3 * D_MIX)),
            const((D_MODEL, 3 * D_MIX)),
            const((3, 3 * D_MIX)),
            const((1, 3 * D_MIX)),
        ],
        out_specs=[tile_spec] * 3 + [slab_spec] * 3,
        scratch_shapes=[
            pltpu.VMEM((tm + 2 * HALO, D_MODEL), BF16),
            pltpu.VMEM((tm + 2 * HALO, 3 * D_MIX), F32),
        ],
        compiler_params=_cparams(1),
        name="inproj",
    )(x2d, x2d, x2d, mod, gain.reshape(1, D_MODEL), w_qkv, w_hy, conv_w,
      conv_b.reshape(1, 3 * D_MIX))


ROW_TOKENS = GRID_W
GROUP_ROWS = 2 * WIN_H
GROUP_TOKENS = GROUP_ROWS * ROW_TOKENS
WIN_TOKENS = WIN_H * ROW_TOKENS


def _attn_bias_table(rpb):
    n_heads, n_drow, n_dcol = rpb.shape
    period = 2 * GRID_W - 1
    wrapped = jnp.concatenate([rpb[..., WIN_W - 1:], jnp.zeros((n_heads, n_drow, period - n_dcol), F32),
                               rpb[..., :WIN_W - 1]], axis=-1).astype(F32)
    toeplitz = jnp.tile(wrapped, GRID_W)[..., :GRID_W * (period - 1)]
    toeplitz = toeplitz.reshape(n_heads, n_drow, GRID_W, period - 1)[..., :GRID_W]
    qc = np.arange(GRID_W)[:, None]
    kc = np.arange(GRID_W)[None, :]
    win_start = np.clip(qc - WIN_W // 2, 0, GRID_W - WIN_W)
    col_ok = (kc >= win_start) & (kc < win_start + WIN_W)
    masked = jnp.where(col_ok[None, None], toeplitz * LOG2_E, NEG_INF)
    bias = jnp.stack([masked[:, o:o + WIN_H] for o in range(WIN_H)], axis=0)
    bias = jnp.transpose(bias, (0, 1, 3, 2, 4))
    return bias.reshape(WIN_H, NA_HEADS, GRID_W, WIN_TOKENS)


NATTEN_UNROLL = 4
KV_WINDOW_ROWS = GROUP_ROWS + WIN_H


def _kv_window_start(g, rows):
    return jnp.clip(g * GROUP_ROWS - WIN_H // 2, 0, rows - KV_WINDOW_ROWS)


def _attn_kernel(q_ref, k_ref, v_ref, bias_ref, gain_ref, o_ref, acc, *, rows):
    g = pl.program_id(1)
    win_start = _kv_window_start(g, rows)
    first_head = lax.broadcasted_iota(jnp.int32, (ROW_TOKENS, 2 * NA_HEAD_DIM), 1) < NA_HEAD_DIM
    head_pairs = [slice(hp * 2 * NA_HEAD_DIM, (hp + 1) * 2 * NA_HEAD_DIM) for hp in range(NA_HEADS // 2)]

    def row_body(rr, carry):
        r = g * GROUP_ROWS + rr
        start = jnp.clip(r - WIN_H // 2, 0, rows - WIN_H)
        koff = pl.multiple_of((start - win_start) * ROW_TOKENS, ROW_TOKENS)
        row_class = start - r + WIN_H - 1
        qoff = pl.multiple_of(rr * ROW_TOKENS, ROW_TOKENS)
        scores = []
        for hp, lanes in enumerate(head_pairs):
            q2 = q_ref[pl.ds(qoff, ROW_TOKENS), lanes]
            kw = k_ref[pl.ds(koff, WIN_TOKENS), lanes]
            for hh in range(2):
                keep = first_head if hh == 0 else jnp.logical_not(first_head)
                qm = jnp.where(keep, q2, jnp.zeros_like(q2))
                s = lax.dot_general(qm, kw, (((1,), (1,)), ((), ())), preferred_element_type=F32)
                scores.append(s + bias_ref[row_class, 2 * hp + hh])
        probs, denoms = [], []
        for s in scores:
            p = jnp.exp2(s - jnp.max(s, axis=-1, keepdims=True))
            denoms.append(jnp.sum(p, axis=-1, keepdims=True))
            probs.append(p.astype(BF16))
        for hp, lanes in enumerate(head_pairs):
            vw = v_ref[pl.ds(koff, WIN_TOKENS), lanes]
            outs = [jnp.dot(probs[2 * hp + hh], vw, preferred_element_type=F32) / denoms[2 * hp + hh]
                    for hh in range(2)]
            acc[pl.ds(qoff, ROW_TOKENS), lanes] = jnp.where(first_head, outs[0], outs[1])
        return carry

    lax.fori_loop(0, GROUP_ROWS, row_body, 0, unroll=NATTEN_UNROLL)
    o_ref[...] = _rms(acc[...], gain_ref[...]).astype(BF16)


def _attention(q, k, v, bias, gain, batch, seq_len):
    rows = seq_len // GRID_W
    q3 = q.reshape(batch, seq_len, D_MIX)
    k3 = k.reshape(batch, seq_len, D_MIX)
    v3 = v.reshape(batch, seq_len, D_MIX)
    cur = pl.BlockSpec((None, GROUP_TOKENS, D_MIX), lambda b, g: (b, g, 0))
    window = pl.BlockSpec((None, pl.Element(KV_WINDOW_ROWS * ROW_TOKENS), pl.Element(D_MIX)),
                          lambda b, g: (b, _kv_window_start(g, rows) * ROW_TOKENS, 0))
    out = pl.pallas_call(
        functools.partial(_attn_kernel, rows=rows),
        out_shape=jax.ShapeDtypeStruct((batch, seq_len, D_MIX), BF16),
        grid=(batch, rows // GROUP_ROWS),
        in_specs=[
            cur, window, window,
            pl.BlockSpec((WIN_H, NA_HEADS, GRID_W, WIN_TOKENS), lambda b, g: (0, 0, 0, 0),
                         pipeline_mode=pl.Buffered(1)),
            pl.BlockSpec((1, D_MIX), lambda b, g: (0, 0)),
        ],
        out_specs=cur,
        scratch_shapes=[pltpu.VMEM((GROUP_TOKENS, D_MIX), F32)],
        compiler_params=_cparams(2),
        name="natten",
    )(q3, k3, v3, bias, gain.reshape(1, D_MIX))
    return out.reshape(batch * seq_len, D_MIX)


def _split_hi_lo(x):
    hi = x.astype(ml_dtypes.bfloat16)
    lo = (x - hi.astype(np.float64)).astype(ml_dtypes.bfloat16)
    return hi, lo


def _stack_hi_lo(m):
    hi, lo = _split_hi_lo(m)
    return np.concatenate([hi, lo], axis=-2)


def _embed(re, im):
    return np.concatenate([np.concatenate([re, -im], axis=-1),
                           np.concatenate([im, re], axis=-1)], axis=-2)


@functools.lru_cache(maxsize=None)
def _fft_tables(seq_len):
    n = 2 * seq_len
    n2 = LANE_BLOCK
    n1 = n // n2
    i2 = np.arange(n2)[:, None, None]
    k1 = np.arange(n1)[None, :, None]
    i1 = np.arange(n1)[None, None, :]
    ang = -2.0 * np.pi * ((k1 * (n2 * i1 + i2)) % n) / n
    gr_full, gi_full = np.cos(ang), np.sin(ang)
    g_real = _stack_hi_lo(np.concatenate([gr_full, gi_full], axis=1))
    gr, gi = gr_full[..., :n1 // 2], gi_full[..., :n1 // 2]
    g_fwd = _stack_hi_lo(_embed(gr, gi))
    g_inv = _stack_hi_lo(_embed(np.swapaxes(gr, 1, 2) / n, -np.swapaxes(gi, 1, 2) / n))
    jk = np.outer(np.arange(n2), np.arange(n2))
    ang2 = -2.0 * np.pi * (jk % n2) / n2
    fr, fi = np.cos(ang2), np.sin(ang2)
    f_fwd = _stack_hi_lo(_embed(fr, fi))
    f_inv = _stack_hi_lo(_embed(fr, -fi))
    return n1, g_fwd, g_real, g_inv, f_fwd, f_inv


def _dft3(m_hl, x, m):
    x_hi = x.astype(BF16)
    x_lo = (x - x_hi.astype(F32)).astype(BF16)
    t = jnp.dot(m_hl, x_hi, preferred_element_type=F32)
    return t[:m] + t[m:] + jnp.dot(m_hl[:m], x_lo, preferred_element_type=F32)


def _stage_a_forward(x_ref, g_ref, a_ref, *, n1):
    for i in range(FFT_NB):
        x = jnp.concatenate([_load_strided(x_ref, (0,), i, n1 // 2),
                             _load_strided(x_ref, (1,), i, n1 // 2)], axis=0)
        _store_strided(a_ref, i, _dft3(g_ref[i], x, 2 * n1))


def _stage_a_inverse(d_ref, gi_ref, y_ref, *, n1):
    for i in range(FFT_NB):
        _store_strided(y_ref, i, _dft3(gi_ref[i], _load_strided(d_ref, (), i, 2 * n1), n1))


def _k1_kernel(x_ref, g_ref, a_ref, *, n1):
    _stage_a_forward(x_ref, g_ref, a_ref, n1=n1)


FFT_TILES = FFT_CT // LANES


def _seq_spec(n1):
    return pl.BlockSpec((None, 2, None, FFT_TILES, n1 // 2 * FFT_NB, LANES),
                        lambda c, j, p: (p, 0, j, c, 0, 0))


def _spec_spec(n1):
    return pl.BlockSpec((None, None, FFT_TILES, 2 * n1 * FFT_NB, LANES), lambda c, j, p: (p, j, c, 0, 0))


def _fft_stage_a(x6, g_fwd, n1):
    pairs, _, slabs, tiles, _, _ = x6.shape
    return pl.pallas_call(
        functools.partial(_k1_kernel, n1=n1),
        out_shape=jax.ShapeDtypeStruct((pairs, slabs, tiles, 2 * n1 * FFT_NB, LANES), F32),
        grid=(tiles // FFT_TILES, slabs, pairs),
        in_specs=[
            _seq_spec(n1),
            pl.BlockSpec((FFT_NB, 4 * n1, n1), lambda c, j, p: (j, 0, 0)),
        ],
        out_specs=_spec_spec(n1),
        compiler_params=_cparams(3),
        name="hy_stage_a",
    )(x6, g_fwd)


def _load_low_index(ref, part, kk):
    return jnp.concatenate([ref[:, t, part, kk].reshape(LANE_BLOCK, LANES) for t in range(ref.shape[1])],
                           axis=1)


def _store_low_index(ref, part, kk, val):
    for t in range(ref.shape[1]):
        ref[:, t, part, kk] = val[:, t * LANES:(t + 1) * LANES].reshape(SLABS, FFT_NB, LANES)


def _k2_kernel(a_ref, kf_ref, f_ref, fi_ref, d_ref, *, kb):
    n2 = LANE_BLOCK
    f_hl = f_ref[...]
    fi_hl = fi_ref[...]

    def body(group, carry):
        ks = [group * FFT_K_GROUP + u for u in range(FFT_K_GROUP)]
        spectra = [_dft3(f_hl, jnp.concatenate([_load_low_index(a_ref, 0, kk),
                                                _load_low_index(a_ref, 1, kk)], axis=0), 2 * n2)
                   for kk in ks]
        products = []
        for kk, c in zip(ks, spectra):
            cr, ci = c[:n2], c[n2:]
            kr, ki = kf_ref[0, kk], kf_ref[1, kk]
            products.append(jnp.concatenate([cr * kr - ci * ki, cr * ki + ci * kr], axis=0))
        for kk, y in zip(ks, products):
            d = _dft3(fi_hl, y, 2 * n2)
            _store_low_index(d_ref, 0, kk, d[:n2])
            _store_low_index(d_ref, 1, kk, d[n2:])
        return carry

    lax.fori_loop(0, kb // FFT_K_GROUP, body, 0)


def _fft_stage_c(a4, kf, f_fwd, f_inv, order, n1):
    pairs, slabs, tiles, _, _ = a4.shape
    n2 = LANE_BLOCK
    kb = FFT_KB
    ch_blocks = tiles // FFT_TILES
    a7 = a4.reshape(pairs, slabs, tiles, 2, n1, FFT_NB, LANES)
    spec = pl.BlockSpec((None, slabs, FFT_TILES, 2, kb, FFT_NB, LANES),
                        lambda c, k, p: (p, 0, c, 0, k, 0, 0))
    d7 = pl.pallas_call(
        functools.partial(_k2_kernel, kb=kb),
        out_shape=jax.ShapeDtypeStruct(a7.shape, F32),
        grid=(ch_blocks, n1 // kb, pairs),
        in_specs=[
            spec,
            pl.BlockSpec((2, kb, n2, FFT_CT), lambda c, k, p: (0, k, 0, order * ch_blocks + c)),
            pl.BlockSpec((4 * n2, 2 * n2), lambda c, k, p: (0, 0)),
            pl.BlockSpec((4 * n2, 2 * n2), lambda c, k, p: (0, 0)),
        ],
        out_specs=spec,
        compiler_params=_cparams(3),
        name="hy_stage_c",
    )(a7, kf, f_fwd, f_inv)
    return d7.reshape(a4.shape)


def _k3_kernel(d_ref, gi_ref, z_ref, x_ref, skip_ref, *rest, n1, forward):
    if forward:
        g_ref, o_ref, a_ref, y_ref = rest
    else:
        o_ref, y_ref = rest
    _stage_a_inverse(d_ref, gi_ref, y_ref, n1=n1)
    rows = n1 // 2 * FFT_NB
    for part in range(2):
        for t in range(FFT_TILES):
            conv = y_ref[t, part * rows:(part + 1) * rows, :]
            o_ref[part, t] = x_ref[part, t] * (conv + skip_ref[t] * z_ref[part, t])
    if forward:
        _stage_a_forward(o_ref, g_ref, a_ref, n1=n1)


def _fft_stage_a_inverse(d4, g_inv, z6, x6, skip_row, n1, g_fwd=None):
    pairs, slabs, tiles, _, _ = d4.shape
    forward = g_fwd is not None
    in_specs = [
        _spec_spec(n1),
        pl.BlockSpec((FFT_NB, 2 * n1, 2 * n1), lambda c, j, p: (j, 0, 0)),
        _seq_spec(n1),
        _seq_spec(n1),
        pl.BlockSpec((FFT_TILES, 1, LANES), lambda c, j, p: (c, 0, 0)),
    ]
    args = [d4, g_inv, z6, x6, skip_row]
    out_shape = [jax.ShapeDtypeStruct(z6.shape, F32)]
    out_specs = [_seq_spec(n1)]
    if forward:
        in_specs.append(pl.BlockSpec((FFT_NB, 4 * n1, n1), lambda c, j, p: (j, 0, 0)))
        args.append(g_fwd)
        out_shape.append(jax.ShapeDtypeStruct(d4.shape, F32))
        out_specs.append(_spec_spec(n1))
    return pl.pallas_call(
        functools.partial(_k3_kernel, n1=n1, forward=forward),
        out_shape=out_shape,
        grid=(tiles // FFT_TILES, slabs, pairs),
        in_specs=in_specs,
        out_specs=out_specs,
        scratch_shapes=[pltpu.VMEM((FFT_TILES, n1 * FFT_NB, LANES), F32)],
        compiler_params=_cparams(3),
        name="hy_stage_a_inv_fwd" if forward else "hy_stage_a_inv",
    )(*args)


def _filt_kernel(z_ref, w1_ref, b1_ref, w2_ref, b2_ref, w3_ref, b3_ref, wo_ref, freq_ref,
                 delta_ref, h_ref, l1_ref, *, tl):
    i = pl.program_id(0)
    freq = freq_ref[...]

    def dot(a, b):
        return jnp.dot(a, b, precision=HIGHEST, preferred_element_type=F32)

    z = z_ref[...]
    h = jnp.sin(freq * (dot(z, w1_ref[...]) + b1_ref[...]))
    h = jnp.sin(freq * (dot(h, w2_ref[...]) + b2_ref[...]))
    h = jnp.sin(freq * (dot(h, w3_ref[...]) + b3_ref[...]))
    hf = dot(h, wo_ref[0]) * jnp.exp(-z[:, 0:1] * delta_ref[...])
    hb = dot(h, wo_ref[1]) * jnp.exp(-z[:, FILT_HALF:FILT_HALF + 1] * delta_ref[...])
    row = i * tl + lax.broadcasted_iota(jnp.int32, (tl, 1), 0)
    hb = jnp.where(row == 0, 0.0, hb)
    _store_slabs(h_ref.at[0], hf)
    _store_slabs(h_ref.at[1], hb)

    @pl.when(i == 0)
    def _():
        l1_ref[...] = jnp.zeros_like(l1_ref)

    l1_ref[...] += (jnp.sum(jnp.abs(hf), axis=0, keepdims=True)
                    + jnp.sum(jnp.abs(hb), axis=0, keepdims=True))


def _pad_to(x, shape):
    return jnp.pad(x, [(0, s - d) for d, s in zip(x.shape, shape)])


def _filter_taps(seq_len, w1, b1, w2, b2, w3, b3, wo, freq):
    t = jnp.linspace(0.0, 1.0, seq_len, dtype=F32)[:, None]
    w = 2.0 * math.pi * jnp.arange(seq_len, dtype=F32)[:, None] / seq_len
    f = jnp.linspace(1e-4, HY_BANDS - 1, HY_BANDS, dtype=F32)[None, :]
    z = _pad_to(jnp.concatenate([t, jnp.cos(f * w), -jnp.sin(f * w)], axis=-1), (seq_len, FILT_HALF))
    z = jnp.concatenate([z, z[::-1]], axis=1)
    deltas = jnp.abs(jnp.linspace(math.log(HY_TARGET) / HY_FAST_DECAY,
                                  math.log(HY_TARGET) / HY_SLOW_DECAY, D_MIX, dtype=F32))
    n_cols = HY_ORDER * D_MIX

    def both(m):
        m = _pad_to(m, (FILT_HALF, FILT_HALF))
        zero = jnp.zeros_like(m)
        return jnp.concatenate([jnp.concatenate([m, zero], axis=1),
                                jnp.concatenate([zero, m], axis=1)], axis=0)

    row = lambda v: jnp.tile(_pad_to(v.reshape(1, -1), (1, FILT_HALF)), (1, 2))
    wo_p = _pad_to(wo, (FILT_HALF, 2 * n_cols))
    zero = jnp.zeros((FILT_HALF, n_cols), F32)
    wo_dirs = jnp.stack([jnp.concatenate([wo_p[:, :n_cols], zero], axis=0),
                         jnp.concatenate([zero, wo_p[:, n_cols:]], axis=0)], axis=0)
    pad2 = (FILT_PAD, FILT_PAD)
    tl = FILT_TILE
    const = lambda shape: pl.BlockSpec(shape, lambda i: (0,) * len(shape))
    return pl.pallas_call(
        functools.partial(_filt_kernel, tl=tl),
        out_shape=[jax.ShapeDtypeStruct((2, SLABS, n_cols // LANES, seq_len // SLABS, LANES), F32),
                   jax.ShapeDtypeStruct((1, n_cols), F32)],
        grid=(seq_len // tl,),
        in_specs=[
            pl.BlockSpec((tl, FILT_PAD), lambda i: (i, 0)),
            const(pad2), const((1, FILT_PAD)), const(pad2), const((1, FILT_PAD)),
            const(pad2), const((1, FILT_PAD)), const((2, FILT_PAD, n_cols)), const((1, FILT_PAD)),
            const((1, n_cols)),
        ],
        out_specs=[pl.BlockSpec((2, SLABS, n_cols // LANES, tl // LANE_BLOCK * FFT_NB, LANES),
                                lambda i: (0, 0, 0, i, 0)),
                   const((1, n_cols))],
        compiler_params=_cparams(1),
        name="hy_filter_taps",
    )(z, both(w1), row(b1), both(w2), row(b2), both(w3), row(b3), wo_dirs, row(freq),
      jnp.tile(deltas, HY_ORDER).reshape(1, n_cols))


def _k2f_kernel(a_ref, l1_ref, f_ref, kf_ref, *, kb):
    n2 = LANE_BLOCK
    f_hl = f_ref[...]
    inv_l1 = 1.0 / l1_ref[...]

    def body(group, carry):
        ks = [group * FFT_K_GROUP + u for u in range(FFT_K_GROUP)]
        spectra = [_dft3(f_hl, jnp.concatenate([_load_low_index(a_ref, 0, kk),
                                                _load_low_index(a_ref, 1, kk)], axis=0), 2 * n2)
                   for kk in ks]
        for kk, c in zip(ks, spectra):
            kf_ref[0, kk] = c[:n2] * inv_l1
            kf_ref[1, kk] = c[n2:] * inv_l1
        return carry

    lax.fori_loop(0, kb // FFT_K_GROUP, body, 0)


def _filter_spectrum(a4, l1, f_fwd, n1):
    _, slabs, tiles, _, _ = a4.shape
    cols = tiles * LANES
    n2 = LANE_BLOCK
    kb, ct = FFT_KB, FFT_CT
    a6 = a4.reshape(slabs, tiles, 2, n1, FFT_NB, LANES)
    return pl.pallas_call(
        functools.partial(_k2f_kernel, kb=kb),
        out_shape=jax.ShapeDtypeStruct((2, n1, n2, cols), F32),
        grid=(cols // ct, n1 // kb),
        in_specs=[
            pl.BlockSpec((slabs, FFT_TILES, 2, kb, FFT_NB, LANES), lambda c, k: (0, c, 0, k, 0, 0)),
            pl.BlockSpec((1, ct), lambda c, k: (0, c)),
            pl.BlockSpec((4 * n2, 2 * n2), lambda c, k: (0, 0)),
        ],
        out_specs=pl.BlockSpec((2, kb, n2, ct), lambda c, k: (0, k, 0, c)),
        compiler_params=_cparams(2),
        name="hy_filter_spectrum",
    )(a6, l1, f_fwd)


def _hyena(hv, hx1, hx2, skip, filt_params, seq_len):
    n1, *tables = _fft_tables(seq_len)
    g_fwd, g_real, g_inv, f_fwd, f_inv = (jnp.asarray(m) for m in tables)
    taps, l1 = _filter_taps(seq_len, *filt_params)
    kf = _filter_spectrum(_fft_stage_a(taps[None], g_real, n1), l1, f_fwd, n1)
    as_pairs = lambda a: a.reshape((a.shape[0] // 2, 2) + a.shape[1:])
    z0, x1, x2 = as_pairs(hv), as_pairs(hx1), as_pairs(hx2)
    d = _fft_stage_c(_fft_stage_a(z0, g_fwd, n1), kf, f_fwd, f_inv, 0, n1)
    skip_rows = skip.reshape(HY_ORDER, D_MIX // LANES, 1, LANES)
    z1, a = _fft_stage_a_inverse(d, g_inv, z0, x1, skip_rows[0], n1, g_fwd=g_fwd)
    d = _fft_stage_c(a, kf, f_fwd, f_inv, 1, n1)
    (z2,) = _fft_stage_a_inverse(d, g_inv, z1, x2, skip_rows[1], n1)
    return z2.reshape(hv.shape)


def _trunk(x, mod, p, final_norm):
    batch, seq_len, _ = x.shape
    x2d = x.reshape(batch * seq_len, D_MODEL)
    x2d = _ffn1(x2d, mod, p["ffn1_norm"], p["ffn1_w_gate"], p["ffn1_w_up"], p["ffn1_w_down"], seq_len)
    q, k, v, hv, hx1, hx2 = _inproj(x2d, mod, p["mix_norm"], p["w_qkv"], p["w_hy"],
                                    p["hy_conv_w"], p["hy_conv_b"], seq_len)
    attn_n = _attention(q, k, v, p["attn_bias"], p["attn_out_norm"], batch, seq_len)
    hz = _hyena(hv, hx1, hx2, p["hy_skip"], p["hy_filter"], seq_len)
    y = _mix_ffn2(x2d, attn_n, hz, mod, p["hy_out_norm"], p["w_out_attn"], p["w_out_hy"],
                  p["ffn2_norm"], p["ffn2_w_gate"], p["ffn2_w_up"], p["ffn2_w_down"], final_norm, seq_len)
    return y.reshape(batch, seq_len, D_MODEL)


def kernel(x_prompt, x_sample, c_prompt, c_sample, w_ada, b_ada, ffn1_norm, ffn1_w_gate, ffn1_w_up,
           ffn1_w_down, mix_norm, w_in, na_rpb, hy_conv_w, hy_conv_b, hy_w1, hy_b1, hy_w2, hy_b2,
           hy_w3, hy_b3, hy_wo, hy_sin_freq, hy_skip, attn_out_norm, hy_out_norm, w_out, ffn2_norm,
           ffn2_w_gate, ffn2_w_up, ffn2_w_down, final_norm):
    assert w_ada.shape[0] == 1, "single-layer encoder"
    n_prompt = c_prompt.shape[0]
    mod_all = _ada(jnp.concatenate([c_prompt, c_sample], axis=0), w_ada[0], b_ada[0])
    mod_all = mod_all.reshape(-1, N_MOD, D_MODEL)
    bf = lambda w: w[0].astype(BF16)
    p = {
        "ffn1_norm": ffn1_norm[0], "ffn1_w_gate": bf(ffn1_w_gate), "ffn1_w_up": bf(ffn1_w_up),
        "ffn1_w_down": bf(ffn1_w_down),
        "mix_norm": mix_norm[0],
        "w_qkv": w_in[0, :, :3 * D_MIX].astype(BF16), "w_hy": w_in[0, :, 3 * D_MIX:].astype(BF16),
        "attn_bias": _attn_bias_table(na_rpb[0]),
        "hy_conv_w": hy_conv_w[0], "hy_conv_b": hy_conv_b[0],
        "hy_filter": (hy_w1[0], hy_b1[0], hy_w2[0], hy_b2[0], hy_w3[0], hy_b3[0], hy_wo[0],
                      hy_sin_freq[0]),
        "hy_skip": hy_skip[0],
        "attn_out_norm": attn_out_norm[0], "hy_out_norm": hy_out_norm[0],
        "w_out_attn": w_out[0, :D_MIX].astype(BF16), "w_out_hy": w_out[0, D_MIX:].astype(BF16),
        "ffn2_norm": ffn2_norm[0], "ffn2_w_gate": bf(ffn2_w_gate), "ffn2_w_up": bf(ffn2_w_up),
        "ffn2_w_down": bf(ffn2_w_down),
    }
    y_prompt = _trunk(x_prompt, mod_all[:n_prompt], p, final_norm)
    y_sample = _trunk(x_sample, mod_all[n_prompt:], p, final_norm)
    return (y_prompt, y_sample)
```

```python
import functools
import math

import ml_dtypes
import numpy as np
import jax
import jax.numpy as jnp
from jax import lax
from jax.experimental import pallas as pl
from jax.experimental.pallas import tpu as pltpu

F32 = jnp.float32
BF16 = jnp.bfloat16
HIGHEST = lax.Precision.HIGHEST

D_MODEL = 1024
GRID_W = 64
D_MIX = 512
NA_HEADS = 8
NA_HEAD_DIM = D_MIX // NA_HEADS
WIN_H = 8
WIN_W = 16
HY_ORDER = 2
HY_BANDS = 8
HY_EMB = 1 + 2 * HY_BANDS
HY_FAST_DECAY = 0.3
HY_SLOW_DECAY = 1.5
HY_TARGET = 1e-2
D_FF = ((8 * D_MODEL // 3 + 127) // 128) * 128
N_MOD = 9
EPS = 1e-6
NEG_INF = -1e30
LOG2_E = math.log2(math.e)

V7X_VMEM_LIMIT_BYTES = 56 * 1024 * 1024
TOKEN_TILE = 512
FFN_TILE = 1024
V7X_MXU_WIDTH = 256
_FF_SPLIT = (D_FF // V7X_MXU_WIDTH + 1) // 2 * V7X_MXU_WIDTH
FF_CHUNKS = ((0, _FF_SPLIT), (_FF_SPLIT, D_FF))
HALO = 16
LANE_BLOCK = 128
FFT_CT = 256
FFT_NB = 16
FFT_KB = 16
FFT_K_GROUP = 8
FILT_TILE = 512
FILT_HALF = 64
FILT_PAD = 2 * FILT_HALF


def _cparams(n_axes):
    return pltpu.CompilerParams(
        dimension_semantics=("arbitrary",) * n_axes,
        vmem_limit_bytes=V7X_VMEM_LIMIT_BYTES,
    )


def _rms(x, gain):
    ms = jnp.mean(x * x, axis=-1, keepdims=True)
    return x * lax.rsqrt(ms + EPS) * gain


def _silu(x):
    return x / (1.0 + jnp.exp(-x))


SLABS = LANE_BLOCK // FFT_NB
LANES = 128


def _store_slabs(ref, tile):
    for i1 in range(tile.shape[0] // LANE_BLOCK):
        for t in range(tile.shape[1] // LANES):
            rows = tile[i1 * LANE_BLOCK:(i1 + 1) * LANE_BLOCK, t * LANES:(t + 1) * LANES]
            ref[:, t, i1 * FFT_NB:(i1 + 1) * FFT_NB, :] = rows.reshape(SLABS, FFT_NB, LANES)


def _load_slabs(ref):
    _, tiles, rows, _ = ref.shape
    return jnp.concatenate(
        [jnp.concatenate([ref[:, t, i1 * FFT_NB:(i1 + 1) * FFT_NB, :].reshape(LANE_BLOCK, LANES)
                          for t in range(tiles)], axis=1)
         for i1 in range(rows // FFT_NB)], axis=0)


def _load_strided(ref, lead, start, size):
    tiles = ref.shape[len(lead)]
    return jnp.concatenate([ref[lead + (t, pl.ds(start, size, stride=FFT_NB), slice(None))]
                            for t in range(tiles)], axis=1)


def _store_strided(ref, start, val):
    for t in range(ref.shape[0]):
        ref[t, pl.ds(start, val.shape[0], stride=FFT_NB), :] = val[:, t * LANES:(t + 1) * LANES]


def _ada_kernel(c_ref, w_ref, b_ref, o_ref):
    s = _silu(c_ref[...])
    o_ref[...] = jnp.dot(s, w_ref[...], precision=HIGHEST, preferred_element_type=F32) + b_ref[...]


def _ada(c_all, w_ada, b_ada):
    rows = c_all.shape[0]
    n_out = w_ada.shape[1]
    tn = D_MODEL
    return pl.pallas_call(
        _ada_kernel,
        out_shape=jax.ShapeDtypeStruct((rows, n_out), F32),
        grid=(n_out // tn,),
        in_specs=[
            pl.BlockSpec((rows, D_MODEL), lambda j: (0, 0)),
            pl.BlockSpec((D_MODEL, tn), lambda j: (0, j)),
            pl.BlockSpec((1, tn), lambda j: (0, j)),
        ],
        out_specs=pl.BlockSpec((rows, tn), lambda j: (0, j)),
        compiler_params=_cparams(1),
        name="ada_mod",
    )(c_all, w_ada, b_ada.reshape(1, n_out))


def _ffn_residual(x, mod_ref, mod_base, gain_ref, wg_ref, wu_ref, wd_ref):
    shift = mod_ref[mod_base:mod_base + 1, :]
    scale = mod_ref[mod_base + 1:mod_base + 2, :]
    gate = mod_ref[mod_base + 2:mod_base + 3, :]
    hb = (_rms(x, gain_ref[...]) * (1.0 + scale) + shift).astype(BF16)
    acc = None
    for c0, c1 in FF_CHUNKS:
        g = jnp.dot(hb, wg_ref[:, c0:c1], preferred_element_type=F32)
        u = jnp.dot(hb, wu_ref[:, c0:c1], preferred_element_type=F32)
        a = (_silu(g) * u).astype(BF16)
        d = jnp.dot(a, wd_ref[c0:c1, :], preferred_element_type=F32)
        acc = d if acc is None else acc + d
    return x + 0.5 * gate * acc


def _ffn1_kernel(x_ref, mod_ref, gain_ref, wg_ref, wu_ref, wd_ref, o_ref):
    o_ref[...] = _ffn_residual(x_ref[...], mod_ref, 0, gain_ref, wg_ref, wu_ref, wd_ref)


def _mix_ffn2_kernel(x_ref, an_ref, hz_ref, mod_ref, hy_gain_ref, wa_ref, wh_ref,
                     gain_ref, wg_ref, wu_ref, wd_ref, fn_ref, o_ref):
    hn = _rms(_load_slabs(hz_ref), hy_gain_ref[...]).astype(BF16)
    mixed = (jnp.dot(an_ref[...], wa_ref[...], preferred_element_type=F32)
             + jnp.dot(hn, wh_ref[...], preferred_element_type=F32))
    x = x_ref[...] + mod_ref[5:6, :] * mixed
    y = _ffn_residual(x, mod_ref, 6, gain_ref, wg_ref, wu_ref, wd_ref)
    o_ref[...] = _rms(y, fn_ref[...])


def _resident(shape):
    return pl.BlockSpec(shape, lambda i: (0, 0), pipeline_mode=pl.Buffered(1))


def _token_spec(tm, width):
    return pl.BlockSpec((tm, width), lambda i: (i, 0))


def _mod_spec(tm, seq_len):
    return pl.BlockSpec((None, N_MOD, D_MODEL), lambda i: (i * tm // seq_len, 0, 0))


def _ffn1(x2d, mod, gain, wg, wu, wd, seq_len):
    t = x2d.shape[0]
    tm = FFN_TILE
    return pl.pallas_call(
        _ffn1_kernel,
        out_shape=jax.ShapeDtypeStruct((t, D_MODEL), F32),
        grid=(t // tm,),
        in_specs=[
            _token_spec(tm, D_MODEL), _mod_spec(tm, seq_len), _resident((1, D_MODEL)),
            _resident((D_MODEL, D_FF)), _resident((D_MODEL, D_FF)), _resident((D_FF, D_MODEL)),
        ],
        out_specs=_token_spec(tm, D_MODEL),
        compiler_params=_cparams(1),
        name="ffn1",
    )(x2d, mod, gain.reshape(1, D_MODEL), wg, wu, wd)


def _mix_ffn2(x2d, attn_n, hz, mod, hy_gain, w_attn, w_hy, gain, wg, wu, wd, final_gain, seq_len):
    t = x2d.shape[0]
    tm = FFN_TILE
    tiles_per_seq = seq_len // tm
    return pl.pallas_call(
        _mix_ffn2_kernel,
        out_shape=jax.ShapeDtypeStruct((t, D_MODEL), F32),
        grid=(t // tm,),
        in_specs=[
            _token_spec(tm, D_MODEL),
            _token_spec(tm, D_MIX),
            pl.BlockSpec((None, SLABS, D_MIX // LANES, tm // LANE_BLOCK * FFT_NB, LANES),
                         lambda i: (i // tiles_per_seq, 0, 0, i % tiles_per_seq, 0)),
            _mod_spec(tm, seq_len),
            _resident((1, D_MIX)), _resident((D_MIX, D_MODEL)), _resident((D_MIX, D_MODEL)),
            _resident((1, D_MODEL)),
            _resident((D_MODEL, D_FF)), _resident((D_MODEL, D_FF)), _resident((D_FF, D_MODEL)),
            _resident((1, D_MODEL)),
        ],
        out_specs=_token_spec(tm, D_MODEL),
        compiler_params=_cparams(1),
        name="mix_ffn2",
    )(x2d, attn_n, hz, mod, hy_gain.reshape(1, D_MIX), w_attn, w_hy, gain.reshape(1, D_MODEL),
      wg, wu, wd, final_gain.reshape(1, D_MODEL))


def _inproj_kernel(x_ref, xp_ref, xn_ref, mod_ref, gain_ref, wqkv_ref, why_ref, cw_ref, cb_ref,
                   q_ref, k_ref, v_ref, hv_ref, hx1_ref, hx2_ref, ext_ref, u_ref,
                   *, tiles_per_seq, tm):
    pos = pl.program_id(0) % tiles_per_seq
    gain = gain_ref[...]
    shift = mod_ref[3:4, :]
    scale = 1.0 + mod_ref[4:5, :]

    def normed(x):
        return _rms(x, gain) * scale + shift

    hb = normed(x_ref[...]).astype(BF16)
    qkv = jnp.dot(hb, wqkv_ref[...], preferred_element_type=F32)
    q_ref[...] = (qkv[:, :D_MIX] * (NA_HEAD_DIM ** -0.5 * LOG2_E)).astype(BF16)
    k_ref[...] = qkv[:, D_MIX:2 * D_MIX].astype(BF16)
    v_ref[...] = qkv[:, 2 * D_MIX:].astype(BF16)

    has_prev = jnp.where(pos != 0, 1.0, 0.0)
    has_next = jnp.where(pos != tiles_per_seq - 1, 1.0, 0.0)
    ext_ref[0:HALO, :] = (normed(xp_ref[...]) * has_prev).astype(BF16)
    ext_ref[HALO:HALO + tm, :] = hb
    ext_ref[HALO + tm:, :] = (normed(xn_ref[...]) * has_next).astype(BF16)
    u_ref[...] = jnp.dot(ext_ref[...], why_ref[...], preferred_element_type=F32)
    conv = (cb_ref[...]
            + u_ref[pl.ds(HALO - 1, tm), :] * cw_ref[0:1, :]
            + u_ref[pl.ds(HALO, tm), :] * cw_ref[1:2, :]
            + u_ref[pl.ds(HALO + 1, tm), :] * cw_ref[2:3, :])
    for part, ref in enumerate((hv_ref, hx1_ref, hx2_ref)):
        _store_slabs(ref, conv[:, part * D_MIX:(part + 1) * D_MIX])


def _inproj(x2d, mod, gain, w_qkv, w_hy, conv_w, conv_b, seq_len):
    t = x2d.shape[0]
    tm = TOKEN_TILE
    n_halo_blocks = t // HALO
    per_tile = tm // HALO
    const = lambda shape: pl.BlockSpec(shape, lambda i: (0, 0))
    tiles_per_seq = seq_len // tm
    tile_out = jax.ShapeDtypeStruct((t, D_MIX), BF16)
    tile_spec = pl.BlockSpec((tm, D_MIX), lambda i: (i, 0))
    slab_rows = tm // LANE_BLOCK * FFT_NB
    slab_out = jax.ShapeDtypeStruct((t // seq_len, SLABS, D_MIX // LANES, seq_len // SLABS, LANES), F32)
    slab_spec = pl.BlockSpec((None, SLABS, D_MIX // LANES, slab_rows, LANES),
                             lambda i: (i // tiles_per_seq, 0, 0, i % tiles_per_seq, 0))
    return pl.pallas_call(
        functools.partial(_inproj_kernel, tiles_per_seq=tiles_per_seq, tm=tm),
        out_shape=[tile_out] * 3 + [slab_out] * 3,
        grid=(t // tm,),
        in_specs=[
            pl.BlockSpec((tm, D_MODEL), lambda i: (i, 0)),
            pl.BlockSpec((HALO, D_MODEL), lambda i: (jnp.maximum(i * per_tile - 1, 0), 0)),
            pl.BlockSpec((HALO, D_MODEL),
                         lambda i: (jnp.minimum((i + 1) * per_tile, n_halo_blocks - 1), 0)),
            pl.BlockSpec((None, N_MOD, D_MODEL), lambda i: (i * tm // seq_len, 0, 0)),
            const((1, D_MODEL)),
            const((D_MODEL, 3 * D_MIX)),
            const((D_MODEL, 3 * D_MIX)),
            const((3, 3 * D_MIX)),
            const((1, 3 * D_MIX)),
        ],
        out_specs=[tile_spec] * 3 + [slab_spec] * 3,
        scratch_shapes=[
            pltpu.VMEM((tm + 2 * HALO, D_MODEL), BF16),
            pltpu.VMEM((tm + 2 * HALO, 3 * D_MIX), F32),
        ],
        compiler_params=_cparams(1),
        name="inproj",
    )(x2d, x2d, x2d, mod, gain.reshape(1, D_MODEL), w_qkv, w_hy, conv_w,
      conv_b.reshape(1, 3 * D_MIX))


ROW_TOKENS = GRID_W
GROUP_ROWS = 2 * WIN_H
GROUP_TOKENS = GROUP_ROWS * ROW_TOKENS
WIN_TOKENS = WIN_H * ROW_TOKENS


def _attn_bias_table(rpb):
    n_heads, n_drow, n_dcol = rpb.shape
    period = 2 * GRID_W - 1
    wrapped = jnp.concatenate([rpb[..., WIN_W - 1:], jnp.zeros((n_heads, n_drow, period - n_dcol), F32),
                               rpb[..., :WIN_W - 1]], axis=-1).astype(F32)
    toeplitz = jnp.tile(wrapped, GRID_W)[..., :GRID_W * (period - 1)]
    toeplitz = toeplitz.reshape(n_heads, n_drow, GRID_W, period - 1)[..., :GRID_W]
    qc = np.arange(GRID_W)[:, None]
    kc = np.arange(GRID_W)[None, :]
    win_start = np.clip(qc - WIN_W // 2, 0, GRID_W - WIN_W)
    col_ok = (kc >= win_start) & (kc < win_start + WIN_W)
    masked = jnp.where(col_ok[None, None], toeplitz * LOG2_E, NEG_INF)
    bias = jnp.stack([masked[:, o:o + WIN_H] for o in range(WIN_H)], axis=0)
    bias = jnp.transpose(bias, (0, 1, 3, 2, 4))
    return bias.reshape(WIN_H, NA_HEADS, GRID_W, WIN_TOKENS)


NATTEN_UNROLL = 8
KV_WINDOW_ROWS = GROUP_ROWS + WIN_H


def _kv_window_start(g, rows):
    return jnp.clip(g * GROUP_ROWS - WIN_H // 2, 0, rows - KV_WINDOW_ROWS)


def _attn_kernel(q_ref, k_ref, v_ref, bias_ref, gain_ref, o_ref, acc, *, rows):
    g = pl.program_id(1)
    win_start = _kv_window_start(g, rows)
    first_head = lax.broadcasted_iota(jnp.int32, (ROW_TOKENS, 2 * NA_HEAD_DIM), 1) < NA_HEAD_DIM
    head_pairs = [slice(hp * 2 * NA_HEAD_DIM, (hp + 1) * 2 * NA_HEAD_DIM) for hp in range(NA_HEADS // 2)]

    def row_body(rr, carry):
        r = g * GROUP_ROWS + rr
        start = jnp.clip(r - WIN_H // 2, 0, rows - WIN_H)
        koff = pl.multiple_of((start - win_start) * ROW_TOKENS, ROW_TOKENS)
        row_class = start - r + WIN_H - 1
        qoff = pl.multiple_of(rr * ROW_TOKENS, ROW_TOKENS)
        scores = []
        for hp, lanes in enumerate(head_pairs):
            q2 = q_ref[pl.ds(qoff, ROW_TOKENS), lanes]
            kw = k_ref[pl.ds(koff, WIN_TOKENS), lanes]
            for hh in range(2):
                keep = first_head if hh == 0 else jnp.logical_not(first_head)
                qm = jnp.where(keep, q2, jnp.zeros_like(q2))
                s = lax.dot_general(qm, kw, (((1,), (1,)), ((), ())), preferred_element_type=F32)
                scores.append(s + bias_ref[row_class, 2 * hp + hh])
        probs, denoms = [], []
        for s in scores:
            p = jnp.exp2(s - jnp.max(s, axis=-1, keepdims=True))
            denoms.append(jnp.sum(p, axis=-1, keepdims=True))
            probs.append(p.astype(BF16))
        for hp, lanes in enumerate(head_pairs):
            vw = v_ref[pl.ds(koff, WIN_TOKENS), lanes]
            outs = [jnp.dot(probs[2 * hp + hh], vw, preferred_element_type=F32) / denoms[2 * hp + hh]
                    for hh in range(2)]
            acc[pl.ds(qoff, ROW_TOKENS), lanes] = jnp.where(first_head, outs[0], outs[1])
        return carry

    lax.fori_loop(0, GROUP_ROWS, row_body, 0, unroll=NATTEN_UNROLL)
    o_ref[...] = _rms(acc[...], gain_ref[...]).astype(BF16)


def _attention(q, k, v, bias, gain, batch, seq_len):
    rows = seq_len // GRID_W
    q3 = q.reshape(batch, seq_len, D_MIX)
    k3 = k.reshape(batch, seq_len, D_MIX)
    v3 = v.reshape(batch, seq_len, D_MIX)
    cur = pl.BlockSpec((None, GROUP_TOKENS, D_MIX), lambda b, g: (b, g, 0))
    window = pl.BlockSpec((None, pl.Element(KV_WINDOW_ROWS * ROW_TOKENS), pl.Element(D_MIX)),
                          lambda b, g: (b, _kv_window_start(g, rows) * ROW_TOKENS, 0))
    out = pl.pallas_call(
        functools.partial(_attn_kernel, rows=rows),
        out_shape=jax.ShapeDtypeStruct((batch, seq_len, D_MIX), BF16),
        grid=(batch, rows // GROUP_ROWS),
        in_specs=[
            cur, window, window,
            pl.BlockSpec((WIN_H, NA_HEADS, GRID_W, WIN_TOKENS), lambda b, g: (0, 0, 0, 0),
                         pipeline_mode=pl.Buffered(1)),
            pl.BlockSpec((1, D_MIX), lambda b, g: (0, 0)),
        ],
        out_specs=cur,
        scratch_shapes=[pltpu.VMEM((GROUP_TOKENS, D_MIX), F32)],
        compiler_params=_cparams(2),
        name="natten",
    )(q3, k3, v3, bias, gain.reshape(1, D_MIX))
    return out.reshape(batch * seq_len, D_MIX)


def _split_hi_lo(x):
    hi = x.astype(ml_dtypes.bfloat16)
    lo = (x - hi.astype(np.float64)).astype(ml_dtypes.bfloat16)
    return hi, lo


def _stack_hi_lo(m):
    hi, lo = _split_hi_lo(m)
    return np.concatenate([hi, lo], axis=-2)


def _embed(re, im):
    return np.concatenate([np.concatenate([re, -im], axis=-1),
                           np.concatenate([im, re], axis=-1)], axis=-2)


@functools.lru_cache(maxsize=None)
def _fft_tables(seq_len):
    n = 2 * seq_len
    n2 = LANE_BLOCK
    n1 = n // n2
    i2 = np.arange(n2)[:, None, None]
    k1 = np.arange(n1)[None, :, None]
    i1 = np.arange(n1)[None, None, :]
    ang = -2.0 * np.pi * ((k1 * (n2 * i1 + i2)) % n) / n
    gr_full, gi_full = np.cos(ang), np.sin(ang)
    g_real = _stack_hi_lo(np.concatenate([gr_full, gi_full], axis=1))
    gr, gi = gr_full[..., :n1 // 2], gi_full[..., :n1 // 2]
    g_fwd = _stack_hi_lo(_embed(gr, gi))
    g_inv = _stack_hi_lo(_embed(np.swapaxes(gr, 1, 2) / n, -np.swapaxes(gi, 1, 2) / n))
    jk = np.outer(np.arange(n2), np.arange(n2))
    ang2 = -2.0 * np.pi * (jk % n2) / n2
    fr, fi = np.cos(ang2), np.sin(ang2)
    f_fwd = _stack_hi_lo(_embed(fr, fi))
    f_inv = _stack_hi_lo(_embed(fr, -fi))
    return n1, g_fwd, g_real, g_inv, f_fwd, f_inv


def _dft3(m_hl, x, m):
    x_hi = x.astype(BF16)
    x_lo = (x - x_hi.astype(F32)).astype(BF16)
    t = jnp.dot(m_hl, x_hi, preferred_element_type=F32)
    return t[:m] + t[m:] + jnp.dot(m_hl[:m], x_lo, preferred_element_type=F32)


def _stage_a_forward(x_ref, g_ref, a_ref, *, n1):
    for i in range(FFT_NB):
        x = jnp.concatenate([_load_strided(x_ref, (0,), i, n1 // 2),
                             _load_strided(x_ref, (1,), i, n1 // 2)], axis=0)
        _store_strided(a_ref, i, _dft3(g_ref[i], x, 2 * n1))


def _stage_a_inverse(d_ref, gi_ref, y_ref, *, n1):
    for i in range(FFT_NB):
        _store_strided(y_ref, i, _dft3(gi_ref[i], _load_strided(d_ref, (), i, 2 * n1), n1))


def _k1_kernel(x_ref, g_ref, a_ref, *, n1):
    _stage_a_forward(x_ref, g_ref, a_ref, n1=n1)


FFT_TILES = FFT_CT // LANES


def _seq_spec(n1):
    return pl.BlockSpec((None, 2, None, FFT_TILES, n1 // 2 * FFT_NB, LANES),
                        lambda c, j, p: (p, 0, j, c, 0, 0))


def _spec_spec(n1):
    return pl.BlockSpec((None, None, FFT_TILES, 2 * n1 * FFT_NB, LANES), lambda c, j, p: (p, j, c, 0, 0))


def _fft_stage_a(x6, g_fwd, n1):
    pairs, _, slabs, tiles, _, _ = x6.shape
    return pl.pallas_call(
        functools.partial(_k1_kernel, n1=n1),
        out_shape=jax.ShapeDtypeStruct((pairs, slabs, tiles, 2 * n1 * FFT_NB, LANES), F32),
        grid=(tiles // FFT_TILES, slabs, pairs),
        in_specs=[
            _seq_spec(n1),
            pl.BlockSpec((FFT_NB, 4 * n1, n1), lambda c, j, p: (j, 0, 0)),
        ],
        out_specs=_spec_spec(n1),
        compiler_params=_cparams(3),
        name="hy_stage_a",
    )(x6, g_fwd)


def _load_low_index(ref, part, kk):
    return jnp.concatenate([ref[:, t, part, kk].reshape(LANE_BLOCK, LANES) for t in range(ref.shape[1])],
                           axis=1)


def _store_low_index(ref, part, kk, val):
    for t in range(ref.shape[1]):
        ref[:, t, part, kk] = val[:, t * LANES:(t + 1) * LANES].reshape(SLABS, FFT_NB, LANES)


def _k2_kernel(a_ref, kf_ref, f_ref, fi_ref, d_ref, *, kb):
    n2 = LANE_BLOCK
    f_hl = f_ref[...]
    fi_hl = fi_ref[...]

    def body(group, carry):
        ks = [group * FFT_K_GROUP + u for u in range(FFT_K_GROUP)]
        spectra = [_dft3(f_hl, jnp.concatenate([_load_low_index(a_ref, 0, kk),
                                                _load_low_index(a_ref, 1, kk)], axis=0), 2 * n2)
                   for kk in ks]
        products = []
        for kk, c in zip(ks, spectra):
            cr, ci = c[:n2], c[n2:]
            kr, ki = kf_ref[0, kk], kf_ref[1, kk]
            products.append(jnp.concatenate([cr * kr - ci * ki, cr * ki + ci * kr], axis=0))
        for kk, y in zip(ks, products):
            d = _dft3(fi_hl, y, 2 * n2)
            _store_low_index(d_ref, 0, kk, d[:n2])
            _store_low_index(d_ref, 1, kk, d[n2:])
        return carry

    lax.fori_loop(0, kb // FFT_K_GROUP, body, 0)


def _fft_stage_c(a4, kf, f_fwd, f_inv, order, n1):
    pairs, slabs, tiles, _, _ = a4.shape
    n2 = LANE_BLOCK
    kb = FFT_KB
    ch_blocks = tiles // FFT_TILES
    a7 = a4.reshape(pairs, slabs, tiles, 2, n1, FFT_NB, LANES)
    spec = pl.BlockSpec((None, slabs, FFT_TILES, 2, kb, FFT_NB, LANES),
                        lambda c, k, p: (p, 0, c, 0, k, 0, 0))
    d7 = pl.pallas_call(
        functools.partial(_k2_kernel, kb=kb),
        out_shape=jax.ShapeDtypeStruct(a7.shape, F32),
        grid=(ch_blocks, n1 // kb, pairs),
        in_specs=[
            spec,
            pl.BlockSpec((2, kb, n2, FFT_CT), lambda c, k, p: (0, k, 0, order * ch_blocks + c)),
            pl.BlockSpec((4 * n2, 2 * n2), lambda c, k, p: (0, 0)),
            pl.BlockSpec((4 * n2, 2 * n2), lambda c, k, p: (0, 0)),
        ],
        out_specs=spec,
        compiler_params=_cparams(3),
        name="hy_stage_c",
    )(a7, kf, f_fwd, f_inv)
    return d7.reshape(a4.shape)


def _k3_kernel(d_ref, gi_ref, z_ref, x_ref, skip_ref, *rest, n1, forward):
    if forward:
        g_ref, o_ref, a_ref, y_ref = rest
    else:
        o_ref, y_ref = rest
    _stage_a_inverse(d_ref, gi_ref, y_ref, n1=n1)
    rows = n1 // 2 * FFT_NB
    for part in range(2):
        for t in range(FFT_TILES):
            conv = y_ref[t, part * rows:(part + 1) * rows, :]
            o_ref[part, t] = x_ref[part, t] * (conv + skip_ref[t] * z_ref[part, t])
    if forward:
        _stage_a_forward(o_ref, g_ref, a_ref, n1=n1)


def _fft_stage_a_inverse(d4, g_inv, z6, x6, skip_row, n1, g_fwd=None):
    pairs, slabs, tiles, _, _ = d4.shape
    forward = g_fwd is not None
    in_specs = [
        _spec_spec(n1),
        pl.BlockSpec((FFT_NB, 2 * n1, 2 * n1), lambda c, j, p: (j, 0, 0)),
        _seq_spec(n1),
        _seq_spec(n1),
        pl.BlockSpec((FFT_TILES, 1, LANES), lambda c, j, p: (c, 0, 0)),
    ]
    args = [d4, g_inv, z6, x6, skip_row]
    out_shape = [jax.ShapeDtypeStruct(z6.shape, F32)]
    out_specs = [_seq_spec(n1)]
    if forward:
        in_specs.append(pl.BlockSpec((FFT_NB, 4 * n1, n1), lambda c, j, p: (j, 0, 0)))
        args.append(g_fwd)
        out_shape.append(jax.ShapeDtypeStruct(d4.shape, F32))
        out_specs.append(_spec_spec(n1))
    return pl.pallas_call(
        functools.partial(_k3_kernel, n1=n1, forward=forward),
        out_shape=out_shape,
        grid=(tiles // FFT_TILES, slabs, pairs),
        in_specs=in_specs,
        out_specs=out_specs,
        scratch_shapes=[pltpu.VMEM((FFT_TILES, n1 * FFT_NB, LANES), F32)],
        compiler_params=_cparams(3),
        name="hy_stage_a_inv_fwd" if forward else "hy_stage_a_inv",
    )(*args)


def _filt_kernel(z_ref, w1_ref, b1_ref, w2_ref, b2_ref, w3_ref, b3_ref, wo_ref, freq_ref,
                 delta_ref, h_ref, l1_ref, *, tl):
    i = pl.program_id(0)
    freq = freq_ref[...]

    def dot(a, b):
        return jnp.dot(a, b, precision=HIGHEST, preferred_element_type=F32)

    z = z_ref[...]
    h = jnp.sin(freq * (dot(z, w1_ref[...]) + b1_ref[...]))
    h = jnp.sin(freq * (dot(h, w2_ref[...]) + b2_ref[...]))
    h = jnp.sin(freq * (dot(h, w3_ref[...]) + b3_ref[...]))
    hf = dot(h, wo_ref[0]) * jnp.exp(-z[:, 0:1] * delta_ref[...])
    hb = dot(h, wo_ref[1]) * jnp.exp(-z[:, FILT_HALF:FILT_HALF + 1] * delta_ref[...])
    row = i * tl + lax.broadcasted_iota(jnp.int32, (tl, 1), 0)
    hb = jnp.where(row == 0, 0.0, hb)
    _store_slabs(h_ref.at[0], hf)
    _store_slabs(h_ref.at[1], hb)

    @pl.when(i == 0)
    def _():
        l1_ref[...] = jnp.zeros_like(l1_ref)

    l1_ref[...] += (jnp.sum(jnp.abs(hf), axis=0, keepdims=True)
                    + jnp.sum(jnp.abs(hb), axis=0, keepdims=True))


def _pad_to(x, shape):
    return jnp.pad(x, [(0, s - d) for d, s in zip(x.shape, shape)])


def _filter_taps(seq_len, w1, b1, w2, b2, w3, b3, wo, freq):
    t = jnp.linspace(0.0, 1.0, seq_len, dtype=F32)[:, None]
    w = 2.0 * math.pi * jnp.arange(seq_len, dtype=F32)[:, None] / seq_len
    f = jnp.linspace(1e-4, HY_BANDS - 1, HY_BANDS, dtype=F32)[None, :]
    z = _pad_to(jnp.concatenate([t, jnp.cos(f * w), -jnp.sin(f * w)], axis=-1), (seq_len, FILT_HALF))
    z = jnp.concatenate([z, z[::-1]], axis=1)
    deltas = jnp.abs(jnp.linspace(math.log(HY_TARGET) / HY_FAST_DECAY,
                                  math.log(HY_TARGET) / HY_SLOW_DECAY, D_MIX, dtype=F32))
    n_cols = HY_ORDER * D_MIX

    def both(m):
        m = _pad_to(m, (FILT_HALF, FILT_HALF))
        zero = jnp.zeros_like(m)
        return jnp.concatenate([jnp.concatenate([m, zero], axis=1),
                                jnp.concatenate([zero, m], axis=1)], axis=0)

    row = lambda v: jnp.tile(_pad_to(v.reshape(1, -1), (1, FILT_HALF)), (1, 2))
    wo_p = _pad_to(wo, (FILT_HALF, 2 * n_cols))
    zero = jnp.zeros((FILT_HALF, n_cols), F32)
    wo_dirs = jnp.stack([jnp.concatenate([wo_p[:, :n_cols], zero], axis=0),
                         jnp.concatenate([zero, wo_p[:, n_cols:]], axis=0)], axis=0)
    pad2 = (FILT_PAD, FILT_PAD)
    tl = FILT_TILE
    const = lambda shape: pl.BlockSpec(shape, lambda i: (0,) * len(shape))
    return pl.pallas_call(
        functools.partial(_filt_kernel, tl=tl),
        out_shape=[jax.ShapeDtypeStruct((2, SLABS, n_cols // LANES, seq_len // SLABS, LANES), F32),
                   jax.ShapeDtypeStruct((1, n_cols), F32)],
        grid=(seq_len // tl,),
        in_specs=[
            pl.BlockSpec((tl, FILT_PAD), lambda i: (i, 0)),
            const(pad2), const((1, FILT_PAD)), const(pad2), const((1, FILT_PAD)),
            const(pad2), const((1, FILT_PAD)), const((2, FILT_PAD, n_cols)), const((1, FILT_PAD)),
            const((1, n_cols)),
        ],
        out_specs=[pl.BlockSpec((2, SLABS, n_cols // LANES, tl // LANE_BLOCK * FFT_NB, LANES),
                                lambda i: (0, 0, 0, i, 0)),
                   const((1, n_cols))],
        compiler_params=_cparams(1),
        name="hy_filter_taps",
    )(z, both(w1), row(b1), both(w2), row(b2), both(w3), row(b3), wo_dirs, row(freq),
      jnp.tile(deltas, HY_ORDER).reshape(1, n_cols))


def _k2f_kernel(a_ref, l1_ref, f_ref, kf_ref, *, kb):
    n2 = LANE_BLOCK
    f_hl = f_ref[...]
    inv_l1 = 1.0 / l1_ref[...]

    def body(group, carry):
        ks = [group * FFT_K_GROUP + u for u in range(FFT_K_GROUP)]
        spectra = [_dft3(f_hl, jnp.concatenate([_load_low_index(a_ref, 0, kk),
                                                _load_low_index(a_ref, 1, kk)], axis=0), 2 * n2)
                   for kk in ks]
        for kk, c in zip(ks, spectra):
            kf_ref[0, kk] = c[:n2] * inv_l1
            kf_ref[1, kk] = c[n2:] * inv_l1
        return carry

    lax.fori_loop(0, kb // FFT_K_GROUP, body, 0)


def _filter_spectrum(a4, l1, f_fwd, n1):
    _, slabs, tiles, _, _ = a4.shape
    cols = tiles * LANES
    n2 = LANE_BLOCK
    kb, ct = FFT_KB, FFT_CT
    a6 = a4.reshape(slabs, tiles, 2, n1, FFT_NB, LANES)
    return pl.pallas_call(
        functools.partial(_k2f_kernel, kb=kb),
        out_shape=jax.ShapeDtypeStruct((2, n1, n2, cols), F32),
        grid=(cols // ct, n1 // kb),
        in_specs=[
            pl.BlockSpec((slabs, FFT_TILES, 2, kb, FFT_NB, LANES), lambda c, k: (0, c, 0, k, 0, 0)),
            pl.BlockSpec((1, ct), lambda c, k: (0, c)),
            pl.BlockSpec((4 * n2, 2 * n2), lambda c, k: (0, 0)),
        ],
        out_specs=pl.BlockSpec((2, kb, n2, ct), lambda c, k: (0, k, 0, c)),
        compiler_params=_cparams(2),
        name="hy_filter_spectrum",
    )(a6, l1, f_fwd)


def _hyena(hv, hx1, hx2, skip, filt_params, seq_len):
    n1, *tables = _fft_tables(seq_len)
    g_fwd, g_real, g_inv, f_fwd, f_inv = (jnp.asarray(m) for m in tables)
    taps, l1 = _filter_taps(seq_len, *filt_params)
    kf = _filter_spectrum(_fft_stage_a(taps[None], g_real, n1), l1, f_fwd, n1)
    as_pairs = lambda a: a.reshape((a.shape[0] // 2, 2) + a.shape[1:])
    z0, x1, x2 = as_pairs(hv), as_pairs(hx1), as_pairs(hx2)
    d = _fft_stage_c(_fft_stage_a(z0, g_fwd, n1), kf, f_fwd, f_inv, 0, n1)
    skip_rows = skip.reshape(HY_ORDER, D_MIX // LANES, 1, LANES)
    z1, a = _fft_stage_a_inverse(d, g_inv, z0, x1, skip_rows[0], n1, g_fwd=g_fwd)
    d = _fft_stage_c(a, kf, f_fwd, f_inv, 1, n1)
    (z2,) = _fft_stage_a_inverse(d, g_inv, z1, x2, skip_rows[1], n1)
    return z2.reshape(hv.shape)


def _trunk(x, mod, p, final_norm):
    batch, seq_len, _ = x.shape
    x2d = x.reshape(batch * seq_len, D_MODEL)
    x2d = _ffn1(x2d, mod, p["ffn1_norm"], p["ffn1_w_gate"], p["ffn1_w_up"], p["ffn1_w_down"], seq_len)
    q, k, v, hv, hx1, hx2 = _inproj(x2d, mod, p["mix_norm"], p["w_qkv"], p["w_hy"],
                                    p["hy_conv_w"], p["hy_conv_b"], seq_len)
    attn_n = _attention(q, k, v, p["attn_bias"], p["attn_out_norm"], batch, seq_len)
    hz = _hyena(hv, hx1, hx2, p["hy_skip"], p["hy_filter"], seq_len)
    y = _mix_ffn2(x2d, attn_n, hz, mod, p["hy_out_norm"], p["w_out_attn"], p["w_out_hy"],
                  p["ffn2_norm"], p["ffn2_w_gate"], p["ffn2_w_up"], p["ffn2_w_down"], final_norm, seq_len)
    return y.reshape(batch, seq_len, D_MODEL)


def kernel(x_prompt, x_sample, c_prompt, c_sample, w_ada, b_ada, ffn1_norm, ffn1_w_gate, ffn1_w_up,
           ffn1_w_down, mix_norm, w_in, na_rpb, hy_conv_w, hy_conv_b, hy_w1, hy_b1, hy_w2, hy_b2,
           hy_w3, hy_b3, hy_wo, hy_sin_freq, hy_skip, attn_out_norm, hy_out_norm, w_out, ffn2_norm,
           ffn2_w_gate, ffn2_w_up, ffn2_w_down, final_norm):
    assert w_ada.shape[0] == 1, "single-layer encoder"
    n_prompt = c_prompt.shape[0]
    mod_all = _ada(jnp.concatenate([c_prompt, c_sample], axis=0), w_ada[0], b_ada[0])
    mod_all = mod_all.reshape(-1, N_MOD, D_MODEL)
    bf = lambda w: w[0].astype(BF16)
    p = {
        "ffn1_norm": ffn1_norm[0], "ffn1_w_gate": bf(ffn1_w_gate), "ffn1_w_up": bf(ffn1_w_up),
        "ffn1_w_down": bf(ffn1_w_down),
        "mix_norm": mix_norm[0],
        "w_qkv": w_in[0, :, :3 * D_MIX].astype(BF16), "w_hy": w_in[0, :, 3 * D_MIX:].astype(BF16),
        "attn_bias": _attn_bias_table(na_rpb[0]),
        "hy_conv_w": hy_conv_w[0], "hy_conv_b": hy_conv_b[0],
        "hy_filter": (hy_w1[0], hy_b1[0], hy_w2[0], hy_b2[0], hy_w3[0], hy_b3[0], hy_wo[0],
                      hy_sin_freq[0]),
        "hy_skip": hy_skip[0],
        "attn_out_norm": attn_out_norm[0], "hy_out_norm": hy_out_norm[0],
        "w_out_attn": w_out[0, :D_MIX].astype(BF16), "w_out_hy": w_out[0, D_MIX:].astype(BF16),
        "ffn2_norm": ffn2_norm[0], "ffn2_w_gate": bf(ffn2_w_gate), "ffn2_w_up": bf(ffn2_w_up),
        "ffn2_w_down": bf(ffn2_w_down),
    }
    y_prompt = _trunk(x_prompt, mod_all[:n_prompt], p, final_norm)
    y_sample = _trunk(x_sample, mod_all[n_prompt:], p, final_norm)
    return (y_prompt, y_sample)
```

```python
import functools
import math

import ml_dtypes
import numpy as np
import jax
import jax.numpy as jnp
from jax import lax
from jax.experimental import pallas as pl
from jax.experimental.pallas import tpu as pltpu

F32 = jnp.float32
BF16 = jnp.bfloat16
HIGHEST = lax.Precision.HIGHEST

D_MODEL = 1024
GRID_W = 64
D_MIX = 512
NA_HEADS = 8
NA_HEAD_DIM = D_MIX // NA_HEADS
WIN_H = 8
WIN_W = 16
HY_ORDER = 2
HY_BANDS = 8
HY_FAST_DECAY = 0.3
HY_SLOW_DECAY = 1.5
HY_TARGET = 1e-2
D_FF = ((8 * D_MODEL // 3 + 127) // 128) * 128
N_MOD = 9
EPS = 1e-6
NEG_INF = -1e30
LOG2_E = math.log2(math.e)

V7X_VMEM_LIMIT_BYTES = 56 * 1024 * 1024
INPROJ_TILE = 1024
FFN_TILE = 1024
V7X_MXU_WIDTH = 256
_FF_SPLIT = (D_FF // V7X_MXU_WIDTH + 1) // 2 * V7X_MXU_WIDTH
FF_CHUNKS = ((0, _FF_SPLIT), (_FF_SPLIT, D_FF))
HALO = 16
LANE_BLOCK = 128
FFT_CT = 256
FFT_NB = 16
FFT_KB = 16
FFT_K_GROUP = 8
FILT_TILE = 512
FILT_HALF = 64
FILT_PAD = 2 * FILT_HALF


def _cparams(n_axes):
    return pltpu.CompilerParams(
        dimension_semantics=("arbitrary",) * n_axes,
        vmem_limit_bytes=V7X_VMEM_LIMIT_BYTES,
    )


def _rms(x, gain):
    ms = jnp.mean(x * x, axis=-1, keepdims=True)
    return x * lax.rsqrt(ms + EPS) * gain


def _silu(x):
    return x / (1.0 + jnp.exp(-x))


SLABS = LANE_BLOCK // FFT_NB
LANES = 128


def _store_slabs(ref, tile):
    for i1 in range(tile.shape[0] // LANE_BLOCK):
        for t in range(tile.shape[1] // LANES):
            rows = tile[i1 * LANE_BLOCK:(i1 + 1) * LANE_BLOCK, t * LANES:(t + 1) * LANES]
            ref[:, t, i1 * FFT_NB:(i1 + 1) * FFT_NB, :] = rows.reshape(SLABS, FFT_NB, LANES)


def _load_slabs(ref):
    _, tiles, rows, _ = ref.shape
    return jnp.concatenate(
        [jnp.concatenate([ref[:, t, i1 * FFT_NB:(i1 + 1) * FFT_NB, :].reshape(LANE_BLOCK, LANES)
                          for t in range(tiles)], axis=1)
         for i1 in range(rows // FFT_NB)], axis=0)


def _load_strided(ref, lead, start, size):
    tiles = ref.shape[len(lead)]
    return jnp.concatenate([ref[lead + (t, pl.ds(start, size, stride=FFT_NB), slice(None))]
                            for t in range(tiles)], axis=1)


def _store_strided(ref, start, val):
    for t in range(ref.shape[0]):
        ref[t, pl.ds(start, val.shape[0], stride=FFT_NB), :] = val[:, t * LANES:(t + 1) * LANES]


def _ada_kernel(c_ref, w_ref, b_ref, o_ref):
    s = _silu(c_ref[...])
    o_ref[...] = jnp.dot(s, w_ref[...], precision=HIGHEST, preferred_element_type=F32) + b_ref[...]


def _ada(c_all, w_ada, b_ada):
    rows = c_all.shape[0]
    n_out = w_ada.shape[1]
    tn = D_MODEL
    return pl.pallas_call(
        _ada_kernel,
        out_shape=jax.ShapeDtypeStruct((rows, n_out), F32),
        grid=(n_out // tn,),
        in_specs=[
            pl.BlockSpec((rows, D_MODEL), lambda j: (0, 0)),
            pl.BlockSpec((D_MODEL, tn), lambda j: (0, j)),
            pl.BlockSpec((1, tn), lambda j: (0, j)),
        ],
        out_specs=pl.BlockSpec((rows, tn), lambda j: (0, j)),
        compiler_params=_cparams(1),
        name="ada_mod",
    )(c_all, w_ada, b_ada.reshape(1, n_out))


def _ffn_residual(x, mod_ref, mod_base, gain_ref, wg_ref, wu_ref, wd_ref):
    shift = mod_ref[mod_base:mod_base + 1, :]
    scale = mod_ref[mod_base + 1:mod_base + 2, :]
    gate = mod_ref[mod_base + 2:mod_base + 3, :]
    hb = (_rms(x, gain_ref[...]) * (1.0 + scale) + shift).astype(BF16)
    acc = None
    for c0, c1 in FF_CHUNKS:
        g = jnp.dot(hb, wg_ref[:, c0:c1], preferred_element_type=F32)
        u = jnp.dot(hb, wu_ref[:, c0:c1], preferred_element_type=F32)
        a = (_silu(g) * u).astype(BF16)
        d = jnp.dot(a, wd_ref[c0:c1, :], preferred_element_type=F32)
        acc = d if acc is None else acc + d
    return x + 0.5 * gate * acc


def _ffn1_kernel(x_ref, mod_ref, gain_ref, wg_ref, wu_ref, wd_ref, o_ref):
    o_ref[...] = _ffn_residual(x_ref[...], mod_ref, 0, gain_ref, wg_ref, wu_ref, wd_ref)


def _mix_ffn2_kernel(x_ref, an_ref, hz_ref, mod_ref, hy_gain_ref, wa_ref, wh_ref,
                     gain_ref, wg_ref, wu_ref, wd_ref, fn_ref, o_ref):
    hn = _rms(_load_slabs(hz_ref), hy_gain_ref[...]).astype(BF16)
    mixed = (jnp.dot(an_ref[...], wa_ref[...], preferred_element_type=F32)
             + jnp.dot(hn, wh_ref[...], preferred_element_type=F32))
    x = x_ref[...] + mod_ref[5:6, :] * mixed
    y = _ffn_residual(x, mod_ref, 6, gain_ref, wg_ref, wu_ref, wd_ref)
    o_ref[...] = _rms(y, fn_ref[...])


def _resident(shape):
    return pl.BlockSpec(shape, lambda i: (0, 0), pipeline_mode=pl.Buffered(1))


def _token_spec(tm, width):
    return pl.BlockSpec((tm, width), lambda i: (i, 0))


def _mod_spec(tm, seq_len):
    return pl.BlockSpec((None, N_MOD, D_MODEL), lambda i: (i * tm // seq_len, 0, 0))


def _ffn1(x2d, mod, gain, wg, wu, wd, seq_len):
    t = x2d.shape[0]
    tm = FFN_TILE
    return pl.pallas_call(
        _ffn1_kernel,
        out_shape=jax.ShapeDtypeStruct((t, D_MODEL), F32),
        grid=(t // tm,),
        in_specs=[
            _token_spec(tm, D_MODEL), _mod_spec(tm, seq_len), _resident((1, D_MODEL)),
            _resident((D_MODEL, D_FF)), _resident((D_MODEL, D_FF)), _resident((D_FF, D_MODEL)),
        ],
        out_specs=_token_spec(tm, D_MODEL),
        compiler_params=_cparams(1),
        name="ffn1",
    )(x2d, mod, gain.reshape(1, D_MODEL), wg, wu, wd)


def _mix_ffn2(x2d, attn_n, hz, mod, hy_gain, w_attn, w_hy, gain, wg, wu, wd, final_gain, seq_len):
    t = x2d.shape[0]
    tm = FFN_TILE
    tiles_per_seq = seq_len // tm
    return pl.pallas_call(
        _mix_ffn2_kernel,
        out_shape=jax.ShapeDtypeStruct((t, D_MODEL), F32),
        grid=(t // tm,),
        in_specs=[
            _token_spec(tm, D_MODEL),
            _token_spec(tm, D_MIX),
            pl.BlockSpec((None, SLABS, D_MIX // LANES, tm // LANE_BLOCK * FFT_NB, LANES),
                         lambda i: (i // tiles_per_seq, 0, 0, i % tiles_per_seq, 0)),
            _mod_spec(tm, seq_len),
            _resident((1, D_MIX)), _resident((D_MIX, D_MODEL)), _resident((D_MIX, D_MODEL)),
            _resident((1, D_MODEL)),
            _resident((D_MODEL, D_FF)), _resident((D_MODEL, D_FF)), _resident((D_FF, D_MODEL)),
            _resident((1, D_MODEL)),
        ],
        out_specs=_token_spec(tm, D_MODEL),
        compiler_params=_cparams(1),
        name="mix_ffn2",
    )(x2d, attn_n, hz, mod, hy_gain.reshape(1, D_MIX), w_attn, w_hy, gain.reshape(1, D_MODEL),
      wg, wu, wd, final_gain.reshape(1, D_MODEL))


def _inproj_kernel(x_ref, xp_ref, xn_ref, mod_ref, gain_ref, wqkv_ref, why_ref, cw_ref, cb_ref,
                   q_ref, k_ref, v_ref, hv_ref, hx1_ref, hx2_ref, ext_ref, u_ref,
                   *, tiles_per_seq, tm):
    pos = pl.program_id(0) % tiles_per_seq
    gain = gain_ref[...]
    shift = mod_ref[3:4, :]
    scale = 1.0 + mod_ref[4:5, :]

    def normed(x):
        return _rms(x, gain) * scale + shift

    hb = normed(x_ref[...]).astype(BF16)
    qkv = jnp.dot(hb, wqkv_ref[...], preferred_element_type=F32)
    q_ref[...] = (qkv[:, :D_MIX] * (NA_HEAD_DIM ** -0.5 * LOG2_E)).astype(BF16)
    k_ref[...] = qkv[:, D_MIX:2 * D_MIX].astype(BF16)
    v_ref[...] = qkv[:, 2 * D_MIX:].astype(BF16)

    has_prev = jnp.where(pos != 0, 1.0, 0.0)
    has_next = jnp.where(pos != tiles_per_seq - 1, 1.0, 0.0)
    ext_ref[0:HALO, :] = (normed(xp_ref[...]) * has_prev).astype(BF16)
    ext_ref[HALO:HALO + tm, :] = hb
    ext_ref[HALO + tm:, :] = (normed(xn_ref[...]) * has_next).astype(BF16)
    u_ref[...] = jnp.dot(ext_ref[...], why_ref[...], preferred_element_type=F32)
    conv = (cb_ref[...]
            + u_ref[pl.ds(HALO - 1, tm), :] * cw_ref[0:1, :]
            + u_ref[pl.ds(HALO, tm), :] * cw_ref[1:2, :]
            + u_ref[pl.ds(HALO + 1, tm), :] * cw_ref[2:3, :])
    for part, ref in enumerate((hv_ref, hx1_ref, hx2_ref)):
        _store_slabs(ref, conv[:, part * D_MIX:(part + 1) * D_MIX])


def _inproj(x2d, mod, gain, w_qkv, w_hy, conv_w, conv_b, seq_len):
    t = x2d.shape[0]
    tm = INPROJ_TILE
    n_halo_blocks = t // HALO
    per_tile = tm // HALO
    tiles_per_seq = seq_len // tm
    tile_out = jax.ShapeDtypeStruct((t, D_MIX), BF16)
    tile_spec = pl.BlockSpec((tm, D_MIX), lambda i: (i, 0))
    slab_rows = tm // LANE_BLOCK * FFT_NB
    slab_out = jax.ShapeDtypeStruct((t // seq_len, SLABS, D_MIX // LANES, seq_len // SLABS, LANES), F32)
    slab_spec = pl.BlockSpec((None, SLABS, D_MIX // LANES, slab_rows, LANES),
                             lambda i: (i // tiles_per_seq, 0, 0, i % tiles_per_seq, 0))
    return pl.pallas_call(
        functools.partial(_inproj_kernel, tiles_per_seq=tiles_per_seq, tm=tm),
        out_shape=[tile_out] * 3 + [slab_out] * 3,
        grid=(t // tm,),
        in_specs=[
            pl.BlockSpec((tm, D_MODEL), lambda i: (i, 0)),
            pl.BlockSpec((HALO, D_MODEL), lambda i: (jnp.maximum(i * per_tile - 1, 0), 0)),
            pl.BlockSpec((HALO, D_MODEL),
                         lambda i: (jnp.minimum((i + 1) * per_tile, n_halo_blocks - 1), 0)),
            pl.BlockSpec((None, N_MOD, D_MODEL), lambda i: (i * tm // seq_len, 0, 0)),
            _resident((1, D_MODEL)),
            _resident((D_MODEL, 3 * D_MIX)),
            _resident((D_MODEL, 3 * D_MIX)),
            _resident((3, 3 * D_MIX)),
            _resident((1, 3 * D_MIX)),
        ],
        out_specs=[tile_spec] * 3 + [slab_spec] * 3,
        scratch_shapes=[
            pltpu.VMEM((tm + 2 * HALO, D_MODEL), BF16),
            pltpu.VMEM((tm + 2 * HALO, 3 * D_MIX), F32),
        ],
        compiler_params=_cparams(1),
        name="inproj",
    )(x2d, x2d, x2d, mod, gain.reshape(1, D_MODEL), w_qkv, w_hy, conv_w,
      conv_b.reshape(1, 3 * D_MIX))


ROW_TOKENS = GRID_W
GROUP_ROWS = 2 * WIN_H
GROUP_TOKENS = GROUP_ROWS * ROW_TOKENS
WIN_TOKENS = WIN_H * ROW_TOKENS


def _attn_bias_table(rpb):
    n_heads, n_drow, n_dcol = rpb.shape
    period = 2 * GRID_W - 1
    wrapped = jnp.concatenate([rpb[..., WIN_W - 1:], jnp.zeros((n_heads, n_drow, period - n_dcol), F32),
                               rpb[..., :WIN_W - 1]], axis=-1).astype(F32)
    toeplitz = jnp.tile(wrapped, GRID_W)[..., :GRID_W * (period - 1)]
    toeplitz = toeplitz.reshape(n_heads, n_drow, GRID_W, period - 1)[..., :GRID_W]
    qc = np.arange(GRID_W)[:, None]
    kc = np.arange(GRID_W)[None, :]
    win_start = np.clip(qc - WIN_W // 2, 0, GRID_W - WIN_W)
    col_ok = (kc >= win_start) & (kc < win_start + WIN_W)
    masked = jnp.where(col_ok[None, None], toeplitz * LOG2_E, NEG_INF)
    bias = jnp.stack([masked[:, o:o + WIN_H] for o in range(WIN_H)], axis=0)
    bias = jnp.transpose(bias, (0, 1, 3, 2, 4))
    return bias.reshape(WIN_H, NA_HEADS, GRID_W, WIN_TOKENS)


NATTEN_UNROLL = 8
KV_WINDOW_ROWS = GROUP_ROWS + WIN_H


def _kv_window_start(g, rows):
    return jnp.clip(g * GROUP_ROWS - WIN_H // 2, 0, rows - KV_WINDOW_ROWS)


def _attn_kernel(q_ref, k_ref, v_ref, bias_ref, gain_ref, o_ref, acc, *, rows):
    g = pl.program_id(1)
    win_start = _kv_window_start(g, rows)
    first_head = lax.broadcasted_iota(jnp.int32, (ROW_TOKENS, 2 * NA_HEAD_DIM), 1) < NA_HEAD_DIM
    head_pairs = [slice(hp * 2 * NA_HEAD_DIM, (hp + 1) * 2 * NA_HEAD_DIM) for hp in range(NA_HEADS // 2)]

    def row_body(rr, carry):
        r = g * GROUP_ROWS + rr
        start = jnp.clip(r - WIN_H // 2, 0, rows - WIN_H)
        koff = pl.multiple_of((start - win_start) * ROW_TOKENS, ROW_TOKENS)
        row_class = start - r + WIN_H - 1
        qoff = pl.multiple_of(rr * ROW_TOKENS, ROW_TOKENS)
        scores = []
        for hp, lanes in enumerate(head_pairs):
            q2 = q_ref[pl.ds(qoff, ROW_TOKENS), lanes]
            kw = k_ref[pl.ds(koff, WIN_TOKENS), lanes]
            for hh in range(2):
                keep = first_head if hh == 0 else jnp.logical_not(first_head)
                qm = jnp.where(keep, q2, jnp.zeros_like(q2))
                s = lax.dot_general(qm, kw, (((1,), (1,)), ((), ())), preferred_element_type=F32)
                scores.append(s + bias_ref[row_class, 2 * hp + hh])
        probs, denoms = [], []
        for s in scores:
            p = jnp.exp2(s - jnp.max(s, axis=-1, keepdims=True))
            denoms.append(jnp.sum(p, axis=-1, keepdims=True))
            probs.append(p.astype(BF16))
        for hp, lanes in enumerate(head_pairs):
            vw = v_ref[pl.ds(koff, WIN_TOKENS), lanes]
            outs = [jnp.dot(probs[2 * hp + hh], vw, preferred_element_type=F32) / denoms[2 * hp + hh]
                    for hh in range(2)]
            acc[pl.ds(qoff, ROW_TOKENS), lanes] = jnp.where(first_head, outs[0], outs[1])
        return carry

    lax.fori_loop(0, GROUP_ROWS, row_body, 0, unroll=NATTEN_UNROLL)
    o_ref[...] = _rms(acc[...], gain_ref[...]).astype(BF16)


def _attention(q, k, v, bias, gain, batch, seq_len):
    rows = seq_len // GRID_W
    q3 = q.reshape(batch, seq_len, D_MIX)
    k3 = k.reshape(batch, seq_len, D_MIX)
    v3 = v.reshape(batch, seq_len, D_MIX)
    cur = pl.BlockSpec((None, GROUP_TOKENS, D_MIX), lambda b, g: (b, g, 0))
    window = pl.BlockSpec((None, pl.Element(KV_WINDOW_ROWS * ROW_TOKENS), pl.Element(D_MIX)),
                          lambda b, g: (b, _kv_window_start(g, rows) * ROW_TOKENS, 0))
    out = pl.pallas_call(
        functools.partial(_attn_kernel, rows=rows),
        out_shape=jax.ShapeDtypeStruct((batch, seq_len, D_MIX), BF16),
        grid=(batch, rows // GROUP_ROWS),
        in_specs=[
            cur, window, window,
            pl.BlockSpec((WIN_H, NA_HEADS, GRID_W, WIN_TOKENS), lambda b, g: (0, 0, 0, 0),
                         pipeline_mode=pl.Buffered(1)),
            pl.BlockSpec((1, D_MIX), lambda b, g: (0, 0)),
        ],
        out_specs=cur,
        scratch_shapes=[pltpu.VMEM((GROUP_TOKENS, D_MIX), F32)],
        compiler_params=_cparams(2),
        name="natten",
    )(q3, k3, v3, bias, gain.reshape(1, D_MIX))
    return out.reshape(batch * seq_len, D_MIX)


def _split_hi_lo(x):
    hi = x.astype(ml_dtypes.bfloat16)
    lo = (x - hi.astype(np.float64)).astype(ml_dtypes.bfloat16)
    return hi, lo


def _stack_hi_lo(m):
    hi, lo = _split_hi_lo(m)
    return np.concatenate([hi, lo], axis=-2)


def _embed(re, im):
    return np.concatenate([np.concatenate([re, -im], axis=-1),
                           np.concatenate([im, re], axis=-1)], axis=-2)


@functools.lru_cache(maxsize=None)
def _fft_tables(seq_len):
    n = 2 * seq_len
    n2 = LANE_BLOCK
    n1 = n // n2
    i2 = np.arange(n2)[:, None, None]
    k1 = np.arange(n1)[None, :, None]
    i1 = np.arange(n1)[None, None, :]
    ang = -2.0 * np.pi * ((k1 * (n2 * i1 + i2)) % n) / n
    gr_full, gi_full = np.cos(ang), np.sin(ang)
    g_real = _stack_hi_lo(np.concatenate([gr_full, gi_full], axis=1))
    gr, gi = gr_full[..., :n1 // 2], gi_full[..., :n1 // 2]
    g_fwd = _stack_hi_lo(_embed(gr, gi))
    g_inv = _stack_hi_lo(_embed(np.swapaxes(gr, 1, 2) / n, -np.swapaxes(gi, 1, 2) / n))
    jk = np.outer(np.arange(n2), np.arange(n2))
    ang2 = -2.0 * np.pi * (jk % n2) / n2
    fr, fi = np.cos(ang2), np.sin(ang2)
    f_fwd = _stack_hi_lo(_embed(fr, fi))
    f_inv = _stack_hi_lo(_embed(fr, -fi))
    return n1, g_fwd, g_real, g_inv, f_fwd, f_inv


def _dft3(m_hl, x, m):
    x_hi = x.astype(BF16)
    x_lo = (x - x_hi.astype(F32)).astype(BF16)
    t = jnp.dot(m_hl, x_hi, preferred_element_type=F32)
    return t[:m] + t[m:] + jnp.dot(m_hl[:m], x_lo, preferred_element_type=F32)


def _stage_a_forward(x_ref, g_ref, a_ref, *, n1):
    for i in range(FFT_NB):
        x = jnp.concatenate([_load_strided(x_ref, (0,), i, n1 // 2),
                             _load_strided(x_ref, (1,), i, n1 // 2)], axis=0)
        _store_strided(a_ref, i, _dft3(g_ref[i], x, 2 * n1))


def _stage_a_inverse(d_ref, gi_ref, y_ref, *, n1):
    for i in range(FFT_NB):
        _store_strided(y_ref, i, _dft3(gi_ref[i], _load_strided(d_ref, (), i, 2 * n1), n1))


def _k1_kernel(x_ref, g_ref, a_ref, *, n1):
    _stage_a_forward(x_ref, g_ref, a_ref, n1=n1)


FFT_TILES = FFT_CT // LANES


def _seq_spec(n1):
    return pl.BlockSpec((None, 2, None, FFT_TILES, n1 // 2 * FFT_NB, LANES),
                        lambda c, j, p: (p, 0, j, c, 0, 0))


def _spec_spec(n1):
    return pl.BlockSpec((None, None, FFT_TILES, 2 * n1 * FFT_NB, LANES), lambda c, j, p: (p, j, c, 0, 0))


def _fft_stage_a(x6, g_fwd, n1):
    pairs, _, slabs, tiles, _, _ = x6.shape
    return pl.pallas_call(
        functools.partial(_k1_kernel, n1=n1),
        out_shape=jax.ShapeDtypeStruct((pairs, slabs, tiles, 2 * n1 * FFT_NB, LANES), F32),
        grid=(tiles // FFT_TILES, slabs, pairs),
        in_specs=[
            _seq_spec(n1),
            pl.BlockSpec((FFT_NB, 4 * n1, n1), lambda c, j, p: (j, 0, 0)),
        ],
        out_specs=_spec_spec(n1),
        compiler_params=_cparams(3),
        name="hy_stage_a",
    )(x6, g_fwd)


def _load_low_index(ref, part, kk):
    return jnp.concatenate([ref[:, t, part, kk].reshape(LANE_BLOCK, LANES) for t in range(ref.shape[1])],
                           axis=1)


def _store_low_index(ref, part, kk, val):
    for t in range(ref.shape[1]):
        ref[:, t, part, kk] = val[:, t * LANES:(t + 1) * LANES].reshape(SLABS, FFT_NB, LANES)


def _k2_kernel(a_ref, kf_ref, f_ref, fi_ref, d_ref, *, kb):
    n2 = LANE_BLOCK
    f_hl = f_ref[...]
    fi_hl = fi_ref[...]

    def body(group, carry):
        ks = [group * FFT_K_GROUP + u for u in range(FFT_K_GROUP)]
        spectra = [_dft3(f_hl, jnp.concatenate([_load_low_index(a_ref, 0, kk),
                                                _load_low_index(a_ref, 1, kk)], axis=0), 2 * n2)
                   for kk in ks]
        products = []
        for kk, c in zip(ks, spectra):
            cr, ci = c[:n2], c[n2:]
            kr, ki = kf_ref[0, kk], kf_ref[1, kk]
            products.append(jnp.concatenate([cr * kr - ci * ki, cr * ki + ci * kr], axis=0))
        for kk, y in zip(ks, products):
            d = _dft3(fi_hl, y, 2 * n2)
            _store_low_index(d_ref, 0, kk, d[:n2])
            _store_low_index(d_ref, 1, kk, d[n2:])
        return carry

    lax.fori_loop(0, kb // FFT_K_GROUP, body, 0)


def _fft_stage_c(a4, kf, f_fwd, f_inv, order, n1):
    pairs, slabs, tiles, _, _ = a4.shape
    n2 = LANE_BLOCK
    kb = FFT_KB
    ch_blocks = tiles // FFT_TILES
    a7 = a4.reshape(pairs, slabs, tiles, 2, n1, FFT_NB, LANES)
    spec = pl.BlockSpec((None, slabs, FFT_TILES, 2, kb, FFT_NB, LANES),
                        lambda c, k, p: (p, 0, c, 0, k, 0, 0))
    d7 = pl.pallas_call(
        functools.partial(_k2_kernel, kb=kb),
        out_shape=jax.ShapeDtypeStruct(a7.shape, F32),
        grid=(ch_blocks, n1 // kb, pairs),
        in_specs=[
            spec,
            pl.BlockSpec((2, kb, n2, FFT_CT), lambda c, k, p: (0, k, 0, order * ch_blocks + c)),
            pl.BlockSpec((4 * n2, 2 * n2), lambda c, k, p: (0, 0)),
            pl.BlockSpec((4 * n2, 2 * n2), lambda c, k, p: (0, 0)),
        ],
        out_specs=spec,
        compiler_params=_cparams(3),
        name="hy_stage_c",
    )(a7, kf, f_fwd, f_inv)
    return d7.reshape(a4.shape)


def _k3_kernel(d_ref, gi_ref, z_ref, x_ref, skip_ref, *rest, n1, forward):
    if forward:
        g_ref, o_ref, a_ref, y_ref = rest
    else:
        o_ref, y_ref = rest
    _stage_a_inverse(d_ref, gi_ref, y_ref, n1=n1)
    rows = n1 // 2 * FFT_NB
    for part in range(2):
        for t in range(FFT_TILES):
            conv = y_ref[t, part * rows:(part + 1) * rows, :]
            o_ref[part, t] = x_ref[part, t] * (conv + skip_ref[t] * z_ref[part, t])
    if forward:
        _stage_a_forward(o_ref, g_ref, a_ref, n1=n1)


def _fft_stage_a_inverse(d4, g_inv, z6, x6, skip_row, n1, g_fwd=None):
    pairs, slabs, tiles, _, _ = d4.shape
    forward = g_fwd is not None
    in_specs = [
        _spec_spec(n1),
        pl.BlockSpec((FFT_NB, 2 * n1, 2 * n1), lambda c, j, p: (j, 0, 0)),
        _seq_spec(n1),
        _seq_spec(n1),
        pl.BlockSpec((FFT_TILES, 1, LANES), lambda c, j, p: (c, 0, 0)),
    ]
    args = [d4, g_inv, z6, x6, skip_row]
    out_shape = [jax.ShapeDtypeStruct(z6.shape, F32)]
    out_specs = [_seq_spec(n1)]
    if forward:
        in_specs.append(pl.BlockSpec((FFT_NB, 4 * n1, n1), lambda c, j, p: (j, 0, 0)))
        args.append(g_fwd)
        out_shape.append(jax.ShapeDtypeStruct(d4.shape, F32))
        out_specs.append(_spec_spec(n1))
    return pl.pallas_call(
        functools.partial(_k3_kernel, n1=n1, forward=forward),
        out_shape=out_shape,
        grid=(tiles // FFT_TILES, slabs, pairs),
        in_specs=in_specs,
        out_specs=out_specs,
        scratch_shapes=[pltpu.VMEM((FFT_TILES, n1 * FFT_NB, LANES), F32)],
        compiler_params=_cparams(3),
        name="hy_stage_a_inv_fwd" if forward else "hy_stage_a_inv",
    )(*args)


def _filt_kernel(z_ref, w1_ref, b1_ref, w2_ref, b2_ref, w3_ref, b3_ref, wo_ref, freq_ref,
                 delta_ref, h_ref, l1_ref, *, tl):
    i = pl.program_id(0)
    freq = freq_ref[...]

    def dot(a, b):
        return jnp.dot(a, b, precision=HIGHEST, preferred_element_type=F32)

    z = z_ref[...]
    h = jnp.sin(freq * (dot(z, w1_ref[...]) + b1_ref[...]))
    h = jnp.sin(freq * (dot(h, w2_ref[...]) + b2_ref[...]))
    h = jnp.sin(freq * (dot(h, w3_ref[...]) + b3_ref[...]))
    hf = dot(h, wo_ref[0]) * jnp.exp(-z[:, 0:1] * delta_ref[...])
    hb = dot(h, wo_ref[1]) * jnp.exp(-z[:, FILT_HALF:FILT_HALF + 1] * delta_ref[...])
    row = i * tl + lax.broadcasted_iota(jnp.int32, (tl, 1), 0)
    hb = jnp.where(row == 0, 0.0, hb)
    _store_slabs(h_ref.at[0], hf)
    _store_slabs(h_ref.at[1], hb)

    @pl.when(i == 0)
    def _():
        l1_ref[...] = jnp.zeros_like(l1_ref)

    l1_ref[...] += (jnp.sum(jnp.abs(hf), axis=0, keepdims=True)
                    + jnp.sum(jnp.abs(hb), axis=0, keepdims=True))


def _pad_to(x, shape):
    return jnp.pad(x, [(0, s - d) for d, s in zip(x.shape, shape)])


def _filter_taps(seq_len, w1, b1, w2, b2, w3, b3, wo, freq):
    t = jnp.linspace(0.0, 1.0, seq_len, dtype=F32)[:, None]
    w = 2.0 * math.pi * jnp.arange(seq_len, dtype=F32)[:, None] / seq_len
    f = jnp.linspace(1e-4, HY_BANDS - 1, HY_BANDS, dtype=F32)[None, :]
    z = _pad_to(jnp.concatenate([t, jnp.cos(f * w), -jnp.sin(f * w)], axis=-1), (seq_len, FILT_HALF))
    z = jnp.concatenate([z, z[::-1]], axis=1)
    deltas = jnp.abs(jnp.linspace(math.log(HY_TARGET) / HY_FAST_DECAY,
                                  math.log(HY_TARGET) / HY_SLOW_DECAY, D_MIX, dtype=F32))
    n_cols = HY_ORDER * D_MIX

    def both(m):
        m = _pad_to(m, (FILT_HALF, FILT_HALF))
        zero = jnp.zeros_like(m)
        return jnp.concatenate([jnp.concatenate([m, zero], axis=1),
                                jnp.concatenate([zero, m], axis=1)], axis=0)

    row = lambda v: jnp.tile(_pad_to(v.reshape(1, -1), (1, FILT_HALF)), (1, 2))
    wo_p = _pad_to(wo, (FILT_HALF, 2 * n_cols))
    zero = jnp.zeros((FILT_HALF, n_cols), F32)
    wo_dirs = jnp.stack([jnp.concatenate([wo_p[:, :n_cols], zero], axis=0),
                         jnp.concatenate([zero, wo_p[:, n_cols:]], axis=0)], axis=0)
    pad2 = (FILT_PAD, FILT_PAD)
    tl = FILT_TILE
    const = lambda shape: pl.BlockSpec(shape, lambda i: (0,) * len(shape))
    return pl.pallas_call(
        functools.partial(_filt_kernel, tl=tl),
        out_shape=[jax.ShapeDtypeStruct((2, SLABS, n_cols // LANES, seq_len // SLABS, LANES), F32),
                   jax.ShapeDtypeStruct((1, n_cols), F32)],
        grid=(seq_len // tl,),
        in_specs=[
            pl.BlockSpec((tl, FILT_PAD), lambda i: (i, 0)),
            const(pad2), const((1, FILT_PAD)), const(pad2), const((1, FILT_PAD)),
            const(pad2), const((1, FILT_PAD)), const((2, FILT_PAD, n_cols)), const((1, FILT_PAD)),
            const((1, n_cols)),
        ],
        out_specs=[pl.BlockSpec((2, SLABS, n_cols // LANES, tl // LANE_BLOCK * FFT_NB, LANES),
                                lambda i: (0, 0, 0, i, 0)),
                   const((1, n_cols))],
        compiler_params=_cparams(1),
        name="hy_filter_taps",
    )(z, both(w1), row(b1), both(w2), row(b2), both(w3), row(b3), wo_dirs, row(freq),
      jnp.tile(deltas, HY_ORDER).reshape(1, n_cols))


def _k2f_kernel(a_ref, l1_ref, f_ref, kf_ref, *, kb):
    n2 = LANE_BLOCK
    f_hl = f_ref[...]
    inv_l1 = 1.0 / l1_ref[...]

    def body(group, carry):
        ks = [group * FFT_K_GROUP + u for u in range(FFT_K_GROUP)]
        spectra = [_dft3(f_hl, jnp.concatenate([_load_low_index(a_ref, 0, kk),
                                                _load_low_index(a_ref, 1, kk)], axis=0), 2 * n2)
                   for kk in ks]
        for kk, c in zip(ks, spectra):
            kf_ref[0, kk] = c[:n2] * inv_l1
            kf_ref[1, kk] = c[n2:] * inv_l1
        return carry

    lax.fori_loop(0, kb // FFT_K_GROUP, body, 0)


def _filter_spectrum(a4, l1, f_fwd, n1):
    _, slabs, tiles, _, _ = a4.shape
    cols = tiles * LANES
    n2 = LANE_BLOCK
    kb, ct = FFT_KB, FFT_CT
    a6 = a4.reshape(slabs, tiles, 2, n1, FFT_NB, LANES)
    return pl.pallas_call(
        functools.partial(_k2f_kernel, kb=kb),
        out_shape=jax.ShapeDtypeStruct((2, n1, n2, cols), F32),
        grid=(cols // ct, n1 // kb),
        in_specs=[
            pl.BlockSpec((slabs, FFT_TILES, 2, kb, FFT_NB, LANES), lambda c, k: (0, c, 0, k, 0, 0)),
            pl.BlockSpec((1, ct), lambda c, k: (0, c)),
            pl.BlockSpec((4 * n2, 2 * n2), lambda c, k: (0, 0)),
        ],
        out_specs=pl.BlockSpec((2, kb, n2, ct), lambda c, k: (0, k, 0, c)),
        compiler_params=_cparams(2),
        name="hy_filter_spectrum",
    )(a6, l1, f_fwd)


def _hyena(hv, hx1, hx2, skip, filt_params, seq_len):
    n1, *tables = _fft_tables(seq_len)
    g_fwd, g_real, g_inv, f_fwd, f_inv = (jnp.asarray(m) for m in tables)
    taps, l1 = _filter_taps(seq_len, *filt_params)
    kf = _filter_spectrum(_fft_stage_a(taps[None], g_real, n1), l1, f_fwd, n1)
    as_pairs = lambda a: a.reshape((a.shape[0] // 2, 2) + a.shape[1:])
    z0, x1, x2 = as_pairs(hv), as_pairs(hx1), as_pairs(hx2)
    d = _fft_stage_c(_fft_stage_a(z0, g_fwd, n1), kf, f_fwd, f_inv, 0, n1)
    skip_rows = skip.reshape(HY_ORDER, D_MIX // LANES, 1, LANES)
    z1, a = _fft_stage_a_inverse(d, g_inv, z0, x1, skip_rows[0], n1, g_fwd=g_fwd)
    d = _fft_stage_c(a, kf, f_fwd, f_inv, 1, n1)
    (z2,) = _fft_stage_a_inverse(d, g_inv, z1, x2, skip_rows[1], n1)
    return z2.reshape(hv.shape)


def _trunk(x, mod, p, final_norm):
    batch, seq_len, _ = x.shape
    x2d = x.reshape(batch * seq_len, D_MODEL)
    x2d = _ffn1(x2d, mod, p["ffn1_norm"], p["ffn1_w_gate"], p["ffn1_w_up"], p["ffn1_w_down"], seq_len)
    q, k, v, hv, hx1, hx2 = _inproj(x2d, mod, p["mix_norm"], p["w_qkv"], p["w_hy"],
                                    p["hy_conv_w"], p["hy_conv_b"], seq_len)
    attn_n = _attention(q, k, v, p["attn_bias"], p["attn_out_norm"], batch, seq_len)
    hz = _hyena(hv, hx1, hx2, p["hy_skip"], p["hy_filter"], seq_len)
    y = _mix_ffn2(x2d, attn_n, hz, mod, p["hy_out_norm"], p["w_out_attn"], p["w_out_hy"],
                  p["ffn2_norm"], p["ffn2_w_gate"], p["ffn2_w_up"], p["ffn2_w_down"], final_norm, seq_len)
    return y.reshape(batch, seq_len, D_MODEL)


def kernel(x_prompt, x_sample, c_prompt, c_sample, w_ada, b_ada, ffn1_norm, ffn1_w_gate, ffn1_w_up,
           ffn1_w_down, mix_norm, w_in, na_rpb, hy_conv_w, hy_conv_b, hy_w1, hy_b1, hy_w2, hy_b2,
           hy_w3, hy_b3, hy_wo, hy_sin_freq, hy_skip, attn_out_norm, hy_out_norm, w_out, ffn2_norm,
           ffn2_w_gate, ffn2_w_up, ffn2_w_down, final_norm):
    assert w_ada.shape[0] == 1, "single-layer encoder"
    n_prompt = c_prompt.shape[0]
    mod_all = _ada(jnp.concatenate([c_prompt, c_sample], axis=0), w_ada[0], b_ada[0])
    mod_all = mod_all.reshape(-1, N_MOD, D_MODEL)
    bf = lambda w: w[0].astype(BF16)
    p = {
        "ffn1_norm": ffn1_norm[0], "ffn1_w_gate": bf(ffn1_w_gate), "ffn1_w_up": bf(ffn1_w_up),
        "ffn1_w_down": bf(ffn1_w_down),
        "mix_norm": mix_norm[0],
        "w_qkv": w_in[0, :, :3 * D_MIX].astype(BF16), "w_hy": w_in[0, :, 3 * D_MIX:].astype(BF16),
        "attn_bias": _attn_bias_table(na_rpb[0]),
        "hy_conv_w": hy_conv_w[0], "hy_conv_b": hy_conv_b[0],
        "hy_filter": (hy_w1[0], hy_b1[0], hy_w2[0], hy_b2[0], hy_w3[0], hy_b3[0], hy_wo[0],
                      hy_sin_freq[0]),
        "hy_skip": hy_skip[0],
        "attn_out_norm": attn_out_norm[0], "hy_out_norm": hy_out_norm[0],
        "w_out_attn": w_out[0, :D_MIX].astype(BF16), "w_out_hy": w_out[0, D_MIX:].astype(BF16),
        "ffn2_norm": ffn2_norm[0], "ffn2_w_gate": bf(ffn2_w_gate), "ffn2_w_up": bf(ffn2_w_up),
        "ffn2_w_down": bf(ffn2_w_down),
    }
    y_prompt = _trunk(x_prompt, mod_all[:n_prompt], p, final_norm)
    y_sample = _trunk(x_sample, mod_all[n_prompt:], p, final_norm)
    return (y_prompt, y_sample)
```

```python
import functools
import math

import ml_dtypes
import numpy as np
import jax
import jax.numpy as jnp
from jax import lax
from jax.experimental import pallas as pl
from jax.experimental.pallas import tpu as pltpu

F32 = jnp.float32
BF16 = jnp.bfloat16
HIGHEST = lax.Precision.HIGHEST

D_MODEL = 1024
GRID_W = 64
D_MIX = 512
NA_HEADS = 8
NA_HEAD_DIM = D_MIX // NA_HEADS
WIN_H = 8
WIN_W = 16
HY_ORDER = 2
HY_BANDS = 8
HY_FAST_DECAY = 0.3
HY_SLOW_DECAY = 1.5
HY_TARGET = 1e-2
D_FF = ((8 * D_MODEL // 3 + 127) // 128) * 128
N_MOD = 9
EPS = 1e-6
NEG_INF = -1e30
LOG2_E = math.log2(math.e)

V7X_VMEM_LIMIT_BYTES = 56 * 1024 * 1024
INPROJ_TILE = 1024
FFN_TILE = 1024
V7X_MXU_WIDTH = 256
_FF_SPLIT = (D_FF // V7X_MXU_WIDTH + 1) // 2 * V7X_MXU_WIDTH
FF_CHUNKS = ((0, _FF_SPLIT), (_FF_SPLIT, D_FF))
HALO = 16
LANE_BLOCK = 128
FFT_CT = 256
FFT_NB = 16
FFT_KB = 16
FFT_K_GROUP = 8
FILT_TILE = 512
FILT_HALF = 64
FILT_PAD = 2 * FILT_HALF


def _cparams(n_axes):
    return pltpu.CompilerParams(
        dimension_semantics=("arbitrary",) * n_axes,
        vmem_limit_bytes=V7X_VMEM_LIMIT_BYTES,
    )


def _rms(x, gain):
    ms = jnp.mean(x * x, axis=-1, keepdims=True)
    return x * lax.rsqrt(ms + EPS) * gain


def _silu(x):
    return x / (1.0 + jnp.exp(-x))


SLABS = LANE_BLOCK // FFT_NB
LANES = 128


def _store_slabs(ref, tile):
    for i1 in range(tile.shape[0] // LANE_BLOCK):
        for t in range(tile.shape[1] // LANES):
            rows = tile[i1 * LANE_BLOCK:(i1 + 1) * LANE_BLOCK, t * LANES:(t + 1) * LANES]
            ref[:, t, i1 * FFT_NB:(i1 + 1) * FFT_NB, :] = rows.reshape(SLABS, FFT_NB, LANES)


def _load_slabs(ref):
    _, tiles, rows, _ = ref.shape
    return jnp.concatenate(
        [jnp.concatenate([ref[:, t, i1 * FFT_NB:(i1 + 1) * FFT_NB, :].reshape(LANE_BLOCK, LANES)
                          for t in range(tiles)], axis=1)
         for i1 in range(rows // FFT_NB)], axis=0)


def _load_strided(ref, lead, start, size):
    tiles = ref.shape[len(lead)]
    return jnp.concatenate([ref[lead + (t, pl.ds(start, size, stride=FFT_NB), slice(None))]
                            for t in range(tiles)], axis=1)


def _store_strided(ref, start, val):
    for t in range(ref.shape[0]):
        ref[t, pl.ds(start, val.shape[0], stride=FFT_NB), :] = val[:, t * LANES:(t + 1) * LANES]


def _ada_kernel(c_ref, w_ref, b_ref, o_ref):
    s = _silu(c_ref[...])
    o_ref[...] = jnp.dot(s, w_ref[...], precision=HIGHEST, preferred_element_type=F32) + b_ref[...]


def _ada(c_all, w_ada, b_ada):
    rows = c_all.shape[0]
    n_out = w_ada.shape[1]
    tn = D_MODEL
    return pl.pallas_call(
        _ada_kernel,
        out_shape=jax.ShapeDtypeStruct((rows, n_out), F32),
        grid=(n_out // tn,),
        in_specs=[
            pl.BlockSpec((rows, D_MODEL), lambda j: (0, 0)),
            pl.BlockSpec((D_MODEL, tn), lambda j: (0, j)),
            pl.BlockSpec((1, tn), lambda j: (0, j)),
        ],
        out_specs=pl.BlockSpec((rows, tn), lambda j: (0, j)),
        compiler_params=_cparams(1),
        name="ada_mod",
    )(c_all, w_ada, b_ada.reshape(1, n_out))


def _ffn_residual(x, mod_ref, mod_base, gain_ref, wg_ref, wu_ref, wd_ref):
    shift = mod_ref[mod_base:mod_base + 1, :]
    scale = mod_ref[mod_base + 1:mod_base + 2, :]
    gate = mod_ref[mod_base + 2:mod_base + 3, :]
    hb = (_rms(x, gain_ref[...]) * (1.0 + scale) + shift).astype(BF16)
    acc = None
    for c0, c1 in FF_CHUNKS:
        g = jnp.dot(hb, wg_ref[:, c0:c1], preferred_element_type=F32)
        u = jnp.dot(hb, wu_ref[:, c0:c1], preferred_element_type=F32)
        a = (_silu(g) * u).astype(BF16)
        d = jnp.dot(a, wd_ref[c0:c1, :], preferred_element_type=F32)
        acc = d if acc is None else acc + d
    return x + 0.5 * gate * acc


def _ffn1_kernel(x_ref, mod_ref, gain_ref, wg_ref, wu_ref, wd_ref, o_ref):
    o_ref[...] = _ffn_residual(x_ref[...], mod_ref, 0, gain_ref, wg_ref, wu_ref, wd_ref)


def _mix_ffn2_kernel(x_ref, an_ref, hz_ref, mod_ref, hy_gain_ref, wa_ref, wh_ref,
                     gain_ref, wg_ref, wu_ref, wd_ref, fn_ref, o_ref):
    hn = _rms(_load_slabs(hz_ref), hy_gain_ref[...]).astype(BF16)
    mixed = (jnp.dot(an_ref[...], wa_ref[...], preferred_element_type=F32)
             + jnp.dot(hn, wh_ref[...], preferred_element_type=F32))
    x = x_ref[...] + mod_ref[5:6, :] * mixed
    y = _ffn_residual(x, mod_ref, 6, gain_ref, wg_ref, wu_ref, wd_ref)
    o_ref[...] = _rms(y, fn_ref[...])


def _resident(shape):
    return pl.BlockSpec(shape, lambda i: (0, 0), pipeline_mode=pl.Buffered(1))


def _token_spec(tm, width):
    return pl.BlockSpec((tm, width), lambda i: (i, 0))


def _mod_spec(tm, seq_len):
    return pl.BlockSpec((None, N_MOD, D_MODEL), lambda i: (i * tm // seq_len, 0, 0))


def _ffn1(x2d, mod, gain, wg, wu, wd, seq_len):
    t = x2d.shape[0]
    tm = FFN_TILE
    return pl.pallas_call(
        _ffn1_kernel,
        out_shape=jax.ShapeDtypeStruct((t, D_MODEL), F32),
        grid=(t // tm,),
        in_specs=[
            _token_spec(tm, D_MODEL), _mod_spec(tm, seq_len), _resident((1, D_MODEL)),
            _resident((D_MODEL, D_FF)), _resident((D_MODEL, D_FF)), _resident((D_FF, D_MODEL)),
        ],
        out_specs=_token_spec(tm, D_MODEL),
        compiler_params=_cparams(1),
        name="ffn1",
    )(x2d, mod, gain.reshape(1, D_MODEL), wg, wu, wd)


def _mix_ffn2(x2d, attn_n, hz, mod, hy_gain, w_attn, w_hy, gain, wg, wu, wd, final_gain, seq_len):
    t = x2d.shape[0]
    tm = FFN_TILE
    tiles_per_seq = seq_len // tm
    return pl.pallas_call(
        _mix_ffn2_kernel,
        out_shape=jax.ShapeDtypeStruct((t, D_MODEL), F32),
        grid=(t // tm,),
        in_specs=[
            _token_spec(tm, D_MODEL),
            _token_spec(tm, D_MIX),
            pl.BlockSpec((None, SLABS, D_MIX // LANES, tm // LANE_BLOCK * FFT_NB, LANES),
                         lambda i: (i // tiles_per_seq, 0, 0, i % tiles_per_seq, 0)),
            _mod_spec(tm, seq_len),
            _resident((1, D_MIX)), _resident((D_MIX, D_MODEL)), _resident((D_MIX, D_MODEL)),
            _resident((1, D_MODEL)),
            _resident((D_MODEL, D_FF)), _resident((D_MODEL, D_FF)), _resident((D_FF, D_MODEL)),
            _resident((1, D_MODEL)),
        ],
        out_specs=_token_spec(tm, D_MODEL),
        compiler_params=_cparams(1),
        name="mix_ffn2",
    )(x2d, attn_n, hz, mod, hy_gain.reshape(1, D_MIX), w_attn, w_hy, gain.reshape(1, D_MODEL),
      wg, wu, wd, final_gain.reshape(1, D_MODEL))


def _inproj_kernel(x_ref, xp_ref, xn_ref, mod_ref, gain_ref, wqkv_ref, why_ref, cw_ref, cb_ref,
                   q_ref, k_ref, v_ref, hv_ref, hx1_ref, hx2_ref, ext_ref, u_ref,
                   *, tiles_per_seq, tm):
    pos = pl.program_id(0) % tiles_per_seq
    gain = gain_ref[...]
    shift = mod_ref[3:4, :]
    scale = 1.0 + mod_ref[4:5, :]

    def normed(x):
        return _rms(x, gain) * scale + shift

    hb = normed(x_ref[...]).astype(BF16)
    qkv = jnp.dot(hb, wqkv_ref[...], preferred_element_type=F32)
    q_ref[...] = (qkv[:, :D_MIX] * (NA_HEAD_DIM ** -0.5 * LOG2_E)).astype(BF16)
    k_ref[...] = qkv[:, D_MIX:2 * D_MIX].astype(BF16)
    v_ref[...] = qkv[:, 2 * D_MIX:].astype(BF16)

    has_prev = jnp.where(pos != 0, 1.0, 0.0)
    has_next = jnp.where(pos != tiles_per_seq - 1, 1.0, 0.0)
    ext_ref[0:HALO, :] = (normed(xp_ref[...]) * has_prev).astype(BF16)
    ext_ref[HALO:HALO + tm, :] = hb
    ext_ref[HALO + tm:, :] = (normed(xn_ref[...]) * has_next).astype(BF16)
    u_ref[...] = jnp.dot(ext_ref[...], why_ref[...], preferred_element_type=F32)
    conv = (cb_ref[...]
            + u_ref[pl.ds(HALO - 1, tm), :] * cw_ref[0:1, :]
            + u_ref[pl.ds(HALO, tm), :] * cw_ref[1:2, :]
            + u_ref[pl.ds(HALO + 1, tm), :] * cw_ref[2:3, :])
    for part, ref in enumerate((hv_ref, hx1_ref, hx2_ref)):
        _store_slabs(ref, conv[:, part * D_MIX:(part + 1) * D_MIX])


def _inproj(x2d, mod, gain, w_qkv, w_hy, conv_w, conv_b, seq_len):
    t = x2d.shape[0]
    tm = INPROJ_TILE
    n_halo_blocks = t // HALO
    per_tile = tm // HALO
    tiles_per_seq = seq_len // tm
    tile_out = jax.ShapeDtypeStruct((t, D_MIX), BF16)
    tile_spec = pl.BlockSpec((tm, D_MIX), lambda i: (i, 0))
    slab_rows = tm // LANE_BLOCK * FFT_NB
    slab_out = jax.ShapeDtypeStruct((t // seq_len, SLABS, D_MIX // LANES, seq_len // SLABS, LANES), F32)
    slab_spec = pl.BlockSpec((None, SLABS, D_MIX // LANES, slab_rows, LANES),
                             lambda i: (i // tiles_per_seq, 0, 0, i % tiles_per_seq, 0))
    return pl.pallas_call(
        functools.partial(_inproj_kernel, tiles_per_seq=tiles_per_seq, tm=tm),
        out_shape=[tile_out] * 3 + [slab_out] * 3,
        grid=(t // tm,),
        in_specs=[
            pl.BlockSpec((tm, D_MODEL), lambda i: (i, 0)),
            pl.BlockSpec((HALO, D_MODEL), lambda i: (jnp.maximum(i * per_tile - 1, 0), 0)),
            pl.BlockSpec((HALO, D_MODEL),
                         lambda i: (jnp.minimum((i + 1) * per_tile, n_halo_blocks - 1), 0)),
            pl.BlockSpec((None, N_MOD, D_MODEL), lambda i: (i * tm // seq_len, 0, 0)),
            _resident((1, D_MODEL)),
            _resident((D_MODEL, 3 * D_MIX)),
            _resident((D_MODEL, 3 * D_MIX)),
            _resident((3, 3 * D_MIX)),
            _resident((1, 3 * D_MIX)),
        ],
        out_specs=[tile_spec] * 3 + [slab_spec] * 3,
        scratch_shapes=[
            pltpu.VMEM((tm + 2 * HALO, D_MODEL), BF16),
            pltpu.VMEM((tm + 2 * HALO, 3 * D_MIX), F32),
        ],
        compiler_params=_cparams(1),
        name="inproj",
    )(x2d, x2d, x2d, mod, gain.reshape(1, D_MODEL), w_qkv, w_hy, conv_w,
      conv_b.reshape(1, 3 * D_MIX))


ROW_TOKENS = GRID_W
GROUP_ROWS = 2 * WIN_H
GROUP_TOKENS = GROUP_ROWS * ROW_TOKENS
WIN_TOKENS = WIN_H * ROW_TOKENS


def _attn_bias_table(rpb):
    n_heads, n_drow, n_dcol = rpb.shape
    period = 2 * GRID_W - 1
    wrapped = jnp.concatenate([rpb[..., WIN_W - 1:], jnp.zeros((n_heads, n_drow, period - n_dcol), F32),
                               rpb[..., :WIN_W - 1]], axis=-1).astype(F32)
    toeplitz = jnp.tile(wrapped, GRID_W)[..., :GRID_W * (period - 1)]
    toeplitz = toeplitz.reshape(n_heads, n_drow, GRID_W, period - 1)[..., :GRID_W]
    qc = np.arange(GRID_W)[:, None]
    kc = np.arange(GRID_W)[None, :]
    win_start = np.clip(qc - WIN_W // 2, 0, GRID_W - WIN_W)
    col_ok = (kc >= win_start) & (kc < win_start + WIN_W)
    masked = jnp.where(col_ok[None, None], toeplitz * LOG2_E, NEG_INF)
    bias = jnp.stack([masked[:, o:o + WIN_H] for o in range(WIN_H)], axis=0)
    bias = jnp.transpose(bias, (0, 1, 3, 2, 4))
    return bias.reshape(WIN_H, NA_HEADS, GRID_W, WIN_TOKENS)


NATTEN_UNROLL = 8
KV_WINDOW_ROWS = GROUP_ROWS + WIN_H


def _kv_window_start(g, rows):
    return jnp.clip(g * GROUP_ROWS - WIN_H // 2, 0, rows - KV_WINDOW_ROWS)


def _attn_kernel(q_ref, k_ref, v_ref, bias_ref, gain_ref, o_ref, acc, *, rows):
    g = pl.program_id(1)
    win_start = _kv_window_start(g, rows)
    first_head = lax.broadcasted_iota(jnp.int32, (ROW_TOKENS, 2 * NA_HEAD_DIM), 1) < NA_HEAD_DIM
    head_pairs = [slice(hp * 2 * NA_HEAD_DIM, (hp + 1) * 2 * NA_HEAD_DIM) for hp in range(NA_HEADS // 2)]

    def row_body(rr, carry):
        r = g * GROUP_ROWS + rr
        start = jnp.clip(r - WIN_H // 2, 0, rows - WIN_H)
        koff = pl.multiple_of((start - win_start) * ROW_TOKENS, ROW_TOKENS)
        row_class = start - r + WIN_H - 1
        qoff = pl.multiple_of(rr * ROW_TOKENS, ROW_TOKENS)
        scores = []
        for hp, lanes in enumerate(head_pairs):
            q2 = q_ref[pl.ds(qoff, ROW_TOKENS), lanes]
            kw = k_ref[pl.ds(koff, WIN_TOKENS), lanes]
            qm = jnp.concatenate([jnp.where(first_head, q2, jnp.zeros_like(q2)),
                                  jnp.where(first_head, jnp.zeros_like(q2), q2)], axis=0)
            s = lax.dot_general(qm, kw, (((1,), (1,)), ((), ())), preferred_element_type=F32)
            for hh in range(2):
                scores.append(s[hh * ROW_TOKENS:(hh + 1) * ROW_TOKENS] + bias_ref[row_class, 2 * hp + hh])
        probs, denoms = [], []
        for s in scores:
            p = jnp.exp2(s - jnp.max(s, axis=-1, keepdims=True))
            denoms.append(jnp.sum(p, axis=-1, keepdims=True))
            probs.append(p.astype(BF16))
        for hp, lanes in enumerate(head_pairs):
            vw = v_ref[pl.ds(koff, WIN_TOKENS), lanes]
            pv = jnp.dot(jnp.concatenate(probs[2 * hp:2 * hp + 2], axis=0), vw, preferred_element_type=F32)
            outs = [pv[hh * ROW_TOKENS:(hh + 1) * ROW_TOKENS] / denoms[2 * hp + hh] for hh in range(2)]
            acc[pl.ds(qoff, ROW_TOKENS), lanes] = jnp.where(first_head, outs[0], outs[1])
        return carry

    lax.fori_loop(0, GROUP_ROWS, row_body, 0, unroll=NATTEN_UNROLL)
    o_ref[...] = _rms(acc[...], gain_ref[...]).astype(BF16)


def _attention(q, k, v, bias, gain, batch, seq_len):
    rows = seq_len // GRID_W
    q3 = q.reshape(batch, seq_len, D_MIX)
    k3 = k.reshape(batch, seq_len, D_MIX)
    v3 = v.reshape(batch, seq_len, D_MIX)
    cur = pl.BlockSpec((None, GROUP_TOKENS, D_MIX), lambda b, g: (b, g, 0))
    window = pl.BlockSpec((None, pl.Element(KV_WINDOW_ROWS * ROW_TOKENS), pl.Element(D_MIX)),
                          lambda b, g: (b, _kv_window_start(g, rows) * ROW_TOKENS, 0))
    out = pl.pallas_call(
        functools.partial(_attn_kernel, rows=rows),
        out_shape=jax.ShapeDtypeStruct((batch, seq_len, D_MIX), BF16),
        grid=(batch, rows // GROUP_ROWS),
        in_specs=[
            cur, window, window,
            pl.BlockSpec((WIN_H, NA_HEADS, GRID_W, WIN_TOKENS), lambda b, g: (0, 0, 0, 0),
                         pipeline_mode=pl.Buffered(1)),
            pl.BlockSpec((1, D_MIX), lambda b, g: (0, 0)),
        ],
        out_specs=cur,
        scratch_shapes=[pltpu.VMEM((GROUP_TOKENS, D_MIX), F32)],
        compiler_params=_cparams(2),
        name="natten",
    )(q3, k3, v3, bias, gain.reshape(1, D_MIX))
    return out.reshape(batch * seq_len, D_MIX)


def _split_hi_lo(x):
    hi = x.astype(ml_dtypes.bfloat16)
    lo = (x - hi.astype(np.float64)).astype(ml_dtypes.bfloat16)
    return hi, lo


def _stack_hi_lo(m):
    hi, lo = _split_hi_lo(m)
    return np.concatenate([hi, lo], axis=-2)


def _embed(re, im):
    return np.concatenate([np.concatenate([re, -im], axis=-1),
                           np.concatenate([im, re], axis=-1)], axis=-2)


@functools.lru_cache(maxsize=None)
def _fft_tables(seq_len):
    n = 2 * seq_len
    n2 = LANE_BLOCK
    n1 = n // n2
    i2 = np.arange(n2)[:, None, None]
    k1 = np.arange(n1)[None, :, None]
    i1 = np.arange(n1)[None, None, :]
    ang = -2.0 * np.pi * ((k1 * (n2 * i1 + i2)) % n) / n
    gr_full, gi_full = np.cos(ang), np.sin(ang)
    g_real = _stack_hi_lo(np.concatenate([gr_full, gi_full], axis=1))
    gr, gi = gr_full[..., :n1 // 2], gi_full[..., :n1 // 2]
    g_fwd = _stack_hi_lo(_embed(gr, gi))
    g_inv = _stack_hi_lo(_embed(np.swapaxes(gr, 1, 2) / n, -np.swapaxes(gi, 1, 2) / n))
    jk = np.outer(np.arange(n2), np.arange(n2))
    ang2 = -2.0 * np.pi * (jk % n2) / n2
    fr, fi = np.cos(ang2), np.sin(ang2)
    f_fwd = _stack_hi_lo(_embed(fr, fi))
    f_inv = _stack_hi_lo(_embed(fr, -fi))
    return n1, g_fwd, g_real, g_inv, f_fwd, f_inv


def _dft3(m_hl, x, m):
    x_hi = x.astype(BF16)
    x_lo = (x - x_hi.astype(F32)).astype(BF16)
    t = jnp.dot(m_hl, x_hi, preferred_element_type=F32)
    return t[:m] + t[m:] + jnp.dot(m_hl[:m], x_lo, preferred_element_type=F32)


def _stage_a_forward(x_ref, g_ref, a_ref, *, n1):
    for i in range(FFT_NB):
        x = jnp.concatenate([_load_strided(x_ref, (0,), i, n1 // 2),
                             _load_strided(x_ref, (1,), i, n1 // 2)], axis=0)
        _store_strided(a_ref, i, _dft3(g_ref[i], x, 2 * n1))


def _stage_a_inverse(d_ref, gi_ref, y_ref, *, n1):
    for i in range(FFT_NB):
        _store_strided(y_ref, i, _dft3(gi_ref[i], _load_strided(d_ref, (), i, 2 * n1), n1))


def _k1_kernel(x_ref, g_ref, a_ref, *, n1):
    _stage_a_forward(x_ref, g_ref, a_ref, n1=n1)


FFT_TILES = FFT_CT // LANES


def _seq_spec(n1):
    return pl.BlockSpec((None, 2, None, FFT_TILES, n1 // 2 * FFT_NB, LANES),
                        lambda c, j, p: (p, 0, j, c, 0, 0))


def _spec_spec(n1):
    return pl.BlockSpec((None, None, FFT_TILES, 2 * n1 * FFT_NB, LANES), lambda c, j, p: (p, j, c, 0, 0))


def _fft_stage_a(x6, g_fwd, n1):
    pairs, _, slabs, tiles, _, _ = x6.shape
    return pl.pallas_call(
        functools.partial(_k1_kernel, n1=n1),
        out_shape=jax.ShapeDtypeStruct((pairs, slabs, tiles, 2 * n1 * FFT_NB, LANES), F32),
        grid=(tiles // FFT_TILES, slabs, pairs),
        in_specs=[
            _seq_spec(n1),
            pl.BlockSpec((FFT_NB, 4 * n1, n1), lambda c, j, p: (j, 0, 0)),
        ],
        out_specs=_spec_spec(n1),
        compiler_params=_cparams(3),
        name="hy_stage_a",
    )(x6, g_fwd)


def _load_low_index(ref, part, kk):
    return jnp.concatenate([ref[:, t, part, kk].reshape(LANE_BLOCK, LANES) for t in range(ref.shape[1])],
                           axis=1)


def _store_low_index(ref, part, kk, val):
    for t in range(ref.shape[1]):
        ref[:, t, part, kk] = val[:, t * LANES:(t + 1) * LANES].reshape(SLABS, FFT_NB, LANES)


def _k2_kernel(a_ref, kf_ref, f_ref, fi_ref, d_ref, *, kb):
    n2 = LANE_BLOCK
    f_hl = f_ref[...]
    fi_hl = fi_ref[...]

    def body(group, carry):
        ks = [group * FFT_K_GROUP + u for u in range(FFT_K_GROUP)]
        spectra = [_dft3(f_hl, jnp.concatenate([_load_low_index(a_ref, 0, kk),
                                                _load_low_index(a_ref, 1, kk)], axis=0), 2 * n2)
                   for kk in ks]
        products = []
        for kk, c in zip(ks, spectra):
            cr, ci = c[:n2], c[n2:]
            kr, ki = kf_ref[0, kk], kf_ref[1, kk]
            products.append(jnp.concatenate([cr * kr - ci * ki, cr * ki + ci * kr], axis=0))
        for kk, y in zip(ks, products):
            d = _dft3(fi_hl, y, 2 * n2)
            _store_low_index(d_ref, 0, kk, d[:n2])
            _store_low_index(d_ref, 1, kk, d[n2:])
        return carry

    lax.fori_loop(0, kb // FFT_K_GROUP, body, 0)


def _fft_stage_c(a4, kf, f_fwd, f_inv, order, n1):
    pairs, slabs, tiles, _, _ = a4.shape
    n2 = LANE_BLOCK
    kb = FFT_KB
    ch_blocks = tiles // FFT_TILES
    a7 = a4.reshape(pairs, slabs, tiles, 2, n1, FFT_NB, LANES)
    spec = pl.BlockSpec((None, slabs, FFT_TILES, 2, kb, FFT_NB, LANES),
                        lambda c, k, p: (p, 0, c, 0, k, 0, 0))
    d7 = pl.pallas_call(
        functools.partial(_k2_kernel, kb=kb),
        out_shape=jax.ShapeDtypeStruct(a7.shape, F32),
        grid=(ch_blocks, n1 // kb, pairs),
        in_specs=[
            spec,
            pl.BlockSpec((2, kb, n2, FFT_CT), lambda c, k, p: (0, k, 0, order * ch_blocks + c)),
            pl.BlockSpec((4 * n2, 2 * n2), lambda c, k, p: (0, 0)),
            pl.BlockSpec((4 * n2, 2 * n2), lambda c, k, p: (0, 0)),
        ],
        out_specs=spec,
        compiler_params=_cparams(3),
        name="hy_stage_c",
    )(a7, kf, f_fwd, f_inv)
    return d7.reshape(a4.shape)


def _k3_kernel(d_ref, gi_ref, z_ref, x_ref, skip_ref, *rest, n1, forward):
    if forward:
        g_ref, o_ref, a_ref, y_ref = rest
    else:
        o_ref, y_ref = rest
    _stage_a_inverse(d_ref, gi_ref, y_ref, n1=n1)
    rows = n1 // 2 * FFT_NB
    for part in range(2):
        for t in range(FFT_TILES):
            conv = y_ref[t, part * rows:(part + 1) * rows, :]
            o_ref[part, t] = x_ref[part, t] * (conv + skip_ref[t] * z_ref[part, t])
    if forward:
        _stage_a_forward(o_ref, g_ref, a_ref, n1=n1)


def _fft_stage_a_inverse(d4, g_inv, z6, x6, skip_row, n1, g_fwd=None):
    pairs, slabs, tiles, _, _ = d4.shape
    forward = g_fwd is not None
    in_specs = [
        _spec_spec(n1),
        pl.BlockSpec((FFT_NB, 2 * n1, 2 * n1), lambda c, j, p: (j, 0, 0)),
        _seq_spec(n1),
        _seq_spec(n1),
        pl.BlockSpec((FFT_TILES, 1, LANES), lambda c, j, p: (c, 0, 0)),
    ]
    args = [d4, g_inv, z6, x6, skip_row]
    out_shape = [jax.ShapeDtypeStruct(z6.shape, F32)]
    out_specs = [_seq_spec(n1)]
    if forward:
        in_specs.append(pl.BlockSpec((FFT_NB, 4 * n1, n1), lambda c, j, p: (j, 0, 0)))
        args.append(g_fwd)
        out_shape.append(jax.ShapeDtypeStruct(d4.shape, F32))
        out_specs.append(_spec_spec(n1))
    return pl.pallas_call(
        functools.partial(_k3_kernel, n1=n1, forward=forward),
        out_shape=out_shape,
        grid=(tiles // FFT_TILES, slabs, pairs),
        in_specs=in_specs,
        out_specs=out_specs,
        scratch_shapes=[pltpu.VMEM((FFT_TILES, n1 * FFT_NB, LANES), F32)],
        compiler_params=_cparams(3),
        name="hy_stage_a_inv_fwd" if forward else "hy_stage_a_inv",
    )(*args)


def _filt_kernel(z_ref, w1_ref, b1_ref, w2_ref, b2_ref, w3_ref, b3_ref, wo_ref, freq_ref,
                 delta_ref, h_ref, l1_ref, *, tl):
    i = pl.program_id(0)
    freq = freq_ref[...]

    def dot(a, b):
        return jnp.dot(a, b, precision=HIGHEST, preferred_element_type=F32)

    z = z_ref[...]
    h = jnp.sin(freq * (dot(z, w1_ref[...]) + b1_ref[...]))
    h = jnp.sin(freq * (dot(h, w2_ref[...]) + b2_ref[...]))
    h = jnp.sin(freq * (dot(h, w3_ref[...]) + b3_ref[...]))
    hf = dot(h, wo_ref[0]) * jnp.exp(-z[:, 0:1] * delta_ref[...])
    hb = dot(h, wo_ref[1]) * jnp.exp(-z[:, FILT_HALF:FILT_HALF + 1] * delta_ref[...])
    row = i * tl + lax.broadcasted_iota(jnp.int32, (tl, 1), 0)
    hb = jnp.where(row == 0, 0.0, hb)
    _store_slabs(h_ref.at[0], hf)
    _store_slabs(h_ref.at[1], hb)

    @pl.when(i == 0)
    def _():
        l1_ref[...] = jnp.zeros_like(l1_ref)

    l1_ref[...] += (jnp.sum(jnp.abs(hf), axis=0, keepdims=True)
                    + jnp.sum(jnp.abs(hb), axis=0, keepdims=True))


def _pad_to(x, shape):
    return jnp.pad(x, [(0, s - d) for d, s in zip(x.shape, shape)])


def _filter_taps(seq_len, w1, b1, w2, b2, w3, b3, wo, freq):
    t = jnp.linspace(0.0, 1.0, seq_len, dtype=F32)[:, None]
    w = 2.0 * math.pi * jnp.arange(seq_len, dtype=F32)[:, None] / seq_len
    f = jnp.linspace(1e-4, HY_BANDS - 1, HY_BANDS, dtype=F32)[None, :]
    z = _pad_to(jnp.concatenate([t, jnp.cos(f * w), -jnp.sin(f * w)], axis=-1), (seq_len, FILT_HALF))
    z = jnp.concatenate([z, z[::-1]], axis=1)
    deltas = jnp.abs(jnp.linspace(math.log(HY_TARGET) / HY_FAST_DECAY,
                                  math.log(HY_TARGET) / HY_SLOW_DECAY, D_MIX, dtype=F32))
    n_cols = HY_ORDER * D_MIX

    def both(m):
        m = _pad_to(m, (FILT_HALF, FILT_HALF))
        zero = jnp.zeros_like(m)
        return jnp.concatenate([jnp.concatenate([m, zero], axis=1),
                                jnp.concatenate([zero, m], axis=1)], axis=0)

    row = lambda v: jnp.tile(_pad_to(v.reshape(1, -1), (1, FILT_HALF)), (1, 2))
    wo_p = _pad_to(wo, (FILT_HALF, 2 * n_cols))
    zero = jnp.zeros((FILT_HALF, n_cols), F32)
    wo_dirs = jnp.stack([jnp.concatenate([wo_p[:, :n_cols], zero], axis=0),
                         jnp.concatenate([zero, wo_p[:, n_cols:]], axis=0)], axis=0)
    pad2 = (FILT_PAD, FILT_PAD)
    tl = FILT_TILE
    const = lambda shape: pl.BlockSpec(shape, lambda i: (0,) * len(shape))
    return pl.pallas_call(
        functools.partial(_filt_kernel, tl=tl),
        out_shape=[jax.ShapeDtypeStruct((2, SLABS, n_cols // LANES, seq_len // SLABS, LANES), F32),
                   jax.ShapeDtypeStruct((1, n_cols), F32)],
        grid=(seq_len // tl,),
        in_specs=[
            pl.BlockSpec((tl, FILT_PAD), lambda i: (i, 0)),
            const(pad2), const((1, FILT_PAD)), const(pad2), const((1, FILT_PAD)),
            const(pad2), const((1, FILT_PAD)), const((2, FILT_PAD, n_cols)), const((1, FILT_PAD)),
            const((1, n_cols)),
        ],
        out_specs=[pl.BlockSpec((2, SLABS, n_cols // LANES, tl // LANE_BLOCK * FFT_NB, LANES),
                                lambda i: (0, 0, 0, i, 0)),
                   const((1, n_cols))],
        compiler_params=_cparams(1),
        name="hy_filter_taps",
    )(z, both(w1), row(b1), both(w2), row(b2), both(w3), row(b3), wo_dirs, row(freq),
      jnp.tile(deltas, HY_ORDER).reshape(1, n_cols))


def _k2f_kernel(a_ref, l1_ref, f_ref, kf_ref, *, kb):
    n2 = LANE_BLOCK
    f_hl = f_ref[...]
    inv_l1 = 1.0 / l1_ref[...]

    def body(group, carry):
        ks = [group * FFT_K_GROUP + u for u in range(FFT_K_GROUP)]
        spectra = [_dft3(f_hl, jnp.concatenate([_load_low_index(a_ref, 0, kk),
                                                _load_low_index(a_ref, 1, kk)], axis=0), 2 * n2)
                   for kk in ks]
        for kk, c in zip(ks, spectra):
            kf_ref[0, kk] = c[:n2] * inv_l1
            kf_ref[1, kk] = c[n2:] * inv_l1
        return carry

    lax.fori_loop(0, kb // FFT_K_GROUP, body, 0)


def _filter_spectrum(a4, l1, f_fwd, n1):
    _, slabs, tiles, _, _ = a4.shape
    cols = tiles * LANES
    n2 = LANE_BLOCK
    kb, ct = FFT_KB, FFT_CT
    a6 = a4.reshape(slabs, tiles, 2, n1, FFT_NB, LANES)
    return pl.pallas_call(
        functools.partial(_k2f_kernel, kb=kb),
        out_shape=jax.ShapeDtypeStruct((2, n1, n2, cols), F32),
        grid=(cols // ct, n1 // kb),
        in_specs=[
            pl.BlockSpec((slabs, FFT_TILES, 2, kb, FFT_NB, LANES), lambda c, k: (0, c, 0, k, 0, 0)),
            pl.BlockSpec((1, ct), lambda c, k: (0, c)),
            pl.BlockSpec((4 * n2, 2 * n2), lambda c, k: (0, 0)),
        ],
        out_specs=pl.BlockSpec((2, kb, n2, ct), lambda c, k: (0, k, 0, c)),
        compiler_params=_cparams(2),
        name="hy_filter_spectrum",
    )(a6, l1, f_fwd)


def _hyena(hv, hx1, hx2, skip, filt_params, seq_len):
    n1, *tables = _fft_tables(seq_len)
    g_fwd, g_real, g_inv, f_fwd, f_inv = (jnp.asarray(m) for m in tables)
    taps, l1 = _filter_taps(seq_len, *filt_params)
    kf = _filter_spectrum(_fft_stage_a(taps[None], g_real, n1), l1, f_fwd, n1)
    as_pairs = lambda a: a.reshape((a.shape[0] // 2, 2) + a.shape[1:])
    z0, x1, x2 = as_pairs(hv), as_pairs(hx1), as_pairs(hx2)
    d = _fft_stage_c(_fft_stage_a(z0, g_fwd, n1), kf, f_fwd, f_inv, 0, n1)
    skip_rows = skip.reshape(HY_ORDER, D_MIX // LANES, 1, LANES)
    z1, a = _fft_stage_a_inverse(d, g_inv, z0, x1, skip_rows[0], n1, g_fwd=g_fwd)
    d = _fft_stage_c(a, kf, f_fwd, f_inv, 1, n1)
    (z2,) = _fft_stage_a_inverse(d, g_inv, z1, x2, skip_rows[1], n1)
    return z2.reshape(hv.shape)


def _trunk(x, mod, p, final_norm):
    batch, seq_len, _ = x.shape
    x2d = x.reshape(batch * seq_len, D_MODEL)
    x2d = _ffn1(x2d, mod, p["ffn1_norm"], p["ffn1_w_gate"], p["ffn1_w_up"], p["ffn1_w_down"], seq_len)
    q, k, v, hv, hx1, hx2 = _inproj(x2d, mod, p["mix_norm"], p["w_qkv"], p["w_hy"],
                                    p["hy_conv_w"], p["hy_conv_b"], seq_len)
    attn_n = _attention(q, k, v, p["attn_bias"], p["attn_out_norm"], batch, seq_len)
    hz = _hyena(hv, hx1, hx2, p["hy_skip"], p["hy_filter"], seq_len)
    y = _mix_ffn2(x2d, attn_n, hz, mod, p["hy_out_norm"], p["w_out_attn"], p["w_out_hy"],
                  p["ffn2_norm"], p["ffn2_w_gate"], p["ffn2_w_up"], p["ffn2_w_down"], final_norm, seq_len)
    return y.reshape(batch, seq_len, D_MODEL)


def kernel(x_prompt, x_sample, c_prompt, c_sample, w_ada, b_ada, ffn1_norm, ffn1_w_gate, ffn1_w_up,
           ffn1_w_down, mix_norm, w_in, na_rpb, hy_conv_w, hy_conv_b, hy_w1, hy_b1, hy_w2, hy_b2,
           hy_w3, hy_b3, hy_wo, hy_sin_freq, hy_skip, attn_out_norm, hy_out_norm, w_out, ffn2_norm,
           ffn2_w_gate, ffn2_w_up, ffn2_w_down, final_norm):
    assert w_ada.shape[0] == 1, "single-layer encoder"
    n_prompt = c_prompt.shape[0]
    mod_all = _ada(jnp.concatenate([c_prompt, c_sample], axis=0), w_ada[0], b_ada[0])
    mod_all = mod_all.reshape(-1, N_MOD, D_MODEL)
    bf = lambda w: w[0].astype(BF16)
    p = {
        "ffn1_norm": ffn1_norm[0], "ffn1_w_gate": bf(ffn1_w_gate), "ffn1_w_up": bf(ffn1_w_up),
        "ffn1_w_down": bf(ffn1_w_down),
        "mix_norm": mix_norm[0],
        "w_qkv": w_in[0, :, :3 * D_MIX].astype(BF16), "w_hy": w_in[0, :, 3 * D_MIX:].astype(BF16),
        "attn_bias": _attn_bias_table(na_rpb[0]),
        "hy_conv_w": hy_conv_w[0], "hy_conv_b": hy_conv_b[0],
        "hy_filter": (hy_w1[0], hy_b1[0], hy_w2[0], hy_b2[0], hy_w3[0], hy_b3[0], hy_wo[0],
                      hy_sin_freq[0]),
        "hy_skip": hy_skip[0],
        "attn_out_norm": attn_out_norm[0], "hy_out_norm": hy_out_norm[0],
        "w_out_attn": w_out[0, :D_MIX].astype(BF16), "w_out_hy": w_out[0, D_MIX:].astype(BF16),
        "ffn2_norm": ffn2_norm[0], "ffn2_w_gate": bf(ffn2_w_gate), "ffn2_w_up": bf(ffn2_w_up),
        "ffn2_w_down": bf(ffn2_w_down),
    }
    y_prompt = _trunk(x_prompt, mod_all[:n_prompt], p, final_norm)
    y_sample = _trunk(x_sample, mod_all[n_prompt:], p, final_norm)
    return (y_prompt, y_sample)
```

```python
import functools
import math

import ml_dtypes
import numpy as np
import jax
import jax.numpy as jnp
from jax import lax
from jax.experimental import pallas as pl
from jax.experimental.pallas import tpu as pltpu

F32 = jnp.float32
BF16 = jnp.bfloat16
HIGHEST = lax.Precision.HIGHEST

D_MODEL = 1024
GRID_W = 64
D_MIX = 512
NA_HEADS = 8
NA_HEAD_DIM = D_MIX // NA_HEADS
WIN_H = 8
WIN_W = 16
HY_ORDER = 2
HY_BANDS = 8
HY_FAST_DECAY = 0.3
HY_SLOW_DECAY = 1.5
HY_TARGET = 1e-2
D_FF = ((8 * D_MODEL // 3 + 127) // 128) * 128
N_MOD = 9
EPS = 1e-6
NEG_INF = -1e30
LOG2_E = math.log2(math.e)

V7X_VMEM_LIMIT_BYTES = 56 * 1024 * 1024
INPROJ_TILE = 1024
FFN_TILE = 1024
V7X_MXU_WIDTH = 256
_FF_SPLIT = (D_FF // V7X_MXU_WIDTH + 1) // 2 * V7X_MXU_WIDTH
FF_CHUNKS = ((0, _FF_SPLIT), (_FF_SPLIT, D_FF))
HALO = 16
LANE_BLOCK = 128
FFT_CT = 256
FFT_NB = 8
SLAB_GROUP = 2
STEP_POSITIONS = SLAB_GROUP * FFT_NB
FFT_KB = 16
FFT_K_GROUP = 8
FILT_TILE = 512
FILT_HALF = 64
FILT_PAD = 2 * FILT_HALF


def _cparams(n_axes):
    return pltpu.CompilerParams(
        dimension_semantics=("arbitrary",) * n_axes,
        vmem_limit_bytes=V7X_VMEM_LIMIT_BYTES,
    )


def _rms(x, gain):
    ms = jnp.mean(x * x, axis=-1, keepdims=True)
    return x * lax.rsqrt(ms + EPS) * gain


def _silu(x):
    return x / (1.0 + jnp.exp(-x))


SLABS = LANE_BLOCK // FFT_NB
LANES = 128


def _store_slabs(ref, tile):
    for i1 in range(tile.shape[0] // LANE_BLOCK):
        for t in range(tile.shape[1] // LANES):
            rows = tile[i1 * LANE_BLOCK:(i1 + 1) * LANE_BLOCK, t * LANES:(t + 1) * LANES]
            ref[:, t, i1 * FFT_NB:(i1 + 1) * FFT_NB, :] = rows.reshape(SLABS, FFT_NB, LANES)


def _load_slabs(ref):
    _, tiles, rows, _ = ref.shape
    return jnp.concatenate(
        [jnp.concatenate([ref[:, t, i1 * FFT_NB:(i1 + 1) * FFT_NB, :].reshape(LANE_BLOCK, LANES)
                          for t in range(tiles)], axis=1)
         for i1 in range(rows // FFT_NB)], axis=0)


def _load_strided(ref, lead, start, size):
    tiles = ref.shape[len(lead)]
    return jnp.concatenate([ref[lead + (t, pl.ds(start, size, stride=FFT_NB), slice(None))]
                            for t in range(tiles)], axis=1)


def _store_strided(ref, start, val):
    for t in range(ref.shape[0]):
        ref[t, pl.ds(start, val.shape[0], stride=FFT_NB), :] = val[:, t * LANES:(t + 1) * LANES]


def _ada_kernel(c_ref, w_ref, b_ref, o_ref):
    s = _silu(c_ref[...])
    o_ref[...] = jnp.dot(s, w_ref[...], precision=HIGHEST, preferred_element_type=F32) + b_ref[...]


def _ada(c_all, w_ada, b_ada):
    rows = c_all.shape[0]
    n_out = w_ada.shape[1]
    tn = D_MODEL
    return pl.pallas_call(
        _ada_kernel,
        out_shape=jax.ShapeDtypeStruct((rows, n_out), F32),
        grid=(n_out // tn,),
        in_specs=[
            pl.BlockSpec((rows, D_MODEL), lambda j: (0, 0)),
            pl.BlockSpec((D_MODEL, tn), lambda j: (0, j)),
            pl.BlockSpec((1, tn), lambda j: (0, j)),
        ],
        out_specs=pl.BlockSpec((rows, tn), lambda j: (0, j)),
        compiler_params=_cparams(1),
        name="ada_mod",
    )(c_all, w_ada, b_ada.reshape(1, n_out))


def _ffn_residual(x, mod_ref, mod_base, gain_ref, wg_ref, wu_ref, wd_ref):
    shift = mod_ref[mod_base:mod_base + 1, :]
    scale = mod_ref[mod_base + 1:mod_base + 2, :]
    gate = mod_ref[mod_base + 2:mod_base + 3, :]
    hb = (_rms(x, gain_ref[...]) * (1.0 + scale) + shift).astype(BF16)
    acc = None
    for c0, c1 in FF_CHUNKS:
        g = jnp.dot(hb, wg_ref[:, c0:c1], preferred_element_type=F32)
        u = jnp.dot(hb, wu_ref[:, c0:c1], preferred_element_type=F32)
        a = (_silu(g) * u).astype(BF16)
        d = jnp.dot(a, wd_ref[c0:c1, :], preferred_element_type=F32)
        acc = d if acc is None else acc + d
    return x + 0.5 * gate * acc


def _ffn1_kernel(x_ref, mod_ref, gain_ref, wg_ref, wu_ref, wd_ref, o_ref):
    o_ref[...] = _ffn_residual(x_ref[...], mod_ref, 0, gain_ref, wg_ref, wu_ref, wd_ref)


def _mix_ffn2_kernel(x_ref, an_ref, hz_ref, mod_ref, hy_gain_ref, wa_ref, wh_ref,
                     gain_ref, wg_ref, wu_ref, wd_ref, fn_ref, o_ref):
    hn = _rms(_load_slabs(hz_ref), hy_gain_ref[...]).astype(BF16)
    mixed = (jnp.dot(an_ref[...], wa_ref[...], preferred_element_type=F32)
             + jnp.dot(hn, wh_ref[...], preferred_element_type=F32))
    x = x_ref[...] + mod_ref[5:6, :] * mixed
    y = _ffn_residual(x, mod_ref, 6, gain_ref, wg_ref, wu_ref, wd_ref)
    o_ref[...] = _rms(y, fn_ref[...])


def _resident(shape):
    return pl.BlockSpec(shape, lambda i: (0, 0), pipeline_mode=pl.Buffered(1))


def _token_spec(tm, width):
    return pl.BlockSpec((tm, width), lambda i: (i, 0))


def _mod_spec(tm, seq_len):
    return pl.BlockSpec((None, N_MOD, D_MODEL), lambda i: (i * tm // seq_len, 0, 0))


def _ffn1(x2d, mod, gain, wg, wu, wd, seq_len):
    t = x2d.shape[0]
    tm = FFN_TILE
    return pl.pallas_call(
        _ffn1_kernel,
        out_shape=jax.ShapeDtypeStruct((t, D_MODEL), F32),
        grid=(t // tm,),
        in_specs=[
            _token_spec(tm, D_MODEL), _mod_spec(tm, seq_len), _resident((1, D_MODEL)),
            _resident((D_MODEL, D_FF)), _resident((D_MODEL, D_FF)), _resident((D_FF, D_MODEL)),
        ],
        out_specs=_token_spec(tm, D_MODEL),
        compiler_params=_cparams(1),
        name="ffn1",
    )(x2d, mod, gain.reshape(1, D_MODEL), wg, wu, wd)


def _mix_ffn2(x2d, attn_n, hz, mod, hy_gain, w_attn, w_hy, gain, wg, wu, wd, final_gain, seq_len):
    t = x2d.shape[0]
    tm = FFN_TILE
    tiles_per_seq = seq_len // tm
    return pl.pallas_call(
        _mix_ffn2_kernel,
        out_shape=jax.ShapeDtypeStruct((t, D_MODEL), F32),
        grid=(t // tm,),
        in_specs=[
            _token_spec(tm, D_MODEL),
            _token_spec(tm, D_MIX),
            pl.BlockSpec((None, SLABS, D_MIX // LANES, tm // LANE_BLOCK * FFT_NB, LANES),
                         lambda i: (i // tiles_per_seq, 0, 0, i % tiles_per_seq, 0)),
            _mod_spec(tm, seq_len),
            _resident((1, D_MIX)), _resident((D_MIX, D_MODEL)), _resident((D_MIX, D_MODEL)),
            _resident((1, D_MODEL)),
            _resident((D_MODEL, D_FF)), _resident((D_MODEL, D_FF)), _resident((D_FF, D_MODEL)),
            _resident((1, D_MODEL)),
        ],
        out_specs=_token_spec(tm, D_MODEL),
        compiler_params=_cparams(1),
        name="mix_ffn2",
    )(x2d, attn_n, hz, mod, hy_gain.reshape(1, D_MIX), w_attn, w_hy, gain.reshape(1, D_MODEL),
      wg, wu, wd, final_gain.reshape(1, D_MODEL))


def _inproj_kernel(x_ref, xp_ref, xn_ref, mod_ref, gain_ref, wqkv_ref, why_ref, cw_ref, cb_ref,
                   q_ref, k_ref, v_ref, hv_ref, hx1_ref, hx2_ref, ext_ref, u_ref,
                   *, tiles_per_seq, tm):
    pos = pl.program_id(0) % tiles_per_seq
    gain = gain_ref[...]
    shift = mod_ref[3:4, :]
    scale = 1.0 + mod_ref[4:5, :]

    def normed(x):
        return _rms(x, gain) * scale + shift

    hb = normed(x_ref[...]).astype(BF16)
    qkv = jnp.dot(hb, wqkv_ref[...], preferred_element_type=F32)
    q_ref[...] = (qkv[:, :D_MIX] * (NA_HEAD_DIM ** -0.5 * LOG2_E)).astype(BF16)
    k_ref[...] = qkv[:, D_MIX:2 * D_MIX].astype(BF16)
    v_ref[...] = qkv[:, 2 * D_MIX:].astype(BF16)

    has_prev = jnp.where(pos != 0, 1.0, 0.0)
    has_next = jnp.where(pos != tiles_per_seq - 1, 1.0, 0.0)
    ext_ref[0:HALO, :] = (normed(xp_ref[...]) * has_prev).astype(BF16)
    ext_ref[HALO:HALO + tm, :] = hb
    ext_ref[HALO + tm:, :] = (normed(xn_ref[...]) * has_next).astype(BF16)
    u_ref[...] = jnp.dot(ext_ref[...], why_ref[...], preferred_element_type=F32)
    conv = (cb_ref[...]
            + u_ref[pl.ds(HALO - 1, tm), :] * cw_ref[0:1, :]
            + u_ref[pl.ds(HALO, tm), :] * cw_ref[1:2, :]
            + u_ref[pl.ds(HALO + 1, tm), :] * cw_ref[2:3, :])
    for part, ref in enumerate((hv_ref, hx1_ref, hx2_ref)):
        _store_slabs(ref, conv[:, part * D_MIX:(part + 1) * D_MIX])


def _inproj(x2d, mod, gain, w_qkv, w_hy, conv_w, conv_b, seq_len):
    t = x2d.shape[0]
    tm = INPROJ_TILE
    n_halo_blocks = t // HALO
    per_tile = tm // HALO
    tiles_per_seq = seq_len // tm
    tile_out = jax.ShapeDtypeStruct((t, D_MIX), BF16)
    tile_spec = pl.BlockSpec((tm, D_MIX), lambda i: (i, 0))
    slab_rows = tm // LANE_BLOCK * FFT_NB
    slab_out = jax.ShapeDtypeStruct((t // seq_len, SLABS, D_MIX // LANES, seq_len // SLABS, LANES), F32)
    slab_spec = pl.BlockSpec((None, SLABS, D_MIX // LANES, slab_rows, LANES),
                             lambda i: (i // tiles_per_seq, 0, 0, i % tiles_per_seq, 0))
    return pl.pallas_call(
        functools.partial(_inproj_kernel, tiles_per_seq=tiles_per_seq, tm=tm),
        out_shape=[tile_out] * 3 + [slab_out] * 3,
        grid=(t // tm,),
        in_specs=[
            pl.BlockSpec((tm, D_MODEL), lambda i: (i, 0)),
            pl.BlockSpec((HALO, D_MODEL), lambda i: (jnp.maximum(i * per_tile - 1, 0), 0)),
            pl.BlockSpec((HALO, D_MODEL),
                         lambda i: (jnp.minimum((i + 1) * per_tile, n_halo_blocks - 1), 0)),
            pl.BlockSpec((None, N_MOD, D_MODEL), lambda i: (i * tm // seq_len, 0, 0)),
            _resident((1, D_MODEL)),
            _resident((D_MODEL, 3 * D_MIX)),
            _resident((D_MODEL, 3 * D_MIX)),
            _resident((3, 3 * D_MIX)),
            _resident((1, 3 * D_MIX)),
        ],
        out_specs=[tile_spec] * 3 + [slab_spec] * 3,
        scratch_shapes=[
            pltpu.VMEM((tm + 2 * HALO, D_MODEL), BF16),
            pltpu.VMEM((tm + 2 * HALO, 3 * D_MIX), F32),
        ],
        compiler_params=_cparams(1),
        name="inproj",
    )(x2d, x2d, x2d, mod, gain.reshape(1, D_MODEL), w_qkv, w_hy, conv_w,
      conv_b.reshape(1, 3 * D_MIX))


ROW_TOKENS = GRID_W
GROUP_ROWS = 2 * WIN_H
GROUP_TOKENS = GROUP_ROWS * ROW_TOKENS
WIN_TOKENS = WIN_H * ROW_TOKENS


def _attn_bias_table(rpb):
    n_heads, n_drow, n_dcol = rpb.shape
    period = 2 * GRID_W - 1
    wrapped = jnp.concatenate([rpb[..., WIN_W - 1:], jnp.zeros((n_heads, n_drow, period - n_dcol), F32),
                               rpb[..., :WIN_W - 1]], axis=-1).astype(F32)
    toeplitz = jnp.tile(wrapped, GRID_W)[..., :GRID_W * (period - 1)]
    toeplitz = toeplitz.reshape(n_heads, n_drow, GRID_W, period - 1)[..., :GRID_W]
    qc = np.arange(GRID_W)[:, None]
    kc = np.arange(GRID_W)[None, :]
    win_start = np.clip(qc - WIN_W // 2, 0, GRID_W - WIN_W)
    col_ok = (kc >= win_start) & (kc < win_start + WIN_W)
    masked = jnp.where(col_ok[None, None], toeplitz * LOG2_E, NEG_INF)
    bias = jnp.stack([masked[:, o:o + WIN_H] for o in range(WIN_H)], axis=0)
    bias = jnp.transpose(bias, (0, 1, 3, 2, 4))
    return bias.reshape(WIN_H, NA_HEADS, GRID_W, WIN_TOKENS)


NATTEN_UNROLL = 8
KV_WINDOW_ROWS = GROUP_ROWS + WIN_H


def _kv_window_start(g, rows):
    return jnp.clip(g * GROUP_ROWS - WIN_H // 2, 0, rows - KV_WINDOW_ROWS)


def _attn_kernel(q_ref, k_ref, v_ref, bias_ref, gain_ref, o_ref, acc, *, rows):
    g = pl.program_id(1)
    win_start = _kv_window_start(g, rows)
    first_head = lax.broadcasted_iota(jnp.int32, (ROW_TOKENS, 2 * NA_HEAD_DIM), 1) < NA_HEAD_DIM
    head_pairs = [slice(hp * 2 * NA_HEAD_DIM, (hp + 1) * 2 * NA_HEAD_DIM) for hp in range(NA_HEADS // 2)]

    def row_body(rr, carry):
        r = g * GROUP_ROWS + rr
        start = jnp.clip(r - WIN_H // 2, 0, rows - WIN_H)
        koff = pl.multiple_of((start - win_start) * ROW_TOKENS, ROW_TOKENS)
        row_class = start - r + WIN_H - 1
        qoff = pl.multiple_of(rr * ROW_TOKENS, ROW_TOKENS)
        scores = []
        for hp, lanes in enumerate(head_pairs):
            q2 = q_ref[pl.ds(qoff, ROW_TOKENS), lanes]
            kw = k_ref[pl.ds(koff, WIN_TOKENS), lanes]
            qm = jnp.concatenate([jnp.where(first_head, q2, jnp.zeros_like(q2)),
                                  jnp.where(first_head, jnp.zeros_like(q2), q2)], axis=0)
            s = lax.dot_general(qm, kw, (((1,), (1,)), ((), ())), preferred_element_type=F32)
            for hh in range(2):
                scores.append(s[hh * ROW_TOKENS:(hh + 1) * ROW_TOKENS] + bias_ref[row_class, 2 * hp + hh])
        probs, denoms = [], []
        for s in scores:
            p = jnp.exp2(s - jnp.max(s, axis=-1, keepdims=True))
            denoms.append(jnp.sum(p, axis=-1, keepdims=True))
            probs.append(p.astype(BF16))
        for hp, lanes in enumerate(head_pairs):
            vw = v_ref[pl.ds(koff, WIN_TOKENS), lanes]
            pv = jnp.dot(jnp.concatenate(probs[2 * hp:2 * hp + 2], axis=0), vw, preferred_element_type=F32)
            outs = [pv[hh * ROW_TOKENS:(hh + 1) * ROW_TOKENS] / denoms[2 * hp + hh] for hh in range(2)]
            acc[pl.ds(qoff, ROW_TOKENS), lanes] = jnp.where(first_head, outs[0], outs[1])
        return carry

    lax.fori_loop(0, GROUP_ROWS, row_body, 0, unroll=NATTEN_UNROLL)
    o_ref[...] = _rms(acc[...], gain_ref[...]).astype(BF16)


def _attention(q, k, v, bias, gain, batch, seq_len):
    rows = seq_len // GRID_W
    q3 = q.reshape(batch, seq_len, D_MIX)
    k3 = k.reshape(batch, seq_len, D_MIX)
    v3 = v.reshape(batch, seq_len, D_MIX)
    cur = pl.BlockSpec((None, GROUP_TOKENS, D_MIX), lambda b, g: (b, g, 0))
    window = pl.BlockSpec((None, pl.Element(KV_WINDOW_ROWS * ROW_TOKENS), pl.Element(D_MIX)),
                          lambda b, g: (b, _kv_window_start(g, rows) * ROW_TOKENS, 0))
    out = pl.pallas_call(
        functools.partial(_attn_kernel, rows=rows),
        out_shape=jax.ShapeDtypeStruct((batch, seq_len, D_MIX), BF16),
        grid=(batch, rows // GROUP_ROWS),
        in_specs=[
            cur, window, window,
            pl.BlockSpec((WIN_H, NA_HEADS, GRID_W, WIN_TOKENS), lambda b, g: (0, 0, 0, 0),
                         pipeline_mode=pl.Buffered(1)),
            pl.BlockSpec((1, D_MIX), lambda b, g: (0, 0)),
        ],
        out_specs=cur,
        scratch_shapes=[pltpu.VMEM((GROUP_TOKENS, D_MIX), F32)],
        compiler_params=_cparams(2),
        name="natten",
    )(q3, k3, v3, bias, gain.reshape(1, D_MIX))
    return out.reshape(batch * seq_len, D_MIX)


def _split_hi_lo(x):
    hi = x.astype(ml_dtypes.bfloat16)
    lo = (x - hi.astype(np.float64)).astype(ml_dtypes.bfloat16)
    return hi, lo


def _stack_hi_lo(m):
    hi, lo = _split_hi_lo(m)
    return np.concatenate([hi, lo], axis=-2)


def _embed(re, im):
    return np.concatenate([np.concatenate([re, -im], axis=-1),
                           np.concatenate([im, re], axis=-1)], axis=-2)


@functools.lru_cache(maxsize=None)
def _fft_tables(seq_len):
    n = 2 * seq_len
    n2 = LANE_BLOCK
    n1 = n // n2
    i2 = np.arange(n2)[:, None, None]
    k1 = np.arange(n1)[None, :, None]
    i1 = np.arange(n1)[None, None, :]
    ang = -2.0 * np.pi * ((k1 * (n2 * i1 + i2)) % n) / n
    gr_full, gi_full = np.cos(ang), np.sin(ang)
    g_real = _stack_hi_lo(np.concatenate([gr_full, gi_full], axis=1))
    gr, gi = gr_full[..., :n1 // 2], gi_full[..., :n1 // 2]
    g_fwd = _stack_hi_lo(_embed(gr, gi))
    g_inv = _stack_hi_lo(_embed(np.swapaxes(gr, 1, 2) / n, -np.swapaxes(gi, 1, 2) / n))
    jk = np.outer(np.arange(n2), np.arange(n2))
    ang2 = -2.0 * np.pi * (jk % n2) / n2
    fr, fi = np.cos(ang2), np.sin(ang2)
    f_fwd = _stack_hi_lo(_embed(fr, fi))
    f_inv = _stack_hi_lo(_embed(fr, -fi))
    return n1, g_fwd, g_real, g_inv, f_fwd, f_inv


def _dft3(m_hl, x, m):
    x_hi = x.astype(BF16)
    x_lo = (x - x_hi.astype(F32)).astype(BF16)
    t = jnp.dot(m_hl, x_hi, preferred_element_type=F32)
    return t[:m] + t[m:] + jnp.dot(m_hl[:m], x_lo, preferred_element_type=F32)


def _stage_a_forward(x_ref, g_ref, a_ref, *, n1):
    for s in range(SLAB_GROUP):
        for i in range(FFT_NB):
            x = jnp.concatenate([_load_strided(x_ref, (0, s), i, n1 // 2),
                                 _load_strided(x_ref, (1, s), i, n1 // 2)], axis=0)
            _store_strided(a_ref.at[s], i, _dft3(g_ref[s * FFT_NB + i], x, 2 * n1))


def _stage_a_inverse(d_ref, gi_ref, y_ref, *, n1):
    for s in range(SLAB_GROUP):
        for i in range(FFT_NB):
            d = _load_strided(d_ref, (s,), i, 2 * n1)
            _store_strided(y_ref.at[s], i, _dft3(gi_ref[s * FFT_NB + i], d, n1))


def _k1_kernel(x_ref, g_ref, a_ref, *, n1):
    _stage_a_forward(x_ref, g_ref, a_ref, n1=n1)


FFT_TILES = FFT_CT // LANES


def _seq_spec(n1):
    return pl.BlockSpec((None, 2, SLAB_GROUP, FFT_TILES, n1 // 2 * FFT_NB, LANES),
                        lambda c, j, p: (p, 0, j, c, 0, 0))


def _spec_spec(n1):
    return pl.BlockSpec((None, SLAB_GROUP, FFT_TILES, 2 * n1 * FFT_NB, LANES),
                        lambda c, j, p: (p, j, c, 0, 0))


def _fft_stage_a(x6, g_fwd, n1):
    pairs, _, slabs, tiles, _, _ = x6.shape
    return pl.pallas_call(
        functools.partial(_k1_kernel, n1=n1),
        out_shape=jax.ShapeDtypeStruct((pairs, slabs, tiles, 2 * n1 * FFT_NB, LANES), F32),
        grid=(tiles // FFT_TILES, slabs // SLAB_GROUP, pairs),
        in_specs=[
            _seq_spec(n1),
            pl.BlockSpec((STEP_POSITIONS, 4 * n1, n1), lambda c, j, p: (j, 0, 0)),
        ],
        out_specs=_spec_spec(n1),
        compiler_params=_cparams(3),
        name="hy_stage_a",
    )(x6, g_fwd)


def _load_low_index(ref, part, kk):
    return jnp.concatenate([ref[:, t, part, kk].reshape(LANE_BLOCK, LANES) for t in range(ref.shape[1])],
                           axis=1)


def _store_low_index(ref, part, kk, val):
    for t in range(ref.shape[1]):
        ref[:, t, part, kk] = val[:, t * LANES:(t + 1) * LANES].reshape(SLABS, FFT_NB, LANES)


def _k2_kernel(a_ref, kf_ref, f_ref, fi_ref, d_ref, *, kb):
    n2 = LANE_BLOCK
    f_hl = f_ref[...]
    fi_hl = fi_ref[...]

    def body(group, carry):
        ks = [group * FFT_K_GROUP + u for u in range(FFT_K_GROUP)]
        spectra = [_dft3(f_hl, jnp.concatenate([_load_low_index(a_ref, 0, kk),
                                                _load_low_index(a_ref, 1, kk)], axis=0), 2 * n2)
                   for kk in ks]
        products = []
        for kk, c in zip(ks, spectra):
            cr, ci = c[:n2], c[n2:]
            kr, ki = kf_ref[0, kk], kf_ref[1, kk]
            products.append(jnp.concatenate([cr * kr - ci * ki, cr * ki + ci * kr], axis=0))
        for kk, y in zip(ks, products):
            d = _dft3(fi_hl, y, 2 * n2)
            _store_low_index(d_ref, 0, kk, d[:n2])
            _store_low_index(d_ref, 1, kk, d[n2:])
        return carry

    lax.fori_loop(0, kb // FFT_K_GROUP, body, 0)


def _fft_stage_c(a4, kf, f_fwd, f_inv, order, n1):
    pairs, slabs, tiles, _, _ = a4.shape
    n2 = LANE_BLOCK
    kb = FFT_KB
    ch_blocks = tiles // FFT_TILES
    a7 = a4.reshape(pairs, slabs, tiles, 2, n1, FFT_NB, LANES)
    spec = pl.BlockSpec((None, slabs, FFT_TILES, 2, kb, FFT_NB, LANES),
                        lambda c, k, p: (p, 0, c, 0, k, 0, 0))
    d7 = pl.pallas_call(
        functools.partial(_k2_kernel, kb=kb),
        out_shape=jax.ShapeDtypeStruct(a7.shape, F32),
        grid=(ch_blocks, n1 // kb, pairs),
        in_specs=[
            spec,
            pl.BlockSpec((2, kb, n2, FFT_CT), lambda c, k, p: (0, k, 0, order * ch_blocks + c)),
            pl.BlockSpec((4 * n2, 2 * n2), lambda c, k, p: (0, 0)),
            pl.BlockSpec((4 * n2, 2 * n2), lambda c, k, p: (0, 0)),
        ],
        out_specs=spec,
        compiler_params=_cparams(3),
        name="hy_stage_c",
    )(a7, kf, f_fwd, f_inv)
    return d7.reshape(a4.shape)


def _k3_kernel(d_ref, gi_ref, z_ref, x_ref, skip_ref, *rest, n1, forward):
    if forward:
        g_ref, o_ref, a_ref, y_ref = rest
    else:
        o_ref, y_ref = rest
    _stage_a_inverse(d_ref, gi_ref, y_ref, n1=n1)
    rows = n1 // 2 * FFT_NB
    for part in range(2):
        for s in range(SLAB_GROUP):
            for t in range(FFT_TILES):
                conv = y_ref[s, t, part * rows:(part + 1) * rows, :]
                o_ref[part, s, t] = x_ref[part, s, t] * (conv + skip_ref[t] * z_ref[part, s, t])
    if forward:
        _stage_a_forward(o_ref, g_ref, a_ref, n1=n1)


def _fft_stage_a_inverse(d4, g_inv, z6, x6, skip_row, n1, g_fwd=None):
    pairs, slabs, tiles, _, _ = d4.shape
    forward = g_fwd is not None
    in_specs = [
        _spec_spec(n1),
        pl.BlockSpec((STEP_POSITIONS, 2 * n1, 2 * n1), lambda c, j, p: (j, 0, 0)),
        _seq_spec(n1),
        _seq_spec(n1),
        pl.BlockSpec((FFT_TILES, 1, LANES), lambda c, j, p: (c, 0, 0)),
    ]
    args = [d4, g_inv, z6, x6, skip_row]
    out_shape = [jax.ShapeDtypeStruct(z6.shape, F32)]
    out_specs = [_seq_spec(n1)]
    if forward:
        in_specs.append(pl.BlockSpec((STEP_POSITIONS, 4 * n1, n1), lambda c, j, p: (j, 0, 0)))
        args.append(g_fwd)
        out_shape.append(jax.ShapeDtypeStruct(d4.shape, F32))
        out_specs.append(_spec_spec(n1))
    return pl.pallas_call(
        functools.partial(_k3_kernel, n1=n1, forward=forward),
        out_shape=out_shape,
        grid=(tiles // FFT_TILES, slabs // SLAB_GROUP, pairs),
        in_specs=in_specs,
        out_specs=out_specs,
        scratch_shapes=[pltpu.VMEM((SLAB_GROUP, FFT_TILES, n1 * FFT_NB, LANES), F32)],
        compiler_params=_cparams(3),
        name="hy_stage_a_inv_fwd" if forward else "hy_stage_a_inv",
    )(*args)


def _filt_kernel(z_ref, w1_ref, b1_ref, w2_ref, b2_ref, w3_ref, b3_ref, wo_ref, freq_ref,
                 delta_ref, h_ref, l1_ref, *, tl):
    i = pl.program_id(0)
    freq = freq_ref[...]

    def dot(a, b):
        return jnp.dot(a, b, precision=HIGHEST, preferred_element_type=F32)

    z = z_ref[...]
    h = jnp.sin(freq * (dot(z, w1_ref[...]) + b1_ref[...]))
    h = jnp.sin(freq * (dot(h, w2_ref[...]) + b2_ref[...]))
    h = jnp.sin(freq * (dot(h, w3_ref[...]) + b3_ref[...]))
    hf = dot(h, wo_ref[0]) * jnp.exp(-z[:, 0:1] * delta_ref[...])
    hb = dot(h, wo_ref[1]) * jnp.exp(-z[:, FILT_HALF:FILT_HALF + 1] * delta_ref[...])
    row = i * tl + lax.broadcasted_iota(jnp.int32, (tl, 1), 0)
    hb = jnp.where(row == 0, 0.0, hb)
    _store_slabs(h_ref.at[0], hf)
    _store_slabs(h_ref.at[1], hb)

    @pl.when(i == 0)
    def _():
        l1_ref[...] = jnp.zeros_like(l1_ref)

    l1_ref[...] += (jnp.sum(jnp.abs(hf), axis=0, keepdims=True)
                    + jnp.sum(jnp.abs(hb), axis=0, keepdims=True))


def _pad_to(x, shape):
    return jnp.pad(x, [(0, s - d) for d, s in zip(x.shape, shape)])


def _filter_taps(seq_len, w1, b1, w2, b2, w3, b3, wo, freq):
    t = jnp.linspace(0.0, 1.0, seq_len, dtype=F32)[:, None]
    w = 2.0 * math.pi * jnp.arange(seq_len, dtype=F32)[:, None] / seq_len
    f = jnp.linspace(1e-4, HY_BANDS - 1, HY_BANDS, dtype=F32)[None, :]
    z = _pad_to(jnp.concatenate([t, jnp.cos(f * w), -jnp.sin(f * w)], axis=-1), (seq_len, FILT_HALF))
    z = jnp.concatenate([z, z[::-1]], axis=1)
    deltas = jnp.abs(jnp.linspace(math.log(HY_TARGET) / HY_FAST_DECAY,
                                  math.log(HY_TARGET) / HY_SLOW_DECAY, D_MIX, dtype=F32))
    n_cols = HY_ORDER * D_MIX

    def both(m):
        m = _pad_to(m, (FILT_HALF, FILT_HALF))
        zero = jnp.zeros_like(m)
        return jnp.concatenate([jnp.concatenate([m, zero], axis=1),
                                jnp.concatenate([zero, m], axis=1)], axis=0)

    row = lambda v: jnp.tile(_pad_to(v.reshape(1, -1), (1, FILT_HALF)), (1, 2))
    wo_p = _pad_to(wo, (FILT_HALF, 2 * n_cols))
    zero = jnp.zeros((FILT_HALF, n_cols), F32)
    wo_dirs = jnp.stack([jnp.concatenate([wo_p[:, :n_cols], zero], axis=0),
                         jnp.concatenate([zero, wo_p[:, n_cols:]], axis=0)], axis=0)
    pad2 = (FILT_PAD, FILT_PAD)
    tl = FILT_TILE
    const = lambda shape: pl.BlockSpec(shape, lambda i: (0,) * len(shape))
    return pl.pallas_call(
        functools.partial(_filt_kernel, tl=tl),
        out_shape=[jax.ShapeDtypeStruct((2, SLABS, n_cols // LANES, seq_len // SLABS, LANES), F32),
                   jax.ShapeDtypeStruct((1, n_cols), F32)],
        grid=(seq_len // tl,),
        in_specs=[
            pl.BlockSpec((tl, FILT_PAD), lambda i: (i, 0)),
            const(pad2), const((1, FILT_PAD)), const(pad2), const((1, FILT_PAD)),
            const(pad2), const((1, FILT_PAD)), const((2, FILT_PAD, n_cols)), const((1, FILT_PAD)),
            const((1, n_cols)),
        ],
        out_specs=[pl.BlockSpec((2, SLABS, n_cols // LANES, tl // LANE_BLOCK * FFT_NB, LANES),
                                lambda i: (0, 0, 0, i, 0)),
                   const((1, n_cols))],
        compiler_params=_cparams(1),
        name="hy_filter_taps",
    )(z, both(w1), row(b1), both(w2), row(b2), both(w3), row(b3), wo_dirs, row(freq),
      jnp.tile(deltas, HY_ORDER).reshape(1, n_cols))


def _k2f_kernel(a_ref, l1_ref, f_ref, kf_ref, *, kb):
    n2 = LANE_BLOCK
    f_hl = f_ref[...]
    inv_l1 = 1.0 / l1_ref[...]

    def body(group, carry):
        ks = [group * FFT_K_GROUP + u for u in range(FFT_K_GROUP)]
        spectra = [_dft3(f_hl, jnp.concatenate([_load_low_index(a_ref, 0, kk),
                                                _load_low_index(a_ref, 1, kk)], axis=0), 2 * n2)
                   for kk in ks]
        for kk, c in zip(ks, spectra):
            kf_ref[0, kk] = c[:n2] * inv_l1
            kf_ref[1, kk] = c[n2:] * inv_l1
        return carry

    lax.fori_loop(0, kb // FFT_K_GROUP, body, 0)


def _filter_spectrum(a4, l1, f_fwd, n1):
    _, slabs, tiles, _, _ = a4.shape
    cols = tiles * LANES
    n2 = LANE_BLOCK
    kb, ct = FFT_KB, FFT_CT
    a6 = a4.reshape(slabs, tiles, 2, n1, FFT_NB, LANES)
    return pl.pallas_call(
        functools.partial(_k2f_kernel, kb=kb),
        out_shape=jax.ShapeDtypeStruct((2, n1, n2, cols), F32),
        grid=(cols // ct, n1 // kb),
        in_specs=[
            pl.BlockSpec((slabs, FFT_TILES, 2, kb, FFT_NB, LANES), lambda c, k: (0, c, 0, k, 0, 0)),
            pl.BlockSpec((1, ct), lambda c, k: (0, c)),
            pl.BlockSpec((4 * n2, 2 * n2), lambda c, k: (0, 0)),
        ],
        out_specs=pl.BlockSpec((2, kb, n2, ct), lambda c, k: (0, k, 0, c)),
        compiler_params=_cparams(2),
        name="hy_filter_spectrum",
    )(a6, l1, f_fwd)


def _hyena(hv, hx1, hx2, skip, filt_params, seq_len):
    n1, *tables = _fft_tables(seq_len)
    g_fwd, g_real, g_inv, f_fwd, f_inv = (jnp.asarray(m) for m in tables)
    taps, l1 = _filter_taps(seq_len, *filt_params)
    kf = _filter_spectrum(_fft_stage_a(taps[None], g_real, n1), l1, f_fwd, n1)
    as_pairs = lambda a: a.reshape((a.shape[0] // 2, 2) + a.shape[1:])
    z0, x1, x2 = as_pairs(hv), as_pairs(hx1), as_pairs(hx2)
    d = _fft_stage_c(_fft_stage_a(z0, g_fwd, n1), kf, f_fwd, f_inv, 0, n1)
    skip_rows = skip.reshape(HY_ORDER, D_MIX // LANES, 1, LANES)
    z1, a = _fft_stage_a_inverse(d, g_inv, z0, x1, skip_rows[0], n1, g_fwd=g_fwd)
    d = _fft_stage_c(a, kf, f_fwd, f_inv, 1, n1)
    (z2,) = _fft_stage_a_inverse(d, g_inv, z1, x2, skip_rows[1], n1)
    return z2.reshape(hv.shape)


def _trunk(x, mod, p, final_norm):
    batch, seq_len, _ = x.shape
    x2d = x.reshape(batch * seq_len, D_MODEL)
    x2d = _ffn1(x2d, mod, p["ffn1_norm"], p["ffn1_w_gate"], p["ffn1_w_up"], p["ffn1_w_down"], seq_len)
    q, k, v, hv, hx1, hx2 = _inproj(x2d, mod, p["mix_norm"], p["w_qkv"], p["w_hy"],
                                    p["hy_conv_w"], p["hy_conv_b"], seq_len)
    attn_n = _attention(q, k, v, p["attn_bias"], p["attn_out_norm"], batch, seq_len)
    hz = _hyena(hv, hx1, hx2, p["hy_skip"], p["hy_filter"], seq_len)
    y = _mix_ffn2(x2d, attn_n, hz, mod, p["hy_out_norm"], p["w_out_attn"], p["w_out_hy"],
                  p["ffn2_norm"], p["ffn2_w_gate"], p["ffn2_w_up"], p["ffn2_w_down"], final_norm, seq_len)
    return y.reshape(batch, seq_len, D_MODEL)


def kernel(x_prompt, x_sample, c_prompt, c_sample, w_ada, b_ada, ffn1_norm, ffn1_w_gate, ffn1_w_up,
           ffn1_w_down, mix_norm, w_in, na_rpb, hy_conv_w, hy_conv_b, hy_w1, hy_b1, hy_w2, hy_b2,
           hy_w3, hy_b3, hy_wo, hy_sin_freq, hy_skip, attn_out_norm, hy_out_norm, w_out, ffn2_norm,
           ffn2_w_gate, ffn2_w_up, ffn2_w_down, final_norm):
    assert w_ada.shape[0] == 1, "single-layer encoder"
    n_prompt = c_prompt.shape[0]
    mod_all = _ada(jnp.concatenate([c_prompt, c_sample], axis=0), w_ada[0], b_ada[0])
    mod_all = mod_all.reshape(-1, N_MOD, D_MODEL)
    bf = lambda w: w[0].astype(BF16)
    p = {
        "ffn1_norm": ffn1_norm[0], "ffn1_w_gate": bf(ffn1_w_gate), "ffn1_w_up": bf(ffn1_w_up),
        "ffn1_w_down": bf(ffn1_w_down),
        "mix_norm": mix_norm[0],
        "w_qkv": w_in[0, :, :3 * D_MIX].astype(BF16), "w_hy": w_in[0, :, 3 * D_MIX:].astype(BF16),
        "attn_bias": _attn_bias_table(na_rpb[0]),
        "hy_conv_w": hy_conv_w[0], "hy_conv_b": hy_conv_b[0],
        "hy_filter": (hy_w1[0], hy_b1[0], hy_w2[0], hy_b2[0], hy_w3[0], hy_b3[0], hy_wo[0],
                      hy_sin_freq[0]),
        "hy_skip": hy_skip[0],
        "attn_out_norm": attn_out_norm[0], "hy_out_norm": hy_out_norm[0],
        "w_out_attn": w_out[0, :D_MIX].astype(BF16), "w_out_hy": w_out[0, D_MIX:].astype(BF16),
        "ffn2_norm": ffn2_norm[0], "ffn2_w_gate": bf(ffn2_w_gate), "ffn2_w_up": bf(ffn2_w_up),
        "ffn2_w_down": bf(ffn2_w_down),
    }
    y_prompt = _trunk(x_prompt, mod_all[:n_prompt], p, final_norm)
    y_sample = _trunk(x_sample, mod_all[n_prompt:], p, final_norm)
    return (y_prompt, y_sample)
```
